```python
import jax
import jax.numpy as jnp
from jax import lax
import numpy as np

D_MODEL = 1024
BATCH = 2
SEQ = 8192
DEPTH = 4

GRID_W = 64
CTX_LEN = 256
HEAD_DIM = 64
BLOCK = 128
WINDOW = 128
ROPE_BASE = 10000.0
NORM_EPS = 1e-6
NEG_INF = -1e30

N_BRANCH = 4
BRANCH_WIDTH = 256

RW_HEADS = 4
RW_WIDTH = RW_HEADS * HEAD_DIM
RW_DECAY_LORA = 64
RW_AAA_LORA = 64
RW_MV_LORA = 32
RW_GATE_LORA = 128
RW_GN_EPS = 64e-5

SW_HEADS = 4
SW_KV_HEADS = 2

MLA_HEADS = 4
MLA_NOPE = 64
MLA_ROPE = 32
MLA_V = 64
MLA_Q_RANK = 256
MLA_KV_RANK = 128

GA_HEADS = 4
GA_KV_HEADS = 2

D_FF = 3584
N_EXPERTS = 8
TOP_K = 2
D_FF_EXPERT = 3584
MOE_BLOCK = 128

IN_WIDTHS = (4 * RW_WIDTH,
             SW_HEADS * HEAD_DIM, SW_KV_HEADS * HEAD_DIM, SW_KV_HEADS * HEAD_DIM,
             MLA_Q_RANK, MLA_KV_RANK, MLA_ROPE,
             GA_HEADS * HEAD_DIM, GA_KV_HEADS * HEAD_DIM, GA_KV_HEADS * HEAD_DIM,
             N_BRANCH * D_MODEL)
IN_WIDTH = (4 * RW_WIDTH + (SW_HEADS + 2 * SW_KV_HEADS) * HEAD_DIM
            + MLA_Q_RANK + MLA_KV_RANK + MLA_ROPE
            + (GA_HEADS + 2 * GA_KV_HEADS) * HEAD_DIM + N_BRANCH * D_MODEL)

kernel_name = 'hybrid_flow_backbone'


def rms_norm(x, g):
    xf = x.astype(jnp.float32)
    y = xf * lax.rsqrt(jnp.mean(xf * xf, axis=-1, keepdims=True) + NORM_EPS)
    return (y * g.astype(jnp.float32)).astype(x.dtype)


def split_cols(p, widths):
    out, off = [], 0
    for w in widths:
        out.append(p[..., off:off + w])
        off += w
    return out


def axial_rope_tables(n_tokens, rot_dim):
    rows = n_tokens // GRID_W
    row = jnp.repeat(jnp.arange(rows, dtype=jnp.float32), GRID_W)
    col = jnp.tile(jnp.arange(GRID_W, dtype=jnp.float32), rows)
    n_freq = rot_dim // 4
    inv_freq = ROPE_BASE ** (-jnp.arange(n_freq, dtype=jnp.float32) / n_freq)
    ang = jnp.concatenate([row[:, None] * inv_freq, col[:, None] * inv_freq], axis=-1)
    return jnp.cos(ang), jnp.sin(ang)


def apply_rope(x, cos, sin):
    half = x.shape[-1] // 2
    c = cos[None, :, None, :].astype(x.dtype)
    s = sin[None, :, None, :].astype(x.dtype)
    x1, x2 = x[..., :half], x[..., half:]
    return jnp.concatenate([x1 * c - x2 * s, x1 * s + x2 * c], axis=-1)


def context_attention(q, k, v, scale, sink=None):
    b, n = q.shape[:2]
    s = jnp.einsum('bqhgd,bkhd->bhgqk', q, k).astype(jnp.float32) * scale
    if sink is not None:
        s_sink = jnp.broadcast_to(sink[None, :, :, None, None].astype(jnp.float32), s.shape[:-1] + (1,))
        p = jax.nn.softmax(jnp.concatenate([s_sink, s], axis=-1), axis=-1)[..., 1:]
    else:
        p = jax.nn.softmax(s, axis=-1)
    o = jnp.einsum('bhgqk,bkhd->bqhgd', p.astype(v.dtype), v)
    return o.reshape(b, n, -1)


def banded_window_attention(q, k, v, k_ctx, v_ctx, sink, scale):
    b, s_len, hk, g, d = q.shape
    nb = s_len // BLOCK
    qb = q.reshape(b, nb, BLOCK, hk, g, d)

    def neighbours(t):
        tb = jnp.pad(t.reshape(b, nb, BLOCK, hk, t.shape[-1]), ((0, 0), (1, 1), (0, 0), (0, 0), (0, 0)))
        return jnp.concatenate([tb[:, :-2], tb[:, 1:-1], tb[:, 2:]], axis=2)

    kb, vb = neighbours(k), neighbours(v)
    blk = jnp.arange(nb)
    q_pos = blk[:, None] * BLOCK + jnp.arange(BLOCK)[None, :]
    k_pos = (blk[:, None] - 1) * BLOCK + jnp.arange(3 * BLOCK)[None, :]
    valid = ((jnp.abs(q_pos[:, :, None] - k_pos[:, None, :]) <= WINDOW)
             & (k_pos >= 0)[:, None, :] & (k_pos < s_len)[:, None, :])
    s_win = jnp.einsum('bnqhgd,bnkhd->bnhgqk', qb, kb).astype(jnp.float32) * scale
    s_win = jnp.where(valid[None, :, None, None], s_win, NEG_INF)
    s_ctx = jnp.einsum('bnqhgd,bchd->bnhgqc', qb, k_ctx).astype(jnp.float32) * scale
    s_sink = jnp.broadcast_to(sink[None, None, :, :, None, None].astype(jnp.float32), s_ctx.shape[:-1] + (1,))
    p = jax.nn.softmax(jnp.concatenate([s_sink, s_ctx, s_win], axis=-1), axis=-1).astype(v.dtype)
    n_ctx = k_ctx.shape[1]
    o = (jnp.einsum('bnhgqc,bchd->bnqhgd', p[..., 1:1 + n_ctx], v_ctx)
         + jnp.einsum('bnhgqk,bnkhd->bnqhgd', p[..., 1 + n_ctx:], vb))
    return o.reshape(b, s_len, -1)


def dense_block_attention(q, k, v, scale):
    b, s_len, hk, g, dq = q.shape
    nb = s_len // BLOCK
    qb = jnp.moveaxis(q.reshape(b, nb, BLOCK, hk, g, dq), 1, 0)

    def one_block(qblk):
        s = jnp.einsum('bqhgd,bkhd->bhgqk', qblk, k).astype(jnp.float32) * scale
        p = jax.nn.softmax(s, axis=-1).astype(v.dtype)
        return jnp.einsum('bhgqk,bkhd->bqhgd', p, v)

    o = lax.map(one_block, qb)
    return jnp.moveaxis(o, 0, 1).reshape(b, s_len, -1)


def gqa_heads(q, k, v, n_heads, n_kv, rope=None, q_gain=None, k_gain=None):
    b, n, _ = q.shape
    q = q.reshape(b, n, n_heads, HEAD_DIM)
    k = k.reshape(b, n, n_kv, HEAD_DIM)
    v = v.reshape(b, n, n_kv, HEAD_DIM)
    if q_gain is not None:
        q = rms_norm(q, q_gain)
        k = rms_norm(k, k_gain)
    if rope is not None:
        q = apply_rope(q, rope[0], rope[1])
        k = apply_rope(k, rope[0], rope[1])
    return q.reshape(b, n, n_kv, n_heads // n_kv, HEAD_DIM), k, v


def window_mixer(pc, pl, sink, rope, ctx_out):
    scale = HEAD_DIM ** -0.5
    qc, kc, vc = gqa_heads(pc[0], pc[1], pc[2], SW_HEADS, SW_KV_HEADS)
    ql, kl, vl = gqa_heads(pl[0], pl[1], pl[2], SW_HEADS, SW_KV_HEADS, rope)
    sink = sink.reshape(SW_KV_HEADS, SW_HEADS // SW_KV_HEADS)
    y_lat = banded_window_attention(ql, kl, vl, kc, vc, sink, scale)
    y_ctx = context_attention(qc, kc, vc, scale, sink) if ctx_out else None
    return y_ctx, y_lat


def grid_attention_mixer(pc, pl, q_gain, k_gain, rope, ctx_out):
    scale = HEAD_DIM ** -0.5
    qc, kc, vc = gqa_heads(pc[0], pc[1], pc[2], GA_HEADS, GA_KV_HEADS, None, q_gain, k_gain)
    ql, kl, vl = gqa_heads(pl[0], pl[1], pl[2], GA_HEADS, GA_KV_HEADS, rope, q_gain, k_gain)
    y_lat = dense_block_attention(ql, jnp.concatenate([kc, kl], axis=1), jnp.concatenate([vc, vl], axis=1), scale)
    y_ctx = context_attention(qc, kc, vc, scale) if ctx_out else None
    return y_ctx, y_lat


def mla_mixer(pc, pl, q_gain, w_uq, kv_gain, w_ukv, rope, ctx_out):
    scale = (MLA_NOPE + MLA_ROPE) ** -0.5

    def project(cq, ckv, k_rope, rope_tab):
        b, n, _ = cq.shape
        q = (rms_norm(cq, q_gain) @ w_uq).reshape(b, n, MLA_HEADS, MLA_NOPE + MLA_ROPE)
        kv = (rms_norm(ckv, kv_gain) @ w_ukv).reshape(b, n, MLA_HEADS, MLA_NOPE + MLA_V)
        q_nope, q_pe = q[..., :MLA_NOPE], q[..., MLA_NOPE:]
        k_nope, v = kv[..., :MLA_NOPE], kv[..., MLA_NOPE:]
        k_pe = k_rope[:, :, None, :]
        if rope_tab is not None:
            q_pe = apply_rope(q_pe, rope_tab[0], rope_tab[1])
            k_pe = apply_rope(k_pe, rope_tab[0], rope_tab[1])
        q = jnp.concatenate([q_nope, q_pe], axis=-1)
        k = jnp.concatenate([k_nope, jnp.broadcast_to(k_pe, (b, n, MLA_HEADS, MLA_ROPE))], axis=-1)
        return q[:, :, :, None, :], k, v

    qc, kc, vc = project(pc[0], pc[1], pc[2], None)
    ql, kl, vl = project(pl[0], pl[1], pl[2], rope)
    y_lat = dense_block_attention(ql, jnp.concatenate([kc, kl], axis=1), jnp.concatenate([vc, vl], axis=1), scale)
    y_ctx = context_attention(qc, kc, vc, scale) if ctx_out else None
    return y_ctx, y_lat


def centred_token_shift(f, mu):
    prev = jnp.pad(f[:, :-1], ((0, 0), (1, 0), (0, 0)))
    nxt = jnp.pad(f[:, 1:], ((0, 0), (0, 1), (0, 0)))
    return f + mu[0] * (prev - f) + mu[1] * (nxt - f)


def rwkv7_scan(r, decay, k, v, a, b, state0, reverse):
    xs = tuple(jnp.moveaxis(t.astype(jnp.float32), 1, 0) for t in (r, decay, k, v, a, b))

    def step(state, inp):
        r_t, w_t, k_t, v_t, a_t, b_t = inp
        sa = jnp.einsum('bhvk,bhk->bhv', state, a_t)
        state = (state * w_t[:, :, None, :] + sa[..., None] * b_t[:, :, None, :]
                 + v_t[..., None] * k_t[:, :, None, :])
        return state, jnp.einsum('bhvk,bhk->bhv', state, r_t)

    final, y = lax.scan(step, state0, xs, reverse=reverse)
    return final, jnp.moveaxis(y, 0, 1)


def rwkv7_mixer(f_ctx, f_lat, vf_ctx, vf_lat, rw, ctx_out):
    def heads(t):
        return t.reshape(t.shape[:-1] + (RW_HEADS, HEAD_DIM))

    r_k = rw['r_k'].reshape(RW_HEADS, HEAD_DIM)

    def features(f, vf):
        f = centred_token_shift(f, rw['mu'])
        r, k, v, z = jnp.split(f, 4, axis=-1)
        if vf is None:
            vf = v
        else:
            v = v + (vf - v) * jax.nn.sigmoid(rw['v0'] + (z @ rw['v1']) @ rw['v2'])
        kk = heads(k * rw['k_k']).astype(jnp.float32)
        kk = (kk * lax.rsqrt(jnp.sum(kk * kk, axis=-1, keepdims=True) + 1e-12)).astype(f.dtype)
        g = jax.nn.sigmoid(z @ rw['g1']) @ rw['g2']
        per_dir = []
        for d in range(2):
            w = -jax.nn.softplus(-(rw['w0'][d] + jnp.tanh(z @ rw['w1'][d]) @ rw['w2'][d])) - 0.5
            decay = jnp.exp(-jnp.exp(w.astype(jnp.float32)))
            a = jax.nn.sigmoid(rw['a0'][d] + (z @ rw['a1'][d]) @ rw['a2'][d])
            k_d = k * (1 + (a - 1) * rw['k_a'])
            per_dir.append((heads(decay), heads(a), heads(k_d)))
        return heads(r), heads(v), kk, g, per_dir, vf

    r_c, v_c, kk_c, g_c, dir_c, vf_ctx = features(f_ctx, vf_ctx)
    r_l, v_l, kk_l, g_l, dir_l, vf_lat = features(f_lat, vf_lat)
    state0 = jnp.zeros((f_lat.shape[0], RW_HEADS, HEAD_DIM, HEAD_DIM), jnp.float32)
    y_c, y_l, bonus_c, bonus_l = 0.0, 0.0, 0.0, 0.0
    for d in range(2):
        rev = d == 1
        dec, a, k_d = dir_c[d]
        s_ctx, y = rwkv7_scan(r_c, dec, k_d, v_c, -kk_c, kk_c * a, state0, rev)
        if ctx_out:
            y_c = y_c + y
            bonus_c = bonus_c + jnp.sum(r_c * k_d * r_k, axis=-1, keepdims=True) * v_c
        dec, a, k_d = dir_l[d]
        _, y = rwkv7_scan(r_l, dec, k_d, v_l, -kk_l, kk_l * a, s_ctx, rev)
        y_l = y_l + y
        bonus_l = bonus_l + jnp.sum(r_l * k_d * r_k, axis=-1, keepdims=True) * v_l

    ln_w = rw['ln_w'].reshape(RW_HEADS, HEAD_DIM).astype(jnp.float32)
    ln_b = rw['ln_b'].reshape(RW_HEADS, HEAD_DIM).astype(jnp.float32)

    def finish(y, bonus, g):
        mu = jnp.mean(y, axis=-1, keepdims=True)
        var = jnp.mean(jnp.square(y - mu), axis=-1, keepdims=True)
        yn = (y - mu) * lax.rsqrt(var + RW_GN_EPS) * ln_w + ln_b
        o = yn.astype(g.dtype) + bonus
        return o.reshape(g.shape) * g

    y_lat = finish(y_l, bonus_l, g_l)
    y_ctx = finish(y_c, bonus_c, g_c) if ctx_out else None
    return y_ctx, y_lat, vf_ctx, vf_lat


def hybrid_token_mixer(h_ctx, h_lat, vf_ctx, vf_lat, w_in, w_branch, w_out, rw, sw_sink,
                       mla_q_norm, mla_w_uq, mla_kv_norm, mla_w_ukv, ga_q_norm, ga_k_norm,
                       rope64, rope32, ctx_out):
    pc = split_cols(h_ctx @ w_in, IN_WIDTHS)
    pl = split_cols(h_lat @ w_in, IN_WIDTHS)
    ya_c, ya_l, vf_ctx, vf_lat = rwkv7_mixer(pc[0], pl[0], vf_ctx, vf_lat, rw, ctx_out)
    yb_c, yb_l = window_mixer(pc[1:4], pl[1:4], sw_sink, rope64, ctx_out)
    yc_c, yc_l = mla_mixer(pc[4:7], pl[4:7], mla_q_norm, mla_w_uq, mla_kv_norm, mla_w_ukv, rope32, ctx_out)
    yd_c, yd_l = grid_attention_mixer(pc[7:10], pl[7:10], ga_q_norm, ga_k_norm, rope64, ctx_out)

    def merge(ys, gate_cols):
        gates = jax.nn.sigmoid(gate_cols.astype(jnp.float32)).astype(gate_cols.dtype)
        m = None
        for i, y in enumerate(ys):
            term = gates[..., i * D_MODEL:(i + 1) * D_MODEL] * (y @ w_branch[i])
            m = term if m is None else m + term
        return m @ w_out

    out_lat = merge([ya_l, yb_l, yc_l, yd_l], pl[10])
    out_ctx = merge([ya_c, yb_c, yc_c, yd_c], pc[10]) if ctx_out else None
    return out_ctx, out_lat, vf_ctx, vf_lat


def swiglu(h, w1, w3, w2):
    return (jax.nn.silu(h @ w1) * (h @ w3)) @ w2


def moe_swiglu(h, w_router, w1, w3, w2):
    n_tok, d = h.shape
    logits = (h @ w_router).astype(jnp.float32)
    top_logit, top_idx = lax.top_k(logits, TOP_K)
    gates = jax.nn.softmax(top_logit, axis=-1)
    n_assign = n_tok * TOP_K
    flat_e = top_idx.reshape(-1)
    order = jnp.argsort(flat_e)
    sorted_e = flat_e[order]
    sorted_tok = (order // TOP_K).astype(jnp.int32)
    sorted_gate = gates.reshape(-1)[order]
    counts = jnp.bincount(flat_e, length=N_EXPERTS)
    padded = (counts + MOE_BLOCK - 1) // MOE_BLOCK * MOE_BLOCK
    start = jnp.cumsum(counts) - counts
    pend = jnp.cumsum(padded)
    pstart = pend - padded
    dest = pstart[sorted_e] + jnp.arange(n_assign) - start[sorted_e]
    n_blocks = -(-n_assign // MOE_BLOCK) + N_EXPERTS
    n_rows = n_blocks * MOE_BLOCK
    tok = jnp.zeros((n_rows,), jnp.int32).at[dest].set(sorted_tok)
    gate = jnp.zeros((n_rows,), h.dtype).at[dest].set(sorted_gate.astype(h.dtype))
    block_e = jnp.minimum(jnp.searchsorted(pend, jnp.arange(n_blocks) * MOE_BLOCK, side='right'), N_EXPERTS - 1)
    xb = h[tok].reshape(n_blocks, MOE_BLOCK, d)

    def expert_block(args):
        xblk, e = args
        return (jax.nn.silu(xblk @ w1[e]) * (xblk @ w3[e])) @ w2[e]

    yb = lax.map(expert_block, (xb, block_e)).reshape(n_rows, d)
    return jnp.zeros_like(h).at[tok].add(yb * gate[:, None])


def setup_inputs(seed: int = 0) -> dict:
    key = jax.random.key(seed)
    keys = iter(jax.random.split(key, 64))
    f32 = jnp.float32

    def normal(shape, scale=1.0):
        return jax.random.normal(next(keys), shape, f32) * scale

    def gain(shape):
        return 1.0 + normal(shape, 0.05)

    L, D, W = DEPTH, D_MODEL, RW_WIDTH
    n_dense, n_moe = (DEPTH + 1) // 2, DEPTH // 2
    return {
        'x': normal((BATCH, SEQ, D)),
        'c': normal((BATCH, D)),
        'ctx': normal((BATCH, CTX_LEN, D)),
        'c_ctx': normal((D,)),
        'w_mod': normal((L, D, 6 * D), 0.02),
        'b_mod': normal((L, 6 * D), 0.02),
        'norm1': gain((L, D)),
        'norm2': gain((L, D)),
        'w_in': normal((L, D, IN_WIDTH), D ** -0.5),
        'w_branch': normal((L, N_BRANCH, BRANCH_WIDTH, D), BRANCH_WIDTH ** -0.5),
        'w_out': normal((L, D, D), D ** -0.5),
        'rw_mu': jax.random.uniform(next(keys), (L, 2, 4 * W), f32, 0.0, 0.5),
        'rw_w0': jnp.linspace(-6.0, -1.0, W, dtype=f32)[None, None, :] + normal((L, 2, W), 0.1),
        'rw_w1': normal((L, 2, W, RW_DECAY_LORA), W ** -0.5),
        'rw_w2': normal((L, 2, RW_DECAY_LORA, W), 0.3 * RW_DECAY_LORA ** -0.5),
        'rw_a0': normal((L, 2, W), 0.1),
        'rw_a1': normal((L, 2, W, RW_AAA_LORA), W ** -0.5),
        'rw_a2': normal((L, 2, RW_AAA_LORA, W), 0.5 * RW_AAA_LORA ** -0.5),
        'rw_v0': normal((L - 1, W), 0.3),
        'rw_v1': normal((L - 1, W, RW_MV_LORA), W ** -0.5),
        'rw_v2': normal((L - 1, RW_MV_LORA, W), 0.5 * RW_MV_LORA ** -0.5),
        'rw_g1': normal((L, W, RW_GATE_LORA), W ** -0.5),
        'rw_g2': normal((L, RW_GATE_LORA, W), RW_GATE_LORA ** -0.5),
        'rw_k_k': 0.85 + normal((L, W), 0.05),
        'rw_k_a': gain((L, W)),
        'rw_r_k': normal((L, W), 0.1),
        'rw_ln_w': gain((L, W)),
        'rw_ln_b': normal((L, W), 0.02),
        'sw_sink': normal((L, SW_HEADS), 0.5),
        'mla_q_norm': gain((L, MLA_Q_RANK)),
        'mla_w_uq': normal((L, MLA_Q_RANK, MLA_HEADS * (MLA_NOPE + MLA_ROPE)), MLA_Q_RANK ** -0.5),
        'mla_kv_norm': gain((L, MLA_KV_RANK)),
        'mla_w_ukv': normal((L, MLA_KV_RANK, MLA_HEADS * (MLA_NOPE + MLA_V)), MLA_KV_RANK ** -0.5),
        'ga_q_norm': gain((L, HEAD_DIM)),
        'ga_k_norm': gain((L, HEAD_DIM)),
        'ffn_w1': normal((n_dense, D, D_FF), D ** -0.5),
        'ffn_w3': normal((n_dense, D, D_FF), D ** -0.5),
        'ffn_w2': normal((n_dense, D_FF, D), D_FF ** -0.5),
        'moe_router': normal((n_moe, D, N_EXPERTS), D ** -0.5),
        'moe_w1': normal((n_moe, N_EXPERTS, D, D_FF_EXPERT), D ** -0.5),
        'moe_w3': normal((n_moe, N_EXPERTS, D, D_FF_EXPERT), D ** -0.5),
        'moe_w2': normal((n_moe, N_EXPERTS, D_FF_EXPERT, D), D_FF_EXPERT ** -0.5),
        'final_norm': gain((D,)),
    }


def reference(x, c, ctx, c_ctx, w_mod, b_mod, norm1, norm2, w_in, w_branch, w_out,
              rw_mu, rw_w0, rw_w1, rw_w2, rw_a0, rw_a1, rw_a2, rw_v0, rw_v1, rw_v2,
              rw_g1, rw_g2, rw_k_k, rw_k_a, rw_r_k, rw_ln_w, rw_ln_b, sw_sink,
              mla_q_norm, mla_w_uq, mla_kv_norm, mla_w_ukv, ga_q_norm, ga_k_norm,
              ffn_w1, ffn_w3, ffn_w2, moe_router, moe_w1, moe_w3, moe_w2, final_norm):
    b, s_len, d = x.shape
    n_ctx_tok = ctx.shape[1]
    rope64 = axial_rope_tables(s_len, HEAD_DIM)
    rope32 = axial_rope_tables(s_len, MLA_ROPE)
    silu_c = jax.nn.silu(c)
    silu_cc = jax.nn.silu(c_ctx)
    x_lat, x_ctx = x, ctx
    vf_lat, vf_ctx = None, None
    for l in range(DEPTH):
        ctx_out = l < DEPTH - 1
        mod_l = silu_c @ w_mod[l] + b_mod[l]
        mod_c = silu_cc @ w_mod[l] + b_mod[l]
        sh1, sc1, g1, sh2, sc2, g2 = jnp.split(mod_l[:, None, :], 6, axis=-1)
        csh1, csc1, cg1, csh2, csc2, cg2 = jnp.split(mod_c, 6, axis=-1)

        h_lat = rms_norm(x_lat, norm1[l]) * (1 + sc1) + sh1
        h_ctx = rms_norm(x_ctx, norm1[l]) * (1 + csc1) + csh1
        rw = dict(mu=rw_mu[l], w0=rw_w0[l], w1=rw_w1[l], w2=rw_w2[l],
                  a0=rw_a0[l], a1=rw_a1[l], a2=rw_a2[l],
                  v0=rw_v0[l - 1] if l > 0 else None,
                  v1=rw_v1[l - 1] if l > 0 else None,
                  v2=rw_v2[l - 1] if l > 0 else None,
                  g1=rw_g1[l], g2=rw_g2[l], k_k=rw_k_k[l], k_a=rw_k_a[l], r_k=rw_r_k[l],
                  ln_w=rw_ln_w[l], ln_b=rw_ln_b[l])
        out_ctx, out_lat, vf_ctx, vf_lat = hybrid_token_mixer(
            h_ctx, h_lat, vf_ctx, vf_lat, w_in[l], w_branch[l], w_out[l], rw, sw_sink[l],
            mla_q_norm[l], mla_w_uq[l], mla_kv_norm[l], mla_w_ukv[l], ga_q_norm[l], ga_k_norm[l],
            rope64, rope32, ctx_out)
        x_lat = x_lat + g1 * out_lat

        f_lat = rms_norm(x_lat, norm2[l]) * (1 + sc2) + sh2
        if ctx_out:
            x_ctx = x_ctx + cg1 * out_ctx
            f_ctx = rms_norm(x_ctx, norm2[l]) * (1 + csc2) + csh2
            tokens = jnp.concatenate([f_ctx.reshape(-1, d), f_lat.reshape(-1, d)], axis=0)
            n_c = b * n_ctx_tok
        else:
            tokens = f_lat.reshape(-1, d)
            n_c = 0
        if l % 2 == 0:
            ffn_out = swiglu(tokens, ffn_w1[l // 2], ffn_w3[l // 2], ffn_w2[l // 2])
        else:
            ffn_out = moe_swiglu(tokens, moe_router[l // 2], moe_w1[l // 2], moe_w3[l // 2], moe_w2[l // 2])
        x_lat = x_lat + g2 * ffn_out[n_c:].reshape(b, s_len, d)
        if ctx_out:
            x_ctx = x_ctx + cg2 * ffn_out[:n_c].reshape(b, n_ctx_tok, d)
    return rms_norm(x_lat, final_norm)
```

```python
import functools

import jax
import jax.numpy as jnp
from jax import lax
from jax.experimental import pallas as pl
from jax.experimental.pallas import tpu as pltpu

D_MODEL = 1024
DEPTH = 4
GRID_W = 64
HEAD_DIM = 64
BLOCK = 128
WINDOW = 128
ROPE_BASE = 10000.0
NORM_EPS = 1e-6
NEG_INF = -1e30
N_BRANCH = 4
BRANCH_WIDTH = 256
RW_HEADS = 4
RW_WIDTH = RW_HEADS * HEAD_DIM
RW_GN_EPS = 64e-5
SW_HEADS = 4
SW_KV_HEADS = 2
MLA_HEADS = 4
MLA_NOPE = 64
MLA_ROPE = 32
MLA_V = 64
MLA_Q_RANK = 256
MLA_KV_RANK = 128
GA_HEADS = 4
GA_KV_HEADS = 2
N_EXPERTS = 8
TOP_K = 2
MOE_BLOCK = 128
IN_WIDTHS = (4 * RW_WIDTH,
             SW_HEADS * HEAD_DIM, SW_KV_HEADS * HEAD_DIM, SW_KV_HEADS * HEAD_DIM,
             MLA_Q_RANK, MLA_KV_RANK, MLA_ROPE,
             GA_HEADS * HEAD_DIM, GA_KV_HEADS * HEAD_DIM, GA_KV_HEADS * HEAD_DIM,
             N_BRANCH * D_MODEL)


def _mm_kernel(a_ref, b_ref, o_ref):
    o_ref[...] = jnp.dot(a_ref[...], b_ref[...], preferred_element_type=jnp.float32)


def _round_up(x, m):
    return (x + m - 1) // m * m


def pmm(a, b, tm=512, tn=512):
    m, k = a.shape
    _, n = b.shape
    tm = min(tm, _round_up(m, 8))
    tn = min(tn, _round_up(n, 128))
    mp, np_ = _round_up(m, tm), _round_up(n, tn)
    a = a.astype(jnp.bfloat16)
    b = b.astype(jnp.bfloat16)
    if mp != m:
        a = jnp.pad(a, ((0, mp - m), (0, 0)))
    if np_ != n:
        b = jnp.pad(b, ((0, 0), (0, np_ - n)))
    out = pl.pallas_call(
        _mm_kernel,
        grid=(np_ // tn, mp // tm),
        in_specs=[pl.BlockSpec((tm, k), lambda j, i: (i, 0)),
                  pl.BlockSpec((k, tn), lambda j, i: (0, j))],
        out_specs=pl.BlockSpec((tm, tn), lambda j, i: (i, j)),
        out_shape=jax.ShapeDtypeStruct((mp, np_), jnp.float32),
        compiler_params=pltpu.CompilerParams(
            dimension_semantics=("arbitrary", "arbitrary"),
            vmem_limit_bytes=48 * 1024 * 1024),
        name="pmm",
    )(a, b)
    return out[:m, :n]


def pmm_nd(a, b, **kw):
    lead = a.shape[:-1]
    return pmm(a.reshape(-1, a.shape[-1]), b, **kw).reshape(lead + (b.shape[-1],))


def rms_norm(x, g):
    xf = x.astype(jnp.float32)
    y = xf * lax.rsqrt(jnp.mean(xf * xf, axis=-1, keepdims=True) + NORM_EPS)
    return (y * g.astype(jnp.float32)).astype(x.dtype)


def split_cols(p, widths):
    out, off = [], 0
    for w in widths:
        out.append(p[..., off:off + w])
        off += w
    return out


def axial_rope_tables(n_tokens, rot_dim):
    rows = n_tokens // GRID_W
    row = jnp.repeat(jnp.arange(rows, dtype=jnp.float32), GRID_W)
    col = jnp.tile(jnp.arange(GRID_W, dtype=jnp.float32), rows)
    n_freq = rot_dim // 4
    inv_freq = ROPE_BASE ** (-jnp.arange(n_freq, dtype=jnp.float32) / n_freq)
    ang = jnp.concatenate([row[:, None] * inv_freq, col[:, None] * inv_freq], axis=-1)
    return jnp.cos(ang), jnp.sin(ang)


def apply_rope(x, cos, sin):
    half = x.shape[-1] // 2
    c = cos[None, :, None, :].astype(x.dtype)
    s = sin[None, :, None, :].astype(x.dtype)
    x1, x2 = x[..., :half], x[..., half:]
    return jnp.concatenate([x1 * c - x2 * s, x1 * s + x2 * c], axis=-1)


def context_attention(q, k, v, scale, sink=None):
    b, n = q.shape[:2]
    s = jnp.einsum('bqhgd,bkhd->bhgqk', q, k).astype(jnp.float32) * scale
    if sink is not None:
        s_sink = jnp.broadcast_to(sink[None, :, :, None, None].astype(jnp.float32), s.shape[:-1] + (1,))
        p = jax.nn.softmax(jnp.concatenate([s_sink, s], axis=-1), axis=-1)[..., 1:]
    else:
        p = jax.nn.softmax(s, axis=-1)
    o = jnp.einsum('bhgqk,bkhd->bqhgd', p.astype(v.dtype), v)
    return o.reshape(b, n, -1)


def banded_window_attention(q, k, v, k_ctx, v_ctx, sink, scale):
    b, s_len, hk, g, d = q.shape
    nb = s_len // BLOCK
    qb = q.reshape(b, nb, BLOCK, hk, g, d)

    def neighbours(t):
        tb = jnp.pad(t.reshape(b, nb, BLOCK, hk, t.shape[-1]), ((0, 0), (1, 1), (0, 0), (0, 0), (0, 0)))
        return jnp.concatenate([tb[:, :-2], tb[:, 1:-1], tb[:, 2:]], axis=2)

    kb, vb = neighbours(k), neighbours(v)
    blk = jnp.arange(nb)
    q_pos = blk[:, None] * BLOCK + jnp.arange(BLOCK)[None, :]
    k_pos = (blk[:, None] - 1) * BLOCK + jnp.arange(3 * BLOCK)[None, :]
    valid = ((jnp.abs(q_pos[:, :, None] - k_pos[:, None, :]) <= WINDOW)
             & (k_pos >= 0)[:, None, :] & (k_pos < s_len)[:, None, :])
    s_win = jnp.einsum('bnqhgd,bnkhd->bnhgqk', qb, kb).astype(jnp.float32) * scale
    s_win = jnp.where(valid[None, :, None, None], s_win, NEG_INF)
    s_ctx = jnp.einsum('bnqhgd,bchd->bnhgqc', qb, k_ctx).astype(jnp.float32) * scale
    s_sink = jnp.broadcast_to(sink[None, None, :, :, None, None].astype(jnp.float32), s_ctx.shape[:-1] + (1,))
    p = jax.nn.softmax(jnp.concatenate([s_sink, s_ctx, s_win], axis=-1), axis=-1).astype(v.dtype)
    n_ctx = k_ctx.shape[1]
    o = (jnp.einsum('bnhgqc,bchd->bnqhgd', p[..., 1:1 + n_ctx], v_ctx)
         + jnp.einsum('bnhgqk,bnkhd->bnqhgd', p[..., 1 + n_ctx:], vb))
    return o.reshape(b, s_len, -1)


def dense_block_attention(q, k, v, scale):
    b, s_len, hk, g, dq = q.shape
    nb = s_len // BLOCK
    qb = jnp.moveaxis(q.reshape(b, nb, BLOCK, hk, g, dq), 1, 0)

    def one_block(qblk):
        s = jnp.einsum('bqhgd,bkhd->bhgqk', qblk, k).astype(jnp.float32) * scale
        p = jax.nn.softmax(s, axis=-1).astype(v.dtype)
        return jnp.einsum('bhgqk,bkhd->bqhgd', p, v)

    o = lax.map(one_block, qb)
    return jnp.moveaxis(o, 0, 1).reshape(b, s_len, -1)


def gqa_heads(q, k, v, n_heads, n_kv, rope=None, q_gain=None, k_gain=None):
    b, n, _ = q.shape
    q = q.reshape(b, n, n_heads, HEAD_DIM)
    k = k.reshape(b, n, n_kv, HEAD_DIM)
    v = v.reshape(b, n, n_kv, HEAD_DIM)
    if q_gain is not None:
        q = rms_norm(q, q_gain)
        k = rms_norm(k, k_gain)
    if rope is not None:
        q = apply_rope(q, rope[0], rope[1])
        k = apply_rope(k, rope[0], rope[1])
    return q.reshape(b, n, n_kv, n_heads // n_kv, HEAD_DIM), k, v


def window_mixer(pc, pl_, sink, rope, ctx_out):
    scale = HEAD_DIM ** -0.5
    qc, kc, vc = gqa_heads(pc[0], pc[1], pc[2], SW_HEADS, SW_KV_HEADS)
    ql, kl, vl = gqa_heads(pl_[0], pl_[1], pl_[2], SW_HEADS, SW_KV_HEADS, rope)
    sink = sink.reshape(SW_KV_HEADS, SW_HEADS // SW_KV_HEADS)
    y_lat = banded_window_attention(ql, kl, vl, kc, vc, sink, scale)
    y_ctx = context_attention(qc, kc, vc, scale, sink) if ctx_out else None
    return y_ctx, y_lat


def grid_attention_mixer(pc, pl_, q_gain, k_gain, rope, ctx_out):
    scale = HEAD_DIM ** -0.5
    qc, kc, vc = gqa_heads(pc[0], pc[1], pc[2], GA_HEADS, GA_KV_HEADS, None, q_gain, k_gain)
    ql, kl, vl = gqa_heads(pl_[0], pl_[1], pl_[2], GA_HEADS, GA_KV_HEADS, rope, q_gain, k_gain)
    y_lat = dense_block_attention(ql, jnp.concatenate([kc, kl], axis=1), jnp.concatenate([vc, vl], axis=1), scale)
    y_ctx = context_attention(qc, kc, vc, scale) if ctx_out else None
    return y_ctx, y_lat


def mla_mixer(pc, pl_, q_gain, w_uq, kv_gain, w_ukv, rope, ctx_out):
    scale = (MLA_NOPE + MLA_ROPE) ** -0.5

    def project(cq, ckv, k_rope, rope_tab):
        b, n, _ = cq.shape
        q = (rms_norm(cq, q_gain) @ w_uq).reshape(b, n, MLA_HEADS, MLA_NOPE + MLA_ROPE)
        kv = (rms_norm(ckv, kv_gain) @ w_ukv).reshape(b, n, MLA_HEADS, MLA_NOPE + MLA_V)
        q_nope, q_pe = q[..., :MLA_NOPE], q[..., MLA_NOPE:]
        k_nope, v = kv[..., :MLA_NOPE], kv[..., MLA_NOPE:]
        k_pe = k_rope[:, :, None, :]
        if rope_tab is not None:
            q_pe = apply_rope(q_pe, rope_tab[0], rope_tab[1])
            k_pe = apply_rope(k_pe, rope_tab[0], rope_tab[1])
        q = jnp.concatenate([q_nope, q_pe], axis=-1)
        k = jnp.concatenate([k_nope, jnp.broadcast_to(k_pe, (b, n, MLA_HEADS, MLA_ROPE))], axis=-1)
        return q[:, :, :, None, :], k, v

    qc, kc, vc = project(pc[0], pc[1], pc[2], None)
    ql, kl, vl = project(pl_[0], pl_[1], pl_[2], rope)
    y_lat = dense_block_attention(ql, jnp.concatenate([kc, kl], axis=1), jnp.concatenate([vc, vl], axis=1), scale)
    y_ctx = context_attention(qc, kc, vc, scale) if ctx_out else None
    return y_ctx, y_lat


def centred_token_shift(f, mu):
    prev = jnp.pad(f[:, :-1], ((0, 0), (1, 0), (0, 0)))
    nxt = jnp.pad(f[:, 1:], ((0, 0), (0, 1), (0, 0)))
    return f + mu[0] * (prev - f) + mu[1] * (nxt - f)


def rwkv7_scan(r, decay, k, v, a, b, state0, reverse):
    xs = tuple(jnp.moveaxis(t.astype(jnp.float32), 1, 0) for t in (r, decay, k, v, a, b))

    def step(state, inp):
        r_t, w_t, k_t, v_t, a_t, b_t = inp
        sa = jnp.einsum('bhvk,bhk->bhv', state, a_t)
        state = (state * w_t[:, :, None, :] + sa[..., None] * b_t[:, :, None, :]
                 + v_t[..., None] * k_t[:, :, None, :])
        return state, jnp.einsum('bhvk,bhk->bhv', state, r_t)

    final, y = lax.scan(step, state0, xs, reverse=reverse)
    return final, jnp.moveaxis(y, 0, 1)


def rwkv7_mixer(f_ctx, f_lat, vf_ctx, vf_lat, rw, ctx_out):
    def heads(t):
        return t.reshape(t.shape[:-1] + (RW_HEADS, HEAD_DIM))

    r_k = rw['r_k'].reshape(RW_HEADS, HEAD_DIM)

    def features(f, vf):
        f = centred_token_shift(f, rw['mu'])
        r, k, v, z = jnp.split(f, 4, axis=-1)
        if vf is None:
            vf = v
        else:
            v = v + (vf - v) * jax.nn.sigmoid(rw['v0'] + (z @ rw['v1']) @ rw['v2'])
        kk = heads(k * rw['k_k']).astype(jnp.float32)
        kk = (kk * lax.rsqrt(jnp.sum(kk * kk, axis=-1, keepdims=True) + 1e-12)).astype(f.dtype)
        g = jax.nn.sigmoid(z @ rw['g1']) @ rw['g2']
        per_dir = []
        for d in range(2):
            w = -jax.nn.softplus(-(rw['w0'][d] + jnp.tanh(z @ rw['w1'][d]) @ rw['w2'][d])) - 0.5
            decay = jnp.exp(-jnp.exp(w.astype(jnp.float32)))
            a = jax.nn.sigmoid(rw['a0'][d] + (z @ rw['a1'][d]) @ rw['a2'][d])
            k_d = k * (1 + (a - 1) * rw['k_a'])
            per_dir.append((heads(decay), heads(a), heads(k_d)))
        return heads(r), heads(v), kk, g, per_dir, vf

    r_c, v_c, kk_c, g_c, dir_c, vf_ctx = features(f_ctx, vf_ctx)
    r_l, v_l, kk_l, g_l, dir_l, vf_lat = features(f_lat, vf_lat)
    state0 = jnp.zeros((f_lat.shape[0], RW_HEADS, HEAD_DIM, HEAD_DIM), jnp.float32)
    y_c, y_l, bonus_c, bonus_l = 0.0, 0.0, 0.0, 0.0
    for d in range(2):
        rev = d == 1
        dec, a, k_d = dir_c[d]
        s_ctx, y = rwkv7_scan(r_c, dec, k_d, v_c, -kk_c, kk_c * a, state0, rev)
        if ctx_out:
            y_c = y_c + y
            bonus_c = bonus_c + jnp.sum(r_c * k_d * r_k, axis=-1, keepdims=True) * v_c
        dec, a, k_d = dir_l[d]
        _, y = rwkv7_scan(r_l, dec, k_d, v_l, -kk_l, kk_l * a, s_ctx, rev)
        y_l = y_l + y
        bonus_l = bonus_l + jnp.sum(r_l * k_d * r_k, axis=-1, keepdims=True) * v_l

    ln_w = rw['ln_w'].reshape(RW_HEADS, HEAD_DIM).astype(jnp.float32)
    ln_b = rw['ln_b'].reshape(RW_HEADS, HEAD_DIM).astype(jnp.float32)

    def finish(y, bonus, g):
        mu = jnp.mean(y, axis=-1, keepdims=True)
        var = jnp.mean(jnp.square(y - mu), axis=-1, keepdims=True)
        yn = (y - mu) * lax.rsqrt(var + RW_GN_EPS) * ln_w + ln_b
        o = yn.astype(g.dtype) + bonus
        return o.reshape(g.shape) * g

    y_lat = finish(y_l, bonus_l, g_l)
    y_ctx = finish(y_c, bonus_c, g_c) if ctx_out else None
    return y_ctx, y_lat, vf_ctx, vf_lat


def hybrid_token_mixer(h_ctx, h_lat, vf_ctx, vf_lat, w_in, w_branch, w_out, rw, sw_sink,
                       mla_q_norm, mla_w_uq, mla_kv_norm, mla_w_ukv, ga_q_norm, ga_k_norm,
                       rope64, rope32, ctx_out):
    pc = split_cols(pmm_nd(h_ctx, w_in), IN_WIDTHS)
    pl_ = split_cols(pmm_nd(h_lat, w_in), IN_WIDTHS)
    ya_c, ya_l, vf_ctx, vf_lat = rwkv7_mixer(pc[0], pl_[0], vf_ctx, vf_lat, rw, ctx_out)
    yb_c, yb_l = window_mixer(pc[1:4], pl_[1:4], sw_sink, rope64, ctx_out)
    yc_c, yc_l = mla_mixer(pc[4:7], pl_[4:7], mla_q_norm, mla_w_uq, mla_kv_norm, mla_w_ukv, rope32, ctx_out)
    yd_c, yd_l = grid_attention_mixer(pc[7:10], pl_[7:10], ga_q_norm, ga_k_norm, rope64, ctx_out)

    def merge(ys, gate_cols):
        gates = jax.nn.sigmoid(gate_cols.astype(jnp.float32)).astype(gate_cols.dtype)
        m = None
        for i, y in enumerate(ys):
            term = gates[..., i * D_MODEL:(i + 1) * D_MODEL] * pmm_nd(y, w_branch[i])
            m = term if m is None else m + term
        return pmm_nd(m, w_out)

    out_lat = merge([ya_l, yb_l, yc_l, yd_l], pl_[10])
    out_ctx = merge([ya_c, yb_c, yc_c, yd_c], pc[10]) if ctx_out else None
    return out_ctx, out_lat, vf_ctx, vf_lat


def swiglu(h, w1, w3, w2):
    return pmm(jax.nn.silu(pmm(h, w1)) * pmm(h, w3), w2)


def moe_swiglu(h, w_router, w1, w3, w2):
    n_tok, d = h.shape
    logits = (h @ w_router).astype(jnp.float32)
    top_logit, top_idx = lax.top_k(logits, TOP_K)
    gates = jax.nn.softmax(top_logit, axis=-1)
    n_assign = n_tok * TOP_K
    flat_e = top_idx.reshape(-1)
    order = jnp.argsort(flat_e)
    sorted_e = flat_e[order]
    sorted_tok = (order // TOP_K).astype(jnp.int32)
    sorted_gate = gates.reshape(-1)[order]
    counts = jnp.bincount(flat_e, length=N_EXPERTS)
    padded = (counts + MOE_BLOCK - 1) // MOE_BLOCK * MOE_BLOCK
    start = jnp.cumsum(counts) - counts
    pend = jnp.cumsum(padded)
    pstart = pend - padded
    dest = pstart[sorted_e] + jnp.arange(n_assign) - start[sorted_e]
    n_blocks = -(-n_assign // MOE_BLOCK) + N_EXPERTS
    n_rows = n_blocks * MOE_BLOCK
    tok = jnp.zeros((n_rows,), jnp.int32).at[dest].set(sorted_tok)
    gate = jnp.zeros((n_rows,), h.dtype).at[dest].set(sorted_gate.astype(h.dtype))
    block_e = jnp.minimum(jnp.searchsorted(pend, jnp.arange(n_blocks) * MOE_BLOCK, side='right'), N_EXPERTS - 1)
    xb = h[tok].reshape(n_blocks, MOE_BLOCK, d)

    def expert_block(args):
        xblk, e = args
        return (jax.nn.silu(xblk @ w1[e]) * (xblk @ w3[e])) @ w2[e]

    yb = lax.map(expert_block, (xb, block_e)).reshape(n_rows, d)
    return jnp.zeros_like(h).at[tok].add(yb * gate[:, None])


def kernel(x, c, ctx, c_ctx, w_mod, b_mod, norm1, norm2, w_in, w_branch, w_out,
           rw_mu, rw_w0, rw_w1, rw_w2, rw_a0, rw_a1, rw_a2, rw_v0, rw_v1, rw_v2,
           rw_g1, rw_g2, rw_k_k, rw_k_a, rw_r_k, rw_ln_w, rw_ln_b, sw_sink,
           mla_q_norm, mla_w_uq, mla_kv_norm, mla_w_ukv, ga_q_norm, ga_k_norm,
           ffn_w1, ffn_w3, ffn_w2, moe_router, moe_w1, moe_w3, moe_w2, final_norm):
    b, s_len, d = x.shape
    n_ctx_tok = ctx.shape[1]
    rope64 = axial_rope_tables(s_len, HEAD_DIM)
    rope32 = axial_rope_tables(s_len, MLA_ROPE)
    silu_c = jax.nn.silu(c)
    silu_cc = jax.nn.silu(c_ctx)
    x_lat, x_ctx = x, ctx
    vf_lat, vf_ctx = None, None
    for l in range(DEPTH):
        ctx_out = l < DEPTH - 1
        mod_l = silu_c @ w_mod[l] + b_mod[l]
        mod_c = silu_cc @ w_mod[l] + b_mod[l]
        sh1, sc1, g1, sh2, sc2, g2 = jnp.split(mod_l[:, None, :], 6, axis=-1)
        csh1, csc1, cg1, csh2, csc2, cg2 = jnp.split(mod_c, 6, axis=-1)

        h_lat = rms_norm(x_lat, norm1[l]) * (1 + sc1) + sh1
        h_ctx = rms_norm(x_ctx, norm1[l]) * (1 + csc1) + csh1
        rw = dict(mu=rw_mu[l], w0=rw_w0[l], w1=rw_w1[l], w2=rw_w2[l],
                  a0=rw_a0[l], a1=rw_a1[l], a2=rw_a2[l],
                  v0=rw_v0[l - 1] if l > 0 else None,
                  v1=rw_v1[l - 1] if l > 0 else None,
                  v2=rw_v2[l - 1] if l > 0 else None,
                  g1=rw_g1[l], g2=rw_g2[l], k_k=rw_k_k[l], k_a=rw_k_a[l], r_k=rw_r_k[l],
                  ln_w=rw_ln_w[l], ln_b=rw_ln_b[l])
        out_ctx, out_lat, vf_ctx, vf_lat = hybrid_token_mixer(
            h_ctx, h_lat, vf_ctx, vf_lat, w_in[l], w_branch[l], w_out[l], rw, sw_sink[l],
            mla_q_norm[l], mla_w_uq[l], mla_kv_norm[l], mla_w_ukv[l], ga_q_norm[l], ga_k_norm[l],
            rope64, rope32, ctx_out)
        x_lat = x_lat + g1 * out_lat

        f_lat = rms_norm(x_lat, norm2[l]) * (1 + sc2) + sh2
        if ctx_out:
            x_ctx = x_ctx + cg1 * out_ctx
            f_ctx = rms_norm(x_ctx, norm2[l]) * (1 + csc2) + csh2
            tokens = jnp.concatenate([f_ctx.reshape(-1, d), f_lat.reshape(-1, d)], axis=0)
            n_c = b * n_ctx_tok
        else:
            tokens = f_lat.reshape(-1, d)
            n_c = 0
        if l % 2 == 0:
            ffn_out = swiglu(tokens, ffn_w1[l // 2], ffn_w3[l // 2], ffn_w2[l // 2])
        else:
            ffn_out = moe_swiglu(tokens, moe_router[l // 2], moe_w1[l // 2], moe_w3[l // 2], moe_w2[l // 2])
        x_lat = x_lat + g2 * ffn_out[n_c:].reshape(b, s_len, d)
        if ctx_out:
            x_ctx = x_ctx + cg2 * ffn_out[:n_c].reshape(b, n_ctx_tok, d)
    return rms_norm(x_lat, final_norm)
```

```python
import functools

import jax
import jax.numpy as jnp
from jax import lax
from jax.experimental import pallas as pl
from jax.experimental.pallas import tpu as pltpu

D_MODEL = 1024
DEPTH = 4
GRID_W = 64
HEAD_DIM = 64
BLOCK = 128
WINDOW = 128
ROPE_BASE = 10000.0
NORM_EPS = 1e-6
NEG_INF = -1e30
N_BRANCH = 4
BRANCH_WIDTH = 256
RW_HEADS = 4
RW_WIDTH = RW_HEADS * HEAD_DIM
RW_GN_EPS = 64e-5
SW_HEADS = 4
SW_KV_HEADS = 2
MLA_HEADS = 4
MLA_NOPE = 64
MLA_ROPE = 32
MLA_V = 64
MLA_Q_RANK = 256
MLA_KV_RANK = 128
GA_HEADS = 4
GA_KV_HEADS = 2
N_EXPERTS = 8
TOP_K = 2
MOE_BLOCK = 128
IN_WIDTHS = (4 * RW_WIDTH,
             SW_HEADS * HEAD_DIM, SW_KV_HEADS * HEAD_DIM, SW_KV_HEADS * HEAD_DIM,
             MLA_Q_RANK, MLA_KV_RANK, MLA_ROPE,
             GA_HEADS * HEAD_DIM, GA_KV_HEADS * HEAD_DIM, GA_KV_HEADS * HEAD_DIM,
             N_BRANCH * D_MODEL)


def _mm_kernel(a_ref, b_ref, o_ref):
    o_ref[...] = jnp.dot(a_ref[...], b_ref[...], preferred_element_type=jnp.float32)


def _round_up(x, m):
    return (x + m - 1) // m * m


def pmm(a, b, tm=512, tn=512):
    m, k = a.shape
    _, n = b.shape
    tm = min(tm, _round_up(m, 8))
    tn = min(tn, _round_up(n, 128))
    mp, np_ = _round_up(m, tm), _round_up(n, tn)
    a = a.astype(jnp.bfloat16)
    b = b.astype(jnp.bfloat16)
    if mp != m:
        a = jnp.pad(a, ((0, mp - m), (0, 0)))
    if np_ != n:
        b = jnp.pad(b, ((0, 0), (0, np_ - n)))
    out = pl.pallas_call(
        _mm_kernel,
        grid=(np_ // tn, mp // tm),
        in_specs=[pl.BlockSpec((tm, k), lambda j, i: (i, 0)),
                  pl.BlockSpec((k, tn), lambda j, i: (0, j))],
        out_specs=pl.BlockSpec((tm, tn), lambda j, i: (i, j)),
        out_shape=jax.ShapeDtypeStruct((mp, np_), jnp.float32),
        compiler_params=pltpu.CompilerParams(
            dimension_semantics=("arbitrary", "arbitrary"),
            vmem_limit_bytes=48 * 1024 * 1024),
        name="pmm",
    )(a, b)
    return out[:m, :n]


def pmm_nd(a, b, **kw):
    lead = a.shape[:-1]
    return pmm(a.reshape(-1, a.shape[-1]), b, **kw).reshape(lead + (b.shape[-1],))


RW_CHUNK = 64
RW_CHUNKS_PER_STEP = 2
RW_INV_BASE = 4

_BF = jnp.bfloat16
_F32 = jnp.float32


def _bd_rows(x):
    lane_head = lax.broadcasted_iota(jnp.int32, x.shape, 1) // HEAD_DIM
    return jnp.concatenate([jnp.where(lane_head == h, x, 0.0) for h in range(RW_HEADS)], axis=0)


def _dot(a, b):
    return jnp.dot(a.astype(_BF), b.astype(_BF), preferred_element_type=_F32)


def _dot_nt(a, b):
    return lax.dot_general(a.astype(_BF), b.astype(_BF), (((1,), (1,)), ((), ())),
                           preferred_element_type=_F32)


def _dot_tn(a, b):
    return lax.dot_general(a.astype(_BF), b.astype(_BF), (((0,), (0,)), ((), ())),
                           preferred_element_type=_F32)


def _split3(x):
    h1 = x.astype(_BF)
    r1 = x - h1.astype(_F32)
    h2 = r1.astype(_BF)
    h3 = (r1 - h2.astype(_F32)).astype(_BF)
    return h1, h2, h3


def _chunk_summary(r, lw, k, v, a, b):
    c = RW_CHUNK
    row = lax.broadcasted_iota(jnp.int32, (c, c), 0)
    col = lax.broadcasted_iota(jnp.int32, (c, c), 1)
    tri = (row >= col).astype(_BF)
    cum = sum(jnp.dot(tri, p, preferred_element_type=_F32) for p in _split3(lw))
    lwlast = cum[c - 1:c, :]
    w_in = jnp.exp(cum)
    w_ex = jnp.exp(cum - lw)
    w_inv = jnp.exp(-cum)
    w_tail = jnp.exp(lwlast - cum)
    wlast = jnp.exp(lwlast)
    rt, at, bt, kt = r * w_in, a * w_ex, b * w_inv, k * w_inv
    bh, kh = b * w_tail, k * w_tail

    t_idx = lax.broadcasted_iota(jnp.int32, (c, RW_WIDTH), 0)
    s_idx = lax.broadcasted_iota(jnp.int32, (c, RW_WIDTH), 1) % HEAD_DIM
    strict = s_idx < t_idx
    incl = s_idx <= t_idx

    l1 = jnp.concatenate([at, rt], axis=0)
    g_b = _dot_nt(l1, _bd_rows(bt))
    g_k = _dot_nt(l1, _bd_rows(kt))
    a_ab = jnp.where(strict, g_b[:c], 0.0)
    a_rb = jnp.where(incl, g_b[c:], 0.0)
    a_ak = jnp.where(strict, g_k[:c], 0.0)
    a_rk = jnp.where(incl, g_k[c:], 0.0)

    eye = jnp.where(s_idx == t_idx, 1.0, 0.0)
    base = RW_INV_BASE
    n_b = jnp.where(t_idx // base == s_idx // base, a_ab, 0.0)
    n_b2 = _dot(n_b, _bd_rows(n_b))
    tm = eye + n_b + n_b2 + _dot(n_b, _bd_rows(n_b2))
    blk = base
    while blk < c:
        off = (t_idx // (2 * blk) == s_idx // (2 * blk)) & (t_idx // blk != s_idx // blk)
        x = _dot(tm, _bd_rows(jnp.where(off, a_ab, 0.0)))
        tm = tm + _dot(x, _bd_rows(tm))
        blk *= 2

    av = _dot(jnp.concatenate([a_ak, a_rk], axis=0), _bd_rows(v))
    akv, arkv = av[:c], av[c:]
    ta = _dot(tm, _bd_rows(at))
    u0 = _dot(tm, _bd_rows(akv))
    r1 = rt + _dot(a_rb, _bd_rows(ta))
    y0 = arkv + _dot(a_rb, _bd_rows(u0))

    blk_r = lax.broadcasted_iota(jnp.int32, (RW_WIDTH, RW_WIDTH), 0) // HEAD_DIM
    blk_c = lax.broadcasted_iota(jnp.int32, (RW_WIDTH, RW_WIDTH), 1) // HEAD_DIM
    m_bd = jnp.where(blk_r == blk_c, _dot_tn(ta, bh), 0.0)
    z = _dot_tn(jnp.concatenate([u0, v], axis=0), jnp.concatenate([bh, kh], axis=0))
    lane_head = lax.broadcasted_iota(jnp.int32, (HEAD_DIM, RW_WIDTH), 1) // HEAD_DIM
    sadd = jnp.zeros((HEAD_DIM, RW_WIDTH), _F32)
    for h in range(RW_HEADS):
        sadd = jnp.where(lane_head == h, z[h * HEAD_DIM:(h + 1) * HEAD_DIM], sadd)
    return r1, y0, m_bd, sadd, wlast


def _rw_summary_kernel(r_ref, lw_ref, k_ref, v_ref, a_ref, b_ref,
                       r1_ref, y0_ref, m_ref, sadd_ref, wl_ref):
    c = RW_CHUNK
    for j in range(RW_CHUNKS_PER_STEP):
        sl = pl.ds(j * c, c)
        r1, y0, m_bd, sadd, wlast = _chunk_summary(
            r_ref[0, sl, :], lw_ref[0, sl, :], k_ref[0, sl, :], v_ref[0, sl, :],
            a_ref[0, sl, :], b_ref[0, sl, :])
        r1_ref[0, sl, :] = r1
        y0_ref[0, sl, :] = y0
        m_ref[0, j] = m_bd.astype(_BF)
        sadd_ref[0, j] = sadd
        wl_ref[0, j] = wlast


def _rw_state_kernel(r1_ref, y0_ref, m_ref, sadd_ref, wl_ref, y_ref, s_ref):
    @pl.when(pl.program_id(0) == 0)
    def _():
        s_ref[...] = jnp.zeros_like(s_ref)

    for q in range(s_ref.shape[0]):
        s = s_ref[q]
        y_ref[q] = _dot_nt(r1_ref[q], _bd_rows(s)) + y0_ref[q]
        s_ref[q] = s * wl_ref[q, 0] + _dot(s, m_ref[q, 0]) + sadd_ref[q, 0]


def rwkv_scan_chunked(r, lw, k, v, a, b):
    n_seq, t, w = r.shape
    c = RW_CHUNK
    n_chunks = t // c
    cps = RW_CHUNKS_PER_STEP
    assert t % (c * cps) == 0 and w == RW_WIDTH
    seq_spec = pl.BlockSpec((1, c * cps, w), lambda q, i: (q, i, 0))
    r1, y0, m_bd, sadd, wlast = pl.pallas_call(
        _rw_summary_kernel,
        grid=(n_seq, n_chunks // cps),
        in_specs=[seq_spec] * 6,
        out_specs=[seq_spec, seq_spec,
                   pl.BlockSpec((1, cps, w, w), lambda q, i: (q, i, 0, 0)),
                   pl.BlockSpec((1, cps, HEAD_DIM, w), lambda q, i: (q, i, 0, 0)),
                   pl.BlockSpec((1, cps, 1, w), lambda q, i: (q, i, 0, 0))],
        out_shape=[jax.ShapeDtypeStruct((n_seq, t, w), _F32),
                   jax.ShapeDtypeStruct((n_seq, t, w), _F32),
                   jax.ShapeDtypeStruct((n_seq, n_chunks, w, w), _BF),
                   jax.ShapeDtypeStruct((n_seq, n_chunks, HEAD_DIM, w), _F32),
                   jax.ShapeDtypeStruct((n_seq, n_chunks, 1, w), _F32)],
        compiler_params=pltpu.CompilerParams(
            dimension_semantics=("parallel", "parallel"),
            vmem_limit_bytes=48 * 1024 * 1024),
        name="rw_chunk_summary",
    )(r, lw, k, v, a, b)
    y = pl.pallas_call(
        _rw_state_kernel,
        grid=(n_chunks,),
        in_specs=[pl.BlockSpec((n_seq, c, w), lambda i: (0, i, 0)),
                  pl.BlockSpec((n_seq, c, w), lambda i: (0, i, 0)),
                  pl.BlockSpec((n_seq, 1, w, w), lambda i: (0, i, 0, 0)),
                  pl.BlockSpec((n_seq, 1, HEAD_DIM, w), lambda i: (0, i, 0, 0)),
                  pl.BlockSpec((n_seq, 1, 1, w), lambda i: (0, i, 0, 0))],
        out_specs=pl.BlockSpec((n_seq, c, w), lambda i: (0, i, 0)),
        out_shape=jax.ShapeDtypeStruct((n_seq, t, w), _F32),
        scratch_shapes=[pltpu.VMEM((n_seq, HEAD_DIM, w), _F32)],
        compiler_params=pltpu.CompilerParams(
            dimension_semantics=("arbitrary",),
            vmem_limit_bytes=48 * 1024 * 1024),
        name="rw_state_pass",
    )(r1, y0, m_bd, sadd, wlast)
    return y


ATTN_TQ = 256
ATTN_TK = 768


def _dense_attn_kernel(q_ref, k_ref, v_ref, o_ref, *, scale, tk):
    g, tq, dq = q_ref.shape[2:]
    dv = v_ref.shape[-1]
    n_keys = k_ref.shape[2]
    q = q_ref[0, 0].reshape(g * tq, dq)

    def body(i, carry):
        m, l, acc = carry
        off = pl.multiple_of(i * tk, tk)
        kc = k_ref[0, 0, pl.ds(off, tk), :]
        vc = v_ref[0, 0, pl.ds(off, tk), :]
        s = lax.dot_general(q, kc, (((1,), (1,)), ((), ())), preferred_element_type=_F32)
        m_new = jnp.maximum(m, jnp.max(s, axis=-1, keepdims=True))
        p = jnp.exp((s - m_new) * scale)
        alpha = jnp.exp((m - m_new) * scale)
        l = alpha * l + jnp.sum(p, axis=-1, keepdims=True)
        acc = alpha * acc + jnp.dot(p.astype(_BF), vc, preferred_element_type=_F32)
        return m_new, l, acc

    m0 = jnp.full((g * tq, 1), -jnp.inf, _F32)
    l0 = jnp.zeros((g * tq, 1), _F32)
    a0 = jnp.zeros((g * tq, dv), _F32)
    m, l, acc = lax.fori_loop(0, n_keys // tk, body, (m0, l0, a0))
    o_ref[0, 0] = (acc / l).reshape(g, tq, dv)


def dense_block_attention(q, k, v, scale):
    b, s_len, hk, g, dq = q.shape
    n_keys, dv = k.shape[1], v.shape[-1]
    tq = min(ATTN_TQ, s_len)
    tk = ATTN_TK if n_keys % ATTN_TK == 0 else n_keys
    qt = jnp.transpose(q, (0, 2, 3, 1, 4)).astype(_BF)
    kt = jnp.transpose(k, (0, 2, 1, 3)).astype(_BF)
    vt = jnp.transpose(v, (0, 2, 1, 3)).astype(_BF)
    o = pl.pallas_call(
        functools.partial(_dense_attn_kernel, scale=scale, tk=tk),
        grid=(b, hk, s_len // tq),
        in_specs=[pl.BlockSpec((1, 1, g, tq, dq), lambda bi, h, i: (bi, h, 0, i, 0)),
                  pl.BlockSpec((1, 1, n_keys, dq), lambda bi, h, i: (bi, h, 0, 0)),
                  pl.BlockSpec((1, 1, n_keys, dv), lambda bi, h, i: (bi, h, 0, 0))],
        out_specs=pl.BlockSpec((1, 1, g, tq, dv), lambda bi, h, i: (bi, h, 0, i, 0)),
        out_shape=jax.ShapeDtypeStruct((b, hk, g, s_len, dv), _F32),
        compiler_params=pltpu.CompilerParams(
            dimension_semantics=("parallel", "parallel", "parallel"),
            vmem_limit_bytes=48 * 1024 * 1024),
        name="dense_attention",
    )(qt, kt, vt)
    return jnp.transpose(o, (0, 3, 1, 2, 4)).reshape(b, s_len, hk * g * dv)


def rms_norm(x, g):
    xf = x.astype(jnp.float32)
    y = xf * lax.rsqrt(jnp.mean(xf * xf, axis=-1, keepdims=True) + NORM_EPS)
    return (y * g.astype(jnp.float32)).astype(x.dtype)


def split_cols(p, widths):
    out, off = [], 0
    for w in widths:
        out.append(p[..., off:off + w])
        off += w
    return out


def axial_rope_tables(n_tokens, rot_dim):
    rows = n_tokens // GRID_W
    row = jnp.repeat(jnp.arange(rows, dtype=jnp.float32), GRID_W)
    col = jnp.tile(jnp.arange(GRID_W, dtype=jnp.float32), rows)
    n_freq = rot_dim // 4
    inv_freq = ROPE_BASE ** (-jnp.arange(n_freq, dtype=jnp.float32) / n_freq)
    ang = jnp.concatenate([row[:, None] * inv_freq, col[:, None] * inv_freq], axis=-1)
    return jnp.cos(ang), jnp.sin(ang)


def apply_rope(x, cos, sin):
    half = x.shape[-1] // 2
    c = cos[None, :, None, :].astype(x.dtype)
    s = sin[None, :, None, :].astype(x.dtype)
    x1, x2 = x[..., :half], x[..., half:]
    return jnp.concatenate([x1 * c - x2 * s, x1 * s + x2 * c], axis=-1)


def context_attention(q, k, v, scale, sink=None):
    b, n = q.shape[:2]
    s = jnp.einsum('bqhgd,bkhd->bhgqk', q, k).astype(jnp.float32) * scale
    if sink is not None:
        s_sink = jnp.broadcast_to(sink[None, :, :, None, None].astype(jnp.float32), s.shape[:-1] + (1,))
        p = jax.nn.softmax(jnp.concatenate([s_sink, s], axis=-1), axis=-1)[..., 1:]
    else:
        p = jax.nn.softmax(s, axis=-1)
    o = jnp.einsum('bhgqk,bkhd->bqhgd', p.astype(v.dtype), v)
    return o.reshape(b, n, -1)


def banded_window_attention(q, k, v, k_ctx, v_ctx, sink, scale):
    b, s_len, hk, g, d = q.shape
    nb = s_len // BLOCK
    qb = q.reshape(b, nb, BLOCK, hk, g, d)

    def neighbours(t):
        tb = jnp.pad(t.reshape(b, nb, BLOCK, hk, t.shape[-1]), ((0, 0), (1, 1), (0, 0), (0, 0), (0, 0)))
        return jnp.concatenate([tb[:, :-2], tb[:, 1:-1], tb[:, 2:]], axis=2)

    kb, vb = neighbours(k), neighbours(v)
    blk = jnp.arange(nb)
    q_pos = blk[:, None] * BLOCK + jnp.arange(BLOCK)[None, :]
    k_pos = (blk[:, None] - 1) * BLOCK + jnp.arange(3 * BLOCK)[None, :]
    valid = ((jnp.abs(q_pos[:, :, None] - k_pos[:, None, :]) <= WINDOW)
             & (k_pos >= 0)[:, None, :] & (k_pos < s_len)[:, None, :])
    s_win = jnp.einsum('bnqhgd,bnkhd->bnhgqk', qb, kb).astype(jnp.float32) * scale
    s_win = jnp.where(valid[None, :, None, None], s_win, NEG_INF)
    s_ctx = jnp.einsum('bnqhgd,bchd->bnhgqc', qb, k_ctx).astype(jnp.float32) * scale
    s_sink = jnp.broadcast_to(sink[None, None, :, :, None, None].astype(jnp.float32), s_ctx.shape[:-1] + (1,))
    p = jax.nn.softmax(jnp.concatenate([s_sink, s_ctx, s_win], axis=-1), axis=-1).astype(v.dtype)
    n_ctx = k_ctx.shape[1]
    o = (jnp.einsum('bnhgqc,bchd->bnqhgd', p[..., 1:1 + n_ctx], v_ctx)
         + jnp.einsum('bnhgqk,bnkhd->bnqhgd', p[..., 1 + n_ctx:], vb))
    return o.reshape(b, s_len, -1)


def gqa_heads(q, k, v, n_heads, n_kv, rope=None, q_gain=None, k_gain=None):
    b, n, _ = q.shape
    q = q.reshape(b, n, n_heads, HEAD_DIM)
    k = k.reshape(b, n, n_kv, HEAD_DIM)
    v = v.reshape(b, n, n_kv, HEAD_DIM)
    if q_gain is not None:
        q = rms_norm(q, q_gain)
        k = rms_norm(k, k_gain)
    if rope is not None:
        q = apply_rope(q, rope[0], rope[1])
        k = apply_rope(k, rope[0], rope[1])
    return q.reshape(b, n, n_kv, n_heads // n_kv, HEAD_DIM), k, v


def window_mixer(pc, pl_, sink, rope, ctx_out):
    scale = HEAD_DIM ** -0.5
    qc, kc, vc = gqa_heads(pc[0], pc[1], pc[2], SW_HEADS, SW_KV_HEADS)
    ql, kl, vl = gqa_heads(pl_[0], pl_[1], pl_[2], SW_HEADS, SW_KV_HEADS, rope)
    sink = sink.reshape(SW_KV_HEADS, SW_HEADS // SW_KV_HEADS)
    y_lat = banded_window_attention(ql, kl, vl, kc, vc, sink, scale)
    y_ctx = context_attention(qc, kc, vc, scale, sink) if ctx_out else None
    return y_ctx, y_lat


def grid_attention_mixer(pc, pl_, q_gain, k_gain, rope, ctx_out):
    scale = HEAD_DIM ** -0.5
    qc, kc, vc = gqa_heads(pc[0], pc[1], pc[2], GA_HEADS, GA_KV_HEADS, None, q_gain, k_gain)
    ql, kl, vl = gqa_heads(pl_[0], pl_[1], pl_[2], GA_HEADS, GA_KV_HEADS, rope, q_gain, k_gain)
    y_lat = dense_block_attention(ql, jnp.concatenate([kc, kl], axis=1), jnp.concatenate([vc, vl], axis=1), scale)
    y_ctx = context_attention(qc, kc, vc, scale) if ctx_out else None
    return y_ctx, y_lat


def mla_mixer(pc, pl_, q_gain, w_uq, kv_gain, w_ukv, rope, ctx_out):
    scale = (MLA_NOPE + MLA_ROPE) ** -0.5

    def project(cq, ckv, k_rope, rope_tab):
        b, n, _ = cq.shape
        q = (rms_norm(cq, q_gain) @ w_uq).reshape(b, n, MLA_HEADS, MLA_NOPE + MLA_ROPE)
        kv = (rms_norm(ckv, kv_gain) @ w_ukv).reshape(b, n, MLA_HEADS, MLA_NOPE + MLA_V)
        q_nope, q_pe = q[..., :MLA_NOPE], q[..., MLA_NOPE:]
        k_nope, v = kv[..., :MLA_NOPE], kv[..., MLA_NOPE:]
        k_pe = k_rope[:, :, None, :]
        if rope_tab is not None:
            q_pe = apply_rope(q_pe, rope_tab[0], rope_tab[1])
            k_pe = apply_rope(k_pe, rope_tab[0], rope_tab[1])
        q = jnp.concatenate([q_nope, q_pe], axis=-1)
        k = jnp.concatenate([k_nope, jnp.broadcast_to(k_pe, (b, n, MLA_HEADS, MLA_ROPE))], axis=-1)
        return q[:, :, :, None, :], k, v

    qc, kc, vc = project(pc[0], pc[1], pc[2], None)
    ql, kl, vl = project(pl_[0], pl_[1], pl_[2], rope)
    y_lat = dense_block_attention(ql, jnp.concatenate([kc, kl], axis=1), jnp.concatenate([vc, vl], axis=1), scale)
    y_ctx = context_attention(qc, kc, vc, scale) if ctx_out else None
    return y_ctx, y_lat


def centred_token_shift(f, mu):
    prev = jnp.pad(f[:, :-1], ((0, 0), (1, 0), (0, 0)))
    nxt = jnp.pad(f[:, 1:], ((0, 0), (0, 1), (0, 0)))
    return f + mu[0] * (prev - f) + mu[1] * (nxt - f)


def rwkv7_mixer(f_ctx, f_lat, vf_ctx, vf_lat, rw, ctx_out):
    def heads(t):
        return t.reshape(t.shape[:-1] + (RW_HEADS, HEAD_DIM))

    r_k = rw['r_k'].reshape(RW_HEADS, HEAD_DIM)

    def features(f, vf):
        f = centred_token_shift(f, rw['mu'])
        r, k, v, z = jnp.split(f, 4, axis=-1)
        if vf is None:
            vf = v
        else:
            v = v + (vf - v) * jax.nn.sigmoid(rw['v0'] + (z @ rw['v1']) @ rw['v2'])
        kk = heads(k * rw['k_k']).astype(jnp.float32)
        kk = (kk * lax.rsqrt(jnp.sum(kk * kk, axis=-1, keepdims=True) + 1e-12)).astype(f.dtype)
        g = jax.nn.sigmoid(z @ rw['g1']) @ rw['g2']
        per_dir = []
        for d in range(2):
            w = -jax.nn.softplus(-(rw['w0'][d] + jnp.tanh(z @ rw['w1'][d]) @ rw['w2'][d])) - 0.5
            log_decay = -jnp.exp(w.astype(jnp.float32))
            a = jax.nn.sigmoid(rw['a0'][d] + (z @ rw['a1'][d]) @ rw['a2'][d])
            k_d = k * (1 + (a - 1) * rw['k_a'])
            per_dir.append((heads(log_decay), heads(a), heads(k_d)))
        return heads(r), heads(v), kk, g, per_dir, vf

    r_c, v_c, kk_c, g_c, dir_c, vf_ctx = features(f_ctx, vf_ctx)
    r_l, v_l, kk_l, g_l, dir_l, vf_lat = features(f_lat, vf_lat)
    n_c, n_l = f_ctx.shape[1], f_lat.shape[1]
    n_b = f_lat.shape[0]

    def sequence(t_ctx, t_lat, rev):
        t_ctx = jnp.broadcast_to(t_ctx, (n_b,) + t_ctx.shape[1:])
        if rev:
            t_ctx, t_lat = jnp.flip(t_ctx, axis=1), jnp.flip(t_lat, axis=1)
        return jnp.concatenate([t_ctx, t_lat], axis=1).reshape(n_b, n_c + n_l, RW_WIDTH)

    seqs = [[], [], [], [], [], []]
    for d in range(2):
        rev = d == 1
        lw_c, a_c, kd_c = dir_c[d]
        lw_l, a_l, kd_l = dir_l[d]
        for lst, (tc, tl) in zip(seqs, ((r_c, r_l), (lw_c, lw_l), (kd_c, kd_l), (v_c, v_l),
                                        (-kk_c, -kk_l), (kk_c * a_c, kk_l * a_l))):
            lst.append(sequence(tc, tl, rev))
    y_all = rwkv_scan_chunked(*[jnp.concatenate(lst, axis=0) for lst in seqs])
    y_all = y_all.reshape(2, n_b, n_c + n_l, RW_HEADS, HEAD_DIM)
    y_c, y_l, bonus_c, bonus_l = 0.0, 0.0, 0.0, 0.0
    for d in range(2):
        rev = d == 1
        y_dc, y_dl = y_all[d, :, :n_c], y_all[d, :, n_c:]
        if rev:
            y_dc, y_dl = jnp.flip(y_dc, axis=1), jnp.flip(y_dl, axis=1)
        _, _, k_d = dir_c[d]
        if ctx_out:
            y_c = y_c + y_dc
            bonus_c = bonus_c + jnp.sum(r_c * k_d * r_k, axis=-1, keepdims=True) * v_c
        _, _, k_d = dir_l[d]
        y_l = y_l + y_dl
        bonus_l = bonus_l + jnp.sum(r_l * k_d * r_k, axis=-1, keepdims=True) * v_l

    ln_w = rw['ln_w'].reshape(RW_HEADS, HEAD_DIM).astype(jnp.float32)
    ln_b = rw['ln_b'].reshape(RW_HEADS, HEAD_DIM).astype(jnp.float32)

    def finish(y, bonus, g):
        mu = jnp.mean(y, axis=-1, keepdims=True)
        var = jnp.mean(jnp.square(y - mu), axis=-1, keepdims=True)
        yn = (y - mu) * lax.rsqrt(var + RW_GN_EPS) * ln_w + ln_b
        o = yn.astype(g.dtype) + bonus
        return o.reshape(g.shape) * g

    y_lat = finish(y_l, bonus_l, g_l)
    y_ctx = finish(y_c, bonus_c, g_c) if ctx_out else None
    return y_ctx, y_lat, vf_ctx, vf_lat


def hybrid_token_mixer(h_ctx, h_lat, vf_ctx, vf_lat, w_in, w_branch, w_out, rw, sw_sink,
                       mla_q_norm, mla_w_uq, mla_kv_norm, mla_w_ukv, ga_q_norm, ga_k_norm,
                       rope64, rope32, ctx_out):
    pc = split_cols(pmm_nd(h_ctx, w_in), IN_WIDTHS)
    pl_ = split_cols(pmm_nd(h_lat, w_in), IN_WIDTHS)
    ya_c, ya_l, vf_ctx, vf_lat = rwkv7_mixer(pc[0], pl_[0], vf_ctx, vf_lat, rw, ctx_out)
    yb_c, yb_l = window_mixer(pc[1:4], pl_[1:4], sw_sink, rope64, ctx_out)
    yc_c, yc_l = mla_mixer(pc[4:7], pl_[4:7], mla_q_norm, mla_w_uq, mla_kv_norm, mla_w_ukv, rope32, ctx_out)
    yd_c, yd_l = grid_attention_mixer(pc[7:10], pl_[7:10], ga_q_norm, ga_k_norm, rope64, ctx_out)

    def merge(ys, gate_cols):
        gates = jax.nn.sigmoid(gate_cols.astype(jnp.float32)).astype(gate_cols.dtype)
        m = None
        for i, y in enumerate(ys):
            term = gates[..., i * D_MODEL:(i + 1) * D_MODEL] * pmm_nd(y, w_branch[i])
            m = term if m is None else m + term
        return pmm_nd(m, w_out)

    out_lat = merge([ya_l, yb_l, yc_l, yd_l], pl_[10])
    out_ctx = merge([ya_c, yb_c, yc_c, yd_c], pc[10]) if ctx_out else None
    return out_ctx, out_lat, vf_ctx, vf_lat


def swiglu(h, w1, w3, w2):
    return pmm(jax.nn.silu(pmm(h, w1)) * pmm(h, w3), w2)


def moe_swiglu(h, w_router, w1, w3, w2):
    n_tok, d = h.shape
    logits = (h @ w_router).astype(jnp.float32)
    top_logit, top_idx = lax.top_k(logits, TOP_K)
    gates = jax.nn.softmax(top_logit, axis=-1)
    n_assign = n_tok * TOP_K
    flat_e = top_idx.reshape(-1)
    order = jnp.argsort(flat_e)
    sorted_e = flat_e[order]
    sorted_tok = (order // TOP_K).astype(jnp.int32)
    sorted_gate = gates.reshape(-1)[order]
    counts = jnp.bincount(flat_e, length=N_EXPERTS)
    padded = (counts + MOE_BLOCK - 1) // MOE_BLOCK * MOE_BLOCK
    start = jnp.cumsum(counts) - counts
    pend = jnp.cumsum(padded)
    pstart = pend - padded
    dest = pstart[sorted_e] + jnp.arange(n_assign) - start[sorted_e]
    n_blocks = -(-n_assign // MOE_BLOCK) + N_EXPERTS
    n_rows = n_blocks * MOE_BLOCK
    tok = jnp.zeros((n_rows,), jnp.int32).at[dest].set(sorted_tok)
    gate = jnp.zeros((n_rows,), h.dtype).at[dest].set(sorted_gate.astype(h.dtype))
    block_e = jnp.minimum(jnp.searchsorted(pend, jnp.arange(n_blocks) * MOE_BLOCK, side='right'), N_EXPERTS - 1)
    xb = h[tok].reshape(n_blocks, MOE_BLOCK, d)

    def expert_block(args):
        xblk, e = args
        return (jax.nn.silu(xblk @ w1[e]) * (xblk @ w3[e])) @ w2[e]

    yb = lax.map(expert_block, (xb, block_e)).reshape(n_rows, d)
    return jnp.zeros_like(h).at[tok].add(yb * gate[:, None])


def kernel(x, c, ctx, c_ctx, w_mod, b_mod, norm1, norm2, w_in, w_branch, w_out,
           rw_mu, rw_w0, rw_w1, rw_w2, rw_a0, rw_a1, rw_a2, rw_v0, rw_v1, rw_v2,
           rw_g1, rw_g2, rw_k_k, rw_k_a, rw_r_k, rw_ln_w, rw_ln_b, sw_sink,
           mla_q_norm, mla_w_uq, mla_kv_norm, mla_w_ukv, ga_q_norm, ga_k_norm,
           ffn_w1, ffn_w3, ffn_w2, moe_router, moe_w1, moe_w3, moe_w2, final_norm):
    b, s_len, d = x.shape
    n_ctx_tok = ctx.shape[1]
    rope64 = axial_rope_tables(s_len, HEAD_DIM)
    rope32 = axial_rope_tables(s_len, MLA_ROPE)
    silu_c = jax.nn.silu(c)
    silu_cc = jax.nn.silu(c_ctx)
    x_lat, x_ctx = x, ctx
    vf_lat, vf_ctx = None, None
    for l in range(DEPTH):
        ctx_out = l < DEPTH - 1
        mod_l = silu_c @ w_mod[l] + b_mod[l]
        mod_c = silu_cc @ w_mod[l] + b_mod[l]
        sh1, sc1, g1, sh2, sc2, g2 = jnp.split(mod_l[:, None, :], 6, axis=-1)
        csh1, csc1, cg1, csh2, csc2, cg2 = jnp.split(mod_c, 6, axis=-1)

        h_lat = rms_norm(x_lat, norm1[l]) * (1 + sc1) + sh1
        h_ctx = rms_norm(x_ctx, norm1[l]) * (1 + csc1) + csh1
        rw = dict(mu=rw_mu[l], w0=rw_w0[l], w1=rw_w1[l], w2=rw_w2[l],
                  a0=rw_a0[l], a1=rw_a1[l], a2=rw_a2[l],
                  v0=rw_v0[l - 1] if l > 0 else None,
                  v1=rw_v1[l - 1] if l > 0 else None,
                  v2=rw_v2[l - 1] if l > 0 else None,
                  g1=rw_g1[l], g2=rw_g2[l], k_k=rw_k_k[l], k_a=rw_k_a[l], r_k=rw_r_k[l],
                  ln_w=rw_ln_w[l], ln_b=rw_ln_b[l])
        out_ctx, out_lat, vf_ctx, vf_lat = hybrid_token_mixer(
            h_ctx, h_lat, vf_ctx, vf_lat, w_in[l], w_branch[l], w_out[l], rw, sw_sink[l],
            mla_q_norm[l], mla_w_uq[l], mla_kv_norm[l], mla_w_ukv[l], ga_q_norm[l], ga_k_norm[l],
            rope64, rope32, ctx_out)
        x_lat = x_lat + g1 * out_lat

        f_lat = rms_norm(x_lat, norm2[l]) * (1 + sc2) + sh2
        if ctx_out:
            x_ctx = x_ctx + cg1 * out_ctx
            f_ctx = rms_norm(x_ctx, norm2[l]) * (1 + csc2) + csh2
            tokens = jnp.concatenate([f_ctx.reshape(-1, d), f_lat.reshape(-1, d)], axis=0)
            n_c = b * n_ctx_tok
        else:
            tokens = f_lat.reshape(-1, d)
            n_c = 0
        if l % 2 == 0:
            ffn_out = swiglu(tokens, ffn_w1[l // 2], ffn_w3[l // 2], ffn_w2[l // 2])
        else:
            ffn_out = moe_swiglu(tokens, moe_router[l // 2], moe_w1[l // 2], moe_w3[l // 2], moe_w2[l // 2])
        x_lat = x_lat + g2 * ffn_out[n_c:].reshape(b, s_len, d)
        if ctx_out:
            x_ctx = x_ctx + cg2 * ffn_out[:n_c].reshape(b, n_ctx_tok, d)
    return rms_norm(x_lat, final_norm)
```

```python
import functools

import jax
import jax.numpy as jnp
from jax import lax
from jax.experimental import pallas as pl
from jax.experimental.pallas import tpu as pltpu

D_MODEL = 1024
DEPTH = 4
GRID_W = 64
HEAD_DIM = 64
BLOCK = 128
WINDOW = 128
ROPE_BASE = 10000.0
NORM_EPS = 1e-6
NEG_INF = -1e30
N_BRANCH = 4
BRANCH_WIDTH = 256
RW_HEADS = 4
RW_WIDTH = RW_HEADS * HEAD_DIM
RW_GN_EPS = 64e-5
SW_HEADS = 4
SW_KV_HEADS = 2
MLA_HEADS = 4
MLA_NOPE = 64
MLA_ROPE = 32
MLA_V = 64
MLA_Q_RANK = 256
MLA_KV_RANK = 128
GA_HEADS = 4
GA_KV_HEADS = 2
N_EXPERTS = 8
TOP_K = 2
MOE_BLOCK = 128
IN_WIDTHS = (4 * RW_WIDTH,
             SW_HEADS * HEAD_DIM, SW_KV_HEADS * HEAD_DIM, SW_KV_HEADS * HEAD_DIM,
             MLA_Q_RANK, MLA_KV_RANK, MLA_ROPE,
             GA_HEADS * HEAD_DIM, GA_KV_HEADS * HEAD_DIM, GA_KV_HEADS * HEAD_DIM,
             N_BRANCH * D_MODEL)


def _mm_kernel(a_ref, b_ref, o_ref):
    o_ref[...] = jnp.dot(a_ref[...], b_ref[...], preferred_element_type=jnp.float32)


def _round_up(x, m):
    return (x + m - 1) // m * m


def pmm(a, b, tm=512, tn=512):
    m, k = a.shape
    _, n = b.shape
    tm = min(tm, _round_up(m, 8))
    tn = min(tn, _round_up(n, 128))
    mp, np_ = _round_up(m, tm), _round_up(n, tn)
    a = a.astype(jnp.bfloat16)
    b = b.astype(jnp.bfloat16)
    if mp != m:
        a = jnp.pad(a, ((0, mp - m), (0, 0)))
    if np_ != n:
        b = jnp.pad(b, ((0, 0), (0, np_ - n)))
    out = pl.pallas_call(
        _mm_kernel,
        grid=(np_ // tn, mp // tm),
        in_specs=[pl.BlockSpec((tm, k), lambda j, i: (i, 0)),
                  pl.BlockSpec((k, tn), lambda j, i: (0, j))],
        out_specs=pl.BlockSpec((tm, tn), lambda j, i: (i, j)),
        out_shape=jax.ShapeDtypeStruct((mp, np_), jnp.float32),
        compiler_params=pltpu.CompilerParams(
            dimension_semantics=("arbitrary", "arbitrary"),
            vmem_limit_bytes=48 * 1024 * 1024),
        name="pmm",
    )(a, b)
    return out[:m, :n]


def pmm_nd(a, b, **kw):
    lead = a.shape[:-1]
    return pmm(a.reshape(-1, a.shape[-1]), b, **kw).reshape(lead + (b.shape[-1],))


RW_CHUNK = 64
RW_CHUNKS_PER_STEP = 6
RW_INV_BASE = 4

_BF = jnp.bfloat16
_F32 = jnp.float32


def _bd_rows(x):
    lane_head = lax.broadcasted_iota(jnp.int32, x.shape, 1) // HEAD_DIM
    return jnp.concatenate([jnp.where(lane_head == h, x, 0.0) for h in range(RW_HEADS)], axis=0)


def _dot(a, b):
    return jnp.dot(a.astype(_BF), b.astype(_BF), preferred_element_type=_F32)


def _dot_nt(a, b):
    return lax.dot_general(a.astype(_BF), b.astype(_BF), (((1,), (1,)), ((), ())),
                           preferred_element_type=_F32)


def _dot_tn(a, b):
    return lax.dot_general(a.astype(_BF), b.astype(_BF), (((0,), (0,)), ((), ())),
                           preferred_element_type=_F32)


def _split3(x):
    h1 = x.astype(_BF)
    r1 = x - h1.astype(_F32)
    h2 = r1.astype(_BF)
    h3 = (r1 - h2.astype(_F32)).astype(_BF)
    return h1, h2, h3


def _each(fn, *lists):
    return [fn(*args) for args in zip(*lists)]


def _chunk_summaries(r, lw, k, v, a, b):
    c = RW_CHUNK
    row = lax.broadcasted_iota(jnp.int32, (c, c), 0)
    col = lax.broadcasted_iota(jnp.int32, (c, c), 1)
    tri = (row >= col).astype(_BF)
    cum = _each(lambda x: sum(jnp.dot(tri, p, preferred_element_type=_F32) for p in _split3(x)), lw)
    lwlast = _each(lambda x: x[c - 1:c, :], cum)
    rt = _each(lambda x, cu: x * jnp.exp(cu), r, cum)
    at = _each(lambda x, cu, l: x * jnp.exp(cu - l), a, cum, lw)
    w_inv = _each(lambda cu: jnp.exp(-cu), cum)
    w_tail = _each(lambda cu, ll: jnp.exp(ll - cu), cum, lwlast)
    wlast = _each(jnp.exp, lwlast)
    bt = _each(jnp.multiply, b, w_inv)
    kt = _each(jnp.multiply, k, w_inv)
    bh = _each(jnp.multiply, b, w_tail)
    kh = _each(jnp.multiply, k, w_tail)

    t_idx = lax.broadcasted_iota(jnp.int32, (c, RW_WIDTH), 0)
    s_idx = lax.broadcasted_iota(jnp.int32, (c, RW_WIDTH), 1) % HEAD_DIM
    strict = s_idx < t_idx
    incl = s_idx <= t_idx

    l1 = _each(lambda x, y: jnp.concatenate([x, y], axis=0), at, rt)
    g_b = _each(lambda x, y: _dot_nt(x, _bd_rows(y)), l1, bt)
    g_k = _each(lambda x, y: _dot_nt(x, _bd_rows(y)), l1, kt)
    a_ab = _each(lambda g: jnp.where(strict, g[:c], 0.0), g_b)
    a_rb = _each(lambda g: jnp.where(incl, g[c:], 0.0), g_b)
    a_ak = _each(lambda g: jnp.where(strict, g[:c], 0.0), g_k)
    a_rk = _each(lambda g: jnp.where(incl, g[c:], 0.0), g_k)

    eye = jnp.where(s_idx == t_idx, 1.0, 0.0)
    base = RW_INV_BASE
    n_b = _each(lambda x: jnp.where(t_idx // base == s_idx // base, x, 0.0), a_ab)
    n_b2 = _each(lambda x: _dot(x, _bd_rows(x)), n_b)
    n_b3 = _each(lambda x, y: _dot(x, _bd_rows(y)), n_b, n_b2)
    tm = _each(lambda x, y, z: eye + x + y + z, n_b, n_b2, n_b3)
    blk = base
    while blk < c:
        off = (t_idx // (2 * blk) == s_idx // (2 * blk)) & (t_idx // blk != s_idx // blk)
        x = _each(lambda t, n: _dot(t, _bd_rows(jnp.where(off, n, 0.0))), tm, a_ab)
        tm = _each(lambda t, xx: t + _dot(xx, _bd_rows(t)), tm, x)
        blk *= 2

    av = _each(lambda x, y, vv: _dot(jnp.concatenate([x, y], axis=0), _bd_rows(vv)), a_ak, a_rk, v)
    ta = _each(lambda t, x: _dot(t, _bd_rows(x)), tm, at)
    u0 = _each(lambda t, x: _dot(t, _bd_rows(x[:c])), tm, av)
    r1 = _each(lambda x, ar, t: x + _dot(ar, _bd_rows(t)), rt, a_rb, ta)
    y0 = _each(lambda x, ar, u: x[c:] + _dot(ar, _bd_rows(u)), av, a_rb, u0)

    blk_r = lax.broadcasted_iota(jnp.int32, (RW_WIDTH, RW_WIDTH), 0) // HEAD_DIM
    blk_c = lax.broadcasted_iota(jnp.int32, (RW_WIDTH, RW_WIDTH), 1) // HEAD_DIM
    m_bd = _each(lambda t, x: jnp.where(blk_r == blk_c, _dot_tn(t, x), 0.0), ta, bh)
    z = _each(lambda u, vv, x, y: _dot_tn(jnp.concatenate([u, vv], axis=0),
                                          jnp.concatenate([x, y], axis=0)), u0, v, bh, kh)
    lane_head = lax.broadcasted_iota(jnp.int32, (HEAD_DIM, RW_WIDTH), 1) // HEAD_DIM

    def diag_blocks(zz):
        out = zz[:HEAD_DIM]
        for h in range(1, RW_HEADS):
            out = jnp.where(lane_head == h, zz[h * HEAD_DIM:(h + 1) * HEAD_DIM], out)
        return out

    sadd = _each(diag_blocks, z)
    return r1, y0, m_bd, sadd, wlast


def _rw_summary_kernel(r_ref, lw_ref, k_ref, v_ref, a_ref, b_ref,
                       r1_ref, y0_ref, m_ref, sadd_ref, wl_ref):
    c = RW_CHUNK

    def chunks(ref):
        return [ref[0, pl.ds(j * c, c), :] for j in range(RW_CHUNKS_PER_STEP)]

    r1, y0, m_bd, sadd, wlast = _chunk_summaries(
        chunks(r_ref), chunks(lw_ref), chunks(k_ref), chunks(v_ref), chunks(a_ref), chunks(b_ref))
    r1_ref[0] = jnp.concatenate(r1, axis=0)
    y0_ref[0] = jnp.concatenate(y0, axis=0)
    m_ref[0] = jnp.stack(m_bd, axis=0).astype(_BF)
    sadd_ref[0] = jnp.stack(sadd, axis=0)
    wl_ref[0] = jnp.stack(wlast, axis=0)


def _rw_state_kernel(r1_ref, y0_ref, m_ref, sadd_ref, wl_ref, y_ref, s_ref):
    @pl.when(pl.program_id(0) == 0)
    def _():
        s_ref[...] = jnp.zeros_like(s_ref)

    for q in range(s_ref.shape[0]):
        s = s_ref[q]
        y_ref[q] = _dot_nt(r1_ref[q], _bd_rows(s)) + y0_ref[q]
        s_ref[q] = s * wl_ref[q, 0] + _dot(s, m_ref[q, 0]) + sadd_ref[q, 0]


def rwkv_scan_chunked(r, lw, k, v, a, b):
    n_seq, t, w = r.shape
    c = RW_CHUNK
    n_chunks = t // c
    cps = RW_CHUNKS_PER_STEP
    assert t % (c * cps) == 0 and w == RW_WIDTH
    seq_spec = pl.BlockSpec((1, c * cps, w), lambda q, i: (q, i, 0))
    r1, y0, m_bd, sadd, wlast = pl.pallas_call(
        _rw_summary_kernel,
        grid=(n_seq, n_chunks // cps),
        in_specs=[seq_spec] * 6,
        out_specs=[seq_spec, seq_spec,
                   pl.BlockSpec((1, cps, w, w), lambda q, i: (q, i, 0, 0)),
                   pl.BlockSpec((1, cps, HEAD_DIM, w), lambda q, i: (q, i, 0, 0)),
                   pl.BlockSpec((1, cps, 1, w), lambda q, i: (q, i, 0, 0))],
        out_shape=[jax.ShapeDtypeStruct((n_seq, t, w), _F32),
                   jax.ShapeDtypeStruct((n_seq, t, w), _F32),
                   jax.ShapeDtypeStruct((n_seq, n_chunks, w, w), _BF),
                   jax.ShapeDtypeStruct((n_seq, n_chunks, HEAD_DIM, w), _F32),
                   jax.ShapeDtypeStruct((n_seq, n_chunks, 1, w), _F32)],
        compiler_params=pltpu.CompilerParams(
            dimension_semantics=("parallel", "parallel"),
            vmem_limit_bytes=48 * 1024 * 1024),
        name="rw_chunk_summary",
    )(r, lw, k, v, a, b)
    y = pl.pallas_call(
        _rw_state_kernel,
        grid=(n_chunks,),
        in_specs=[pl.BlockSpec((n_seq, c, w), lambda i: (0, i, 0)),
                  pl.BlockSpec((n_seq, c, w), lambda i: (0, i, 0)),
                  pl.BlockSpec((n_seq, 1, w, w), lambda i: (0, i, 0, 0)),
                  pl.BlockSpec((n_seq, 1, HEAD_DIM, w), lambda i: (0, i, 0, 0)),
                  pl.BlockSpec((n_seq, 1, 1, w), lambda i: (0, i, 0, 0))],
        out_specs=pl.BlockSpec((n_seq, c, w), lambda i: (0, i, 0)),
        out_shape=jax.ShapeDtypeStruct((n_seq, t, w), _F32),
        scratch_shapes=[pltpu.VMEM((n_seq, HEAD_DIM, w), _F32)],
        compiler_params=pltpu.CompilerParams(
            dimension_semantics=("arbitrary",),
            vmem_limit_bytes=48 * 1024 * 1024),
        name="rw_state_pass",
    )(r1, y0, m_bd, sadd, wlast)
    return y


ATTN_TQ = 256
ATTN_TK = 768


LANES = 128
LOG2_E = 1.4426950408889634


def _dense_attn_kernel(q_ref, k_ref, v_ref, o_ref, *, tk):
    g, tq, dq = q_ref.shape[2:]
    dv = o_ref.shape[-1]
    n_keys = k_ref.shape[2]
    q = q_ref[0, 0].reshape(g * tq, dq)

    def scores(i):
        kc = k_ref[0, 0, pl.ds(i * tk, tk), :]
        return lax.dot_general(q, kc, (((1,), (1,)), ((), ())), preferred_element_type=_F32)

    def update(i, m, acc, s):
        vc = v_ref[0, 0, pl.ds(i * tk, tk), :]
        m_new = jnp.maximum(m, jnp.max(s, axis=-1, keepdims=True))
        p = jnp.exp2(s - m_new).astype(_BF)
        return m_new, jnp.exp2(m - m_new) * acc + jnp.dot(p, vc, preferred_element_type=_F32)

    n_chunks = n_keys // tk
    m = jnp.full((g * tq, 1), -jnp.inf, _F32)
    acc = jnp.zeros((g * tq, v_ref.shape[-1]), _F32)
    s = scores(0)
    for i in range(n_chunks):
        s_next = scores(i + 1) if i + 1 < n_chunks else None
        m, acc = update(i, m, acc, s)
        s = s_next
    o_ref[0, 0] = (acc[:, :dv] / acc[:, dv:dv + 1]).reshape(g, tq, dv)


def dense_block_attention(q, k, v, scale):
    b, s_len, hk, g, dq = q.shape
    n_keys, dv = k.shape[1], v.shape[-1]
    tq = min(ATTN_TQ, s_len)
    tk = ATTN_TK if n_keys % ATTN_TK == 0 else n_keys
    assert s_len % tq == 0
    qt = jnp.transpose(q * (scale * LOG2_E), (0, 2, 3, 1, 4)).astype(_BF)
    kt = jnp.transpose(k, (0, 2, 1, 3)).astype(_BF)
    vt = jnp.transpose(v, (0, 2, 1, 3))
    vt = jnp.concatenate([vt, jnp.ones_like(vt[..., :1]),
                          jnp.zeros(vt.shape[:-1] + (LANES - dv - 1,), vt.dtype)], axis=-1).astype(_BF)
    o = pl.pallas_call(
        functools.partial(_dense_attn_kernel, tk=tk),
        grid=(b, hk, s_len // tq),
        in_specs=[pl.BlockSpec((1, 1, g, tq, dq), lambda bi, h, i: (bi, h, 0, i, 0)),
                  pl.BlockSpec((1, 1, n_keys, dq), lambda bi, h, i: (bi, h, 0, 0)),
                  pl.BlockSpec((1, 1, n_keys, LANES), lambda bi, h, i: (bi, h, 0, 0))],
        out_specs=pl.BlockSpec((1, 1, g, tq, dv), lambda bi, h, i: (bi, h, 0, i, 0)),
        out_shape=jax.ShapeDtypeStruct((b, hk, g, s_len, dv), _F32),
        compiler_params=pltpu.CompilerParams(
            dimension_semantics=("parallel", "parallel", "parallel"),
            vmem_limit_bytes=48 * 1024 * 1024),
        name="dense_attention",
    )(qt, kt, vt)
    return jnp.transpose(o, (0, 3, 1, 2, 4)).reshape(b, s_len, hk * g * dv)


def rms_norm(x, g):
    xf = x.astype(jnp.float32)
    y = xf * lax.rsqrt(jnp.mean(xf * xf, axis=-1, keepdims=True) + NORM_EPS)
    return (y * g.astype(jnp.float32)).astype(x.dtype)


def split_cols(p, widths):
    out, off = [], 0
    for w in widths:
        out.append(p[..., off:off + w])
        off += w
    return out


def axial_rope_tables(n_tokens, rot_dim):
    rows = n_tokens // GRID_W
    row = jnp.repeat(jnp.arange(rows, dtype=jnp.float32), GRID_W)
    col = jnp.tile(jnp.arange(GRID_W, dtype=jnp.float32), rows)
    n_freq = rot_dim // 4
    inv_freq = ROPE_BASE ** (-jnp.arange(n_freq, dtype=jnp.float32) / n_freq)
    ang = jnp.concatenate([row[:, None] * inv_freq, col[:, None] * inv_freq], axis=-1)
    return jnp.cos(ang), jnp.sin(ang)


def apply_rope(x, cos, sin):
    half = x.shape[-1] // 2
    c = cos[None, :, None, :].astype(x.dtype)
    s = sin[None, :, None, :].astype(x.dtype)
    x1, x2 = x[..., :half], x[..., half:]
    return jnp.concatenate([x1 * c - x2 * s, x1 * s + x2 * c], axis=-1)


def context_attention(q, k, v, scale, sink=None):
    b, n = q.shape[:2]
    s = jnp.einsum('bqhgd,bkhd->bhgqk', q, k).astype(jnp.float32) * scale
    if sink is not None:
        s_sink = jnp.broadcast_to(sink[None, :, :, None, None].astype(jnp.float32), s.shape[:-1] + (1,))
        p = jax.nn.softmax(jnp.concatenate([s_sink, s], axis=-1), axis=-1)[..., 1:]
    else:
        p = jax.nn.softmax(s, axis=-1)
    o = jnp.einsum('bhgqk,bkhd->bqhgd', p.astype(v.dtype), v)
    return o.reshape(b, n, -1)


def banded_window_attention(q, k, v, k_ctx, v_ctx, sink, scale):
    b, s_len, hk, g, d = q.shape
    nb = s_len // BLOCK
    qb = q.reshape(b, nb, BLOCK, hk, g, d)

    def neighbours(t):
        tb = jnp.pad(t.reshape(b, nb, BLOCK, hk, t.shape[-1]), ((0, 0), (1, 1), (0, 0), (0, 0), (0, 0)))
        return jnp.concatenate([tb[:, :-2], tb[:, 1:-1], tb[:, 2:]], axis=2)

    kb, vb = neighbours(k), neighbours(v)
    blk = jnp.arange(nb)
    q_pos = blk[:, None] * BLOCK + jnp.arange(BLOCK)[None, :]
    k_pos = (blk[:, None] - 1) * BLOCK + jnp.arange(3 * BLOCK)[None, :]
    valid = ((jnp.abs(q_pos[:, :, None] - k_pos[:, None, :]) <= WINDOW)
             & (k_pos >= 0)[:, None, :] & (k_pos < s_len)[:, None, :])
    s_win = jnp.einsum('bnqhgd,bnkhd->bnhgqk', qb, kb).astype(jnp.float32) * scale
    s_win = jnp.where(valid[None, :, None, None], s_win, NEG_INF)
    s_ctx = jnp.einsum('bnqhgd,bchd->bnhgqc', qb, k_ctx).astype(jnp.float32) * scale
    s_sink = jnp.broadcast_to(sink[None, None, :, :, None, None].astype(jnp.float32), s_ctx.shape[:-1] + (1,))
    p = jax.nn.softmax(jnp.concatenate([s_sink, s_ctx, s_win], axis=-1), axis=-1).astype(v.dtype)
    n_ctx = k_ctx.shape[1]
    o = (jnp.einsum('bnhgqc,bchd->bnqhgd', p[..., 1:1 + n_ctx], v_ctx)
         + jnp.einsum('bnhgqk,bnkhd->bnqhgd', p[..., 1 + n_ctx:], vb))
    return o.reshape(b, s_len, -1)


def gqa_heads(q, k, v, n_heads, n_kv, rope=None, q_gain=None, k_gain=None):
    b, n, _ = q.shape
    q = q.reshape(b, n, n_heads, HEAD_DIM)
    k = k.reshape(b, n, n_kv, HEAD_DIM)
    v = v.reshape(b, n, n_kv, HEAD_DIM)
    if q_gain is not None:
        q = rms_norm(q, q_gain)
        k = rms_norm(k, k_gain)
    if rope is not None:
        q = apply_rope(q, rope[0], rope[1])
        k = apply_rope(k, rope[0], rope[1])
    return q.reshape(b, n, n_kv, n_heads // n_kv, HEAD_DIM), k, v


def window_mixer(pc, pl_, sink, rope, ctx_out):
    scale = HEAD_DIM ** -0.5
    qc, kc, vc = gqa_heads(pc[0], pc[1], pc[2], SW_HEADS, SW_KV_HEADS)
    ql, kl, vl = gqa_heads(pl_[0], pl_[1], pl_[2], SW_HEADS, SW_KV_HEADS, rope)
    sink = sink.reshape(SW_KV_HEADS, SW_HEADS // SW_KV_HEADS)
    y_lat = banded_window_attention(ql, kl, vl, kc, vc, sink, scale)
    y_ctx = context_attention(qc, kc, vc, scale, sink) if ctx_out else None
    return y_ctx, y_lat


def grid_attention_mixer(pc, pl_, q_gain, k_gain, rope, ctx_out):
    scale = HEAD_DIM ** -0.5
    qc, kc, vc = gqa_heads(pc[0], pc[1], pc[2], GA_HEADS, GA_KV_HEADS, None, q_gain, k_gain)
    ql, kl, vl = gqa_heads(pl_[0], pl_[1], pl_[2], GA_HEADS, GA_KV_HEADS, rope, q_gain, k_gain)
    y_lat = dense_block_attention(ql, jnp.concatenate([kc, kl], axis=1), jnp.concatenate([vc, vl], axis=1), scale)
    y_ctx = context_attention(qc, kc, vc, scale) if ctx_out else None
    return y_ctx, y_lat


def mla_mixer(pc, pl_, q_gain, w_uq, kv_gain, w_ukv, rope, ctx_out):
    scale = (MLA_NOPE + MLA_ROPE) ** -0.5

    def project(cq, ckv, k_rope, rope_tab):
        b, n, _ = cq.shape
        q = (rms_norm(cq, q_gain) @ w_uq).reshape(b, n, MLA_HEADS, MLA_NOPE + MLA_ROPE)
        kv = (rms_norm(ckv, kv_gain) @ w_ukv).reshape(b, n, MLA_HEADS, MLA_NOPE + MLA_V)
        q_nope, q_pe = q[..., :MLA_NOPE], q[..., MLA_NOPE:]
        k_nope, v = kv[..., :MLA_NOPE], kv[..., MLA_NOPE:]
        k_pe = k_rope[:, :, None, :]
        if rope_tab is not None:
            q_pe = apply_rope(q_pe, rope_tab[0], rope_tab[1])
            k_pe = apply_rope(k_pe, rope_tab[0], rope_tab[1])
        q = jnp.concatenate([q_nope, q_pe], axis=-1)
        k = jnp.concatenate([k_nope, jnp.broadcast_to(k_pe, (b, n, MLA_HEADS, MLA_ROPE))], axis=-1)
        return q[:, :, :, None, :], k, v

    qc, kc, vc = project(pc[0], pc[1], pc[2], None)
    ql, kl, vl = project(pl_[0], pl_[1], pl_[2], rope)
    y_lat = dense_block_attention(ql, jnp.concatenate([kc, kl], axis=1), jnp.concatenate([vc, vl], axis=1), scale)
    y_ctx = context_attention(qc, kc, vc, scale) if ctx_out else None
    return y_ctx, y_lat


def centred_token_shift(f, mu):
    prev = jnp.pad(f[:, :-1], ((0, 0), (1, 0), (0, 0)))
    nxt = jnp.pad(f[:, 1:], ((0, 0), (0, 1), (0, 0)))
    return f + mu[0] * (prev - f) + mu[1] * (nxt - f)


def rwkv7_mixer(f_ctx, f_lat, vf_ctx, vf_lat, rw, ctx_out):
    def heads(t):
        return t.reshape(t.shape[:-1] + (RW_HEADS, HEAD_DIM))

    r_k = rw['r_k'].reshape(RW_HEADS, HEAD_DIM)

    def features(f, vf):
        f = centred_token_shift(f, rw['mu'])
        r, k, v, z = jnp.split(f, 4, axis=-1)
        if vf is None:
            vf = v
        else:
            v = v + (vf - v) * jax.nn.sigmoid(rw['v0'] + (z @ rw['v1']) @ rw['v2'])
        kk = heads(k * rw['k_k']).astype(jnp.float32)
        kk = (kk * lax.rsqrt(jnp.sum(kk * kk, axis=-1, keepdims=True) + 1e-12)).astype(f.dtype)
        g = jax.nn.sigmoid(z @ rw['g1']) @ rw['g2']
        per_dir = []
        for d in range(2):
            w = -jax.nn.softplus(-(rw['w0'][d] + jnp.tanh(z @ rw['w1'][d]) @ rw['w2'][d])) - 0.5
            log_decay = -jnp.exp(w.astype(jnp.float32))
            a = jax.nn.sigmoid(rw['a0'][d] + (z @ rw['a1'][d]) @ rw['a2'][d])
            k_d = k * (1 + (a - 1) * rw['k_a'])
            per_dir.append((heads(log_decay), heads(a), heads(k_d)))
        return heads(r), heads(v), kk, g, per_dir, vf

    r_c, v_c, kk_c, g_c, dir_c, vf_ctx = features(f_ctx, vf_ctx)
    r_l, v_l, kk_l, g_l, dir_l, vf_lat = features(f_lat, vf_lat)
    n_c, n_l = f_ctx.shape[1], f_lat.shape[1]
    n_b = f_lat.shape[0]

    def sequence(t_ctx, t_lat, rev):
        t_ctx = jnp.broadcast_to(t_ctx, (n_b,) + t_ctx.shape[1:])
        if rev:
            t_ctx, t_lat = jnp.flip(t_ctx, axis=1), jnp.flip(t_lat, axis=1)
        return jnp.concatenate([t_ctx, t_lat], axis=1).reshape(n_b, n_c + n_l, RW_WIDTH)

    seqs = [[], [], [], [], [], []]
    for d in range(2):
        rev = d == 1
        lw_c, a_c, kd_c = dir_c[d]
        lw_l, a_l, kd_l = dir_l[d]
        for lst, (tc, tl) in zip(seqs, ((r_c, r_l), (lw_c, lw_l), (kd_c, kd_l), (v_c, v_l),
                                        (-kk_c, -kk_l), (kk_c * a_c, kk_l * a_l))):
            lst.append(sequence(tc, tl, rev))
    y_all = rwkv_scan_chunked(*[jnp.concatenate(lst, axis=0) for lst in seqs])
    y_all = y_all.reshape(2, n_b, n_c + n_l, RW_HEADS, HEAD_DIM)
    y_c, y_l, bonus_c, bonus_l = 0.0, 0.0, 0.0, 0.0
    for d in range(2):
        rev = d == 1
        y_dc, y_dl = y_all[d, :, :n_c], y_all[d, :, n_c:]
        if rev:
            y_dc, y_dl = jnp.flip(y_dc, axis=1), jnp.flip(y_dl, axis=1)
        _, _, k_d = dir_c[d]
        if ctx_out:
            y_c = y_c + y_dc
            bonus_c = bonus_c + jnp.sum(r_c * k_d * r_k, axis=-1, keepdims=True) * v_c
        _, _, k_d = dir_l[d]
        y_l = y_l + y_dl
        bonus_l = bonus_l + jnp.sum(r_l * k_d * r_k, axis=-1, keepdims=True) * v_l

    ln_w = rw['ln_w'].reshape(RW_HEADS, HEAD_DIM).astype(jnp.float32)
    ln_b = rw['ln_b'].reshape(RW_HEADS, HEAD_DIM).astype(jnp.float32)

    def finish(y, bonus, g):
        mu = jnp.mean(y, axis=-1, keepdims=True)
        var = jnp.mean(jnp.square(y - mu), axis=-1, keepdims=True)
        yn = (y - mu) * lax.rsqrt(var + RW_GN_EPS) * ln_w + ln_b
        o = yn.astype(g.dtype) + bonus
        return o.reshape(g.shape) * g

    y_lat = finish(y_l, bonus_l, g_l)
    y_ctx = finish(y_c, bonus_c, g_c) if ctx_out else None
    return y_ctx, y_lat, vf_ctx, vf_lat


def hybrid_token_mixer(h_ctx, h_lat, vf_ctx, vf_lat, w_in, w_branch, w_out, rw, sw_sink,
                       mla_q_norm, mla_w_uq, mla_kv_norm, mla_w_ukv, ga_q_norm, ga_k_norm,
                       rope64, rope32, ctx_out):
    pc = split_cols(pmm_nd(h_ctx, w_in), IN_WIDTHS)
    pl_ = split_cols(pmm_nd(h_lat, w_in), IN_WIDTHS)
    ya_c, ya_l, vf_ctx, vf_lat = rwkv7_mixer(pc[0], pl_[0], vf_ctx, vf_lat, rw, ctx_out)
    yb_c, yb_l = window_mixer(pc[1:4], pl_[1:4], sw_sink, rope64, ctx_out)
    yc_c, yc_l = mla_mixer(pc[4:7], pl_[4:7], mla_q_norm, mla_w_uq, mla_kv_norm, mla_w_ukv, rope32, ctx_out)
    yd_c, yd_l = grid_attention_mixer(pc[7:10], pl_[7:10], ga_q_norm, ga_k_norm, rope64, ctx_out)

    def merge(ys, gate_cols):
        gates = jax.nn.sigmoid(gate_cols.astype(jnp.float32)).astype(gate_cols.dtype)
        m = None
        for i, y in enumerate(ys):
            term = gates[..., i * D_MODEL:(i + 1) * D_MODEL] * pmm_nd(y, w_branch[i])
            m = term if m is None else m + term
        return pmm_nd(m, w_out)

    out_lat = merge([ya_l, yb_l, yc_l, yd_l], pl_[10])
    out_ctx = merge([ya_c, yb_c, yc_c, yd_c], pc[10]) if ctx_out else None
    return out_ctx, out_lat, vf_ctx, vf_lat


FFN_ROWS = 512
FFN_COLS = 512


def _swiglu_kernel(be_ref, x_ref, w1_ref, w3_ref, w2_ref, g_ref, o_ref, acc_ref):
    del be_ref
    f = pl.program_id(1)

    @pl.when(f == 0)
    def _():
        acc_ref[...] = jnp.zeros_like(acc_ref)

    x = x_ref[...]
    h1 = jnp.dot(x, w1_ref[0], preferred_element_type=_F32)
    h3 = jnp.dot(x, w3_ref[0], preferred_element_type=_F32)
    h = (h1 * jax.nn.sigmoid(h1) * h3).astype(_BF)
    acc_ref[...] += jnp.dot(h, w2_ref[0], preferred_element_type=_F32)

    @pl.when(f == pl.num_programs(1) - 1)
    def _():
        o_ref[...] = acc_ref[...] * g_ref[...]


def grouped_swiglu(x, block_e, gate, w1, w3, w2):
    n_rows, d = x.shape
    n_f = w1.shape[-1]
    assert n_rows % FFN_ROWS == 0 and n_f % FFN_COLS == 0
    grid_spec = pltpu.PrefetchScalarGridSpec(
        num_scalar_prefetch=1,
        grid=(n_rows // FFN_ROWS, n_f // FFN_COLS),
        in_specs=[pl.BlockSpec((FFN_ROWS, d), lambda i, f, be: (i, 0)),
                  pl.BlockSpec((1, d, FFN_COLS), lambda i, f, be: (be[i], 0, f)),
                  pl.BlockSpec((1, d, FFN_COLS), lambda i, f, be: (be[i], 0, f)),
                  pl.BlockSpec((1, FFN_COLS, d), lambda i, f, be: (be[i], f, 0)),
                  pl.BlockSpec((FFN_ROWS, 1), lambda i, f, be: (i, 0))],
        out_specs=pl.BlockSpec((FFN_ROWS, d), lambda i, f, be: (i, 0)),
        scratch_shapes=[pltpu.VMEM((FFN_ROWS, d), _F32)])
    return pl.pallas_call(
        _swiglu_kernel,
        grid_spec=grid_spec,
        out_shape=jax.ShapeDtypeStruct((n_rows, d), _F32),
        compiler_params=pltpu.CompilerParams(
            dimension_semantics=("parallel", "arbitrary"),
            vmem_limit_bytes=48 * 1024 * 1024),
        name="grouped_swiglu",
    )(block_e, x.astype(_BF), w1.astype(_BF), w3.astype(_BF), w2.astype(_BF), gate)


def swiglu(h, w1, w3, w2):
    n_tok = h.shape[0]
    n_rows = _round_up(n_tok, FFN_ROWS)
    x = jnp.pad(h, ((0, n_rows - n_tok), (0, 0))) if n_rows != n_tok else h
    out = grouped_swiglu(x, jnp.zeros((n_rows // FFN_ROWS,), jnp.int32), jnp.ones((n_rows, 1), _F32),
                         w1[None], w3[None], w2[None])
    return out[:n_tok]


def moe_swiglu(h, w_router, w1, w3, w2):
    n_tok, d = h.shape
    logits = (h @ w_router).astype(jnp.float32)
    top_logit, top_idx = lax.top_k(logits, TOP_K)
    gates = jax.nn.softmax(top_logit, axis=-1)
    n_assign = n_tok * TOP_K
    flat_e = top_idx.reshape(-1)
    order = jnp.argsort(flat_e)
    sorted_e = flat_e[order]
    sorted_tok = (order // TOP_K).astype(jnp.int32)
    sorted_gate = gates.reshape(-1)[order]
    counts = jnp.bincount(flat_e, length=N_EXPERTS)
    padded = (counts + FFN_ROWS - 1) // FFN_ROWS * FFN_ROWS
    start = jnp.cumsum(counts) - counts
    pend = jnp.cumsum(padded)
    pstart = pend - padded
    dest = pstart[sorted_e] + jnp.arange(n_assign) - start[sorted_e]
    n_blocks = -(-n_assign // FFN_ROWS) + N_EXPERTS
    n_rows = n_blocks * FFN_ROWS
    tok = jnp.zeros((n_rows,), jnp.int32).at[dest].set(sorted_tok)
    gate = jnp.zeros((n_rows,), h.dtype).at[dest].set(sorted_gate.astype(h.dtype))
    block_e = jnp.minimum(jnp.searchsorted(pend, jnp.arange(n_blocks) * FFN_ROWS, side='right'),
                          N_EXPERTS - 1).astype(jnp.int32)
    xb = h.astype(_BF)[tok]
    yb = grouped_swiglu(xb, block_e, gate[:, None], w1, w3, w2)
    return jnp.zeros_like(h).at[tok].add(yb)


def kernel(x, c, ctx, c_ctx, w_mod, b_mod, norm1, norm2, w_in, w_branch, w_out,
           rw_mu, rw_w0, rw_w1, rw_w2, rw_a0, rw_a1, rw_a2, rw_v0, rw_v1, rw_v2,
           rw_g1, rw_g2, rw_k_k, rw_k_a, rw_r_k, rw_ln_w, rw_ln_b, sw_sink,
           mla_q_norm, mla_w_uq, mla_kv_norm, mla_w_ukv, ga_q_norm, ga_k_norm,
           ffn_w1, ffn_w3, ffn_w2, moe_router, moe_w1, moe_w3, moe_w2, final_norm):
    b, s_len, d = x.shape
    n_ctx_tok = ctx.shape[1]
    rope64 = axial_rope_tables(s_len, HEAD_DIM)
    rope32 = axial_rope_tables(s_len, MLA_ROPE)
    silu_c = jax.nn.silu(c)
    silu_cc = jax.nn.silu(c_ctx)
    x_lat, x_ctx = x, ctx
    vf_lat, vf_ctx = None, None
    for l in range(DEPTH):
        ctx_out = l < DEPTH - 1
        mod_l = silu_c @ w_mod[l] + b_mod[l]
        mod_c = silu_cc @ w_mod[l] + b_mod[l]
        sh1, sc1, g1, sh2, sc2, g2 = jnp.split(mod_l[:, None, :], 6, axis=-1)
        csh1, csc1, cg1, csh2, csc2, cg2 = jnp.split(mod_c, 6, axis=-1)

        h_lat = rms_norm(x_lat, norm1[l]) * (1 + sc1) + sh1
        h_ctx = rms_norm(x_ctx, norm1[l]) * (1 + csc1) + csh1
        rw = dict(mu=rw_mu[l], w0=rw_w0[l], w1=rw_w1[l], w2=rw_w2[l],
                  a0=rw_a0[l], a1=rw_a1[l], a2=rw_a2[l],
                  v0=rw_v0[l - 1] if l > 0 else None,
                  v1=rw_v1[l - 1] if l > 0 else None,
                  v2=rw_v2[l - 1] if l > 0 else None,
                  g1=rw_g1[l], g2=rw_g2[l], k_k=rw_k_k[l], k_a=rw_k_a[l], r_k=rw_r_k[l],
                  ln_w=rw_ln_w[l], ln_b=rw_ln_b[l])
        out_ctx, out_lat, vf_ctx, vf_lat = hybrid_token_mixer(
            h_ctx, h_lat, vf_ctx, vf_lat, w_in[l], w_branch[l], w_out[l], rw, sw_sink[l],
            mla_q_norm[l], mla_w_uq[l], mla_kv_norm[l], mla_w_ukv[l], ga_q_norm[l], ga_k_norm[l],
            rope64, rope32, ctx_out)
        x_lat = x_lat + g1 * out_lat

        f_lat = rms_norm(x_lat, norm2[l]) * (1 + sc2) + sh2
        if ctx_out:
            x_ctx = x_ctx + cg1 * out_ctx
            f_ctx = rms_norm(x_ctx, norm2[l]) * (1 + csc2) + csh2
            tokens = jnp.concatenate([f_ctx.reshape(-1, d), f_lat.reshape(-1, d)], axis=0)
            n_c = b * n_ctx_tok
        else:
            tokens = f_lat.reshape(-1, d)
            n_c = 0
        if l % 2 == 0:
            ffn_out = swiglu(tokens, ffn_w1[l // 2], ffn_w3[l // 2], ffn_w2[l // 2])
        else:
            ffn_out = moe_swiglu(tokens, moe_router[l // 2], moe_w1[l // 2], moe_w3[l // 2], moe_w2[l // 2])
        x_lat = x_lat + g2 * ffn_out[n_c:].reshape(b, s_len, d)
        if ctx_out:
            x_ctx = x_ctx + cg2 * ffn_out[:n_c].reshape(b, n_ctx_tok, d)
    return rms_norm(x_lat, final_norm)
```

```python
import functools

import jax
import jax.numpy as jnp
from jax import lax
from jax.experimental import pallas as pl
from jax.experimental.pallas import tpu as pltpu

D_MODEL = 1024
DEPTH = 4
GRID_W = 64
HEAD_DIM = 64
BLOCK = 128
WINDOW = 128
ROPE_BASE = 10000.0
NORM_EPS = 1e-6
NEG_INF = -1e30
N_BRANCH = 4
BRANCH_WIDTH = 256
RW_HEADS = 4
RW_WIDTH = RW_HEADS * HEAD_DIM
RW_GN_EPS = 64e-5
SW_HEADS = 4
SW_KV_HEADS = 2
MLA_HEADS = 4
MLA_NOPE = 64
MLA_ROPE = 32
MLA_V = 64
MLA_Q_RANK = 256
MLA_KV_RANK = 128
GA_HEADS = 4
GA_KV_HEADS = 2
N_EXPERTS = 8
TOP_K = 2
MOE_BLOCK = 128
IN_WIDTHS = (4 * RW_WIDTH,
             SW_HEADS * HEAD_DIM, SW_KV_HEADS * HEAD_DIM, SW_KV_HEADS * HEAD_DIM,
             MLA_Q_RANK, MLA_KV_RANK, MLA_ROPE,
             GA_HEADS * HEAD_DIM, GA_KV_HEADS * HEAD_DIM, GA_KV_HEADS * HEAD_DIM,
             N_BRANCH * D_MODEL)


def _mm_kernel(a_ref, b_ref, o_ref):
    o_ref[...] = jnp.dot(a_ref[...], b_ref[...], preferred_element_type=jnp.float32)


def _round_up(x, m):
    return (x + m - 1) // m * m


def pmm(a, b, tm=512, tn=512):
    m, k = a.shape
    _, n = b.shape
    tm = min(tm, _round_up(m, 8))
    tn = min(tn, _round_up(n, 128))
    mp, np_ = _round_up(m, tm), _round_up(n, tn)
    a = a.astype(jnp.bfloat16)
    b = b.astype(jnp.bfloat16)
    if mp != m:
        a = jnp.pad(a, ((0, mp - m), (0, 0)))
    if np_ != n:
        b = jnp.pad(b, ((0, 0), (0, np_ - n)))
    out = pl.pallas_call(
        _mm_kernel,
        grid=(np_ // tn, mp // tm),
        in_specs=[pl.BlockSpec((tm, k), lambda j, i: (i, 0)),
                  pl.BlockSpec((k, tn), lambda j, i: (0, j))],
        out_specs=pl.BlockSpec((tm, tn), lambda j, i: (i, j)),
        out_shape=jax.ShapeDtypeStruct((mp, np_), jnp.float32),
        compiler_params=pltpu.CompilerParams(
            dimension_semantics=("arbitrary", "arbitrary"),
            vmem_limit_bytes=48 * 1024 * 1024),
        name="pmm",
    )(a, b)
    return out[:m, :n]


def pmm_nd(a, b, **kw):
    lead = a.shape[:-1]
    return pmm(a.reshape(-1, a.shape[-1]), b, **kw).reshape(lead + (b.shape[-1],))


RW_CHUNK = 64
RW_CHUNKS_PER_STEP = 6
RW_INV_BASE = 4

_BF = jnp.bfloat16
_F32 = jnp.float32


def _bd_rows(x):
    lane_head = lax.broadcasted_iota(jnp.int32, x.shape, 1) // HEAD_DIM
    return jnp.concatenate([jnp.where(lane_head == h, x, 0.0) for h in range(RW_HEADS)], axis=0)


def _dot(a, b):
    return jnp.dot(a.astype(_BF), b.astype(_BF), preferred_element_type=_F32)


def _dot_nt(a, b):
    return lax.dot_general(a.astype(_BF), b.astype(_BF), (((1,), (1,)), ((), ())),
                           preferred_element_type=_F32)


def _dot_tn(a, b):
    return lax.dot_general(a.astype(_BF), b.astype(_BF), (((0,), (0,)), ((), ())),
                           preferred_element_type=_F32)


def _split3(x):
    h1 = x.astype(_BF)
    r1 = x - h1.astype(_F32)
    h2 = r1.astype(_BF)
    h3 = (r1 - h2.astype(_F32)).astype(_BF)
    return h1, h2, h3


def _each(fn, *lists):
    return [fn(*args) for args in zip(*lists)]


def _chunk_summaries(r, lw, k, v, a, b, reverse):
    c = RW_CHUNK
    row = lax.broadcasted_iota(jnp.int32, (c, c), 0)
    col = lax.broadcasted_iota(jnp.int32, (c, c), 1)
    tri = ((row <= col) if reverse else (row >= col)).astype(_BF)
    cum = _each(lambda x: sum(jnp.dot(tri, p, preferred_element_type=_F32) for p in _split3(x)), lw)
    lwlast = _each(lambda x: x[0:1, :] if reverse else x[c - 1:c, :], cum)
    rt = _each(lambda x, cu: x * jnp.exp(cu), r, cum)
    at = _each(lambda x, cu, l: x * jnp.exp(cu - l), a, cum, lw)
    w_inv = _each(lambda cu: jnp.exp(-cu), cum)
    w_tail = _each(lambda cu, ll: jnp.exp(ll - cu), cum, lwlast)
    wlast = _each(jnp.exp, lwlast)
    bt = _each(jnp.multiply, b, w_inv)
    kt = _each(jnp.multiply, k, w_inv)
    bh = _each(jnp.multiply, b, w_tail)
    kh = _each(jnp.multiply, k, w_tail)

    t_idx = lax.broadcasted_iota(jnp.int32, (c, RW_WIDTH), 0)
    s_idx = lax.broadcasted_iota(jnp.int32, (c, RW_WIDTH), 1) % HEAD_DIM
    strict = (s_idx > t_idx) if reverse else (s_idx < t_idx)
    incl = (s_idx >= t_idx) if reverse else (s_idx <= t_idx)

    l1 = _each(lambda x, y: jnp.concatenate([x, y], axis=0), at, rt)
    g_b = _each(lambda x, y: _dot_nt(x, _bd_rows(y)), l1, bt)
    g_k = _each(lambda x, y: _dot_nt(x, _bd_rows(y)), l1, kt)
    a_ab = _each(lambda g: jnp.where(strict, g[:c], 0.0), g_b)
    a_rb = _each(lambda g: jnp.where(incl, g[c:], 0.0), g_b)
    a_ak = _each(lambda g: jnp.where(strict, g[:c], 0.0), g_k)
    a_rk = _each(lambda g: jnp.where(incl, g[c:], 0.0), g_k)

    eye = jnp.where(s_idx == t_idx, 1.0, 0.0)
    base = RW_INV_BASE
    n_b = _each(lambda x: jnp.where(t_idx // base == s_idx // base, x, 0.0), a_ab)
    n_b2 = _each(lambda x: _dot(x, _bd_rows(x)), n_b)
    n_b3 = _each(lambda x, y: _dot(x, _bd_rows(y)), n_b, n_b2)
    tm = _each(lambda x, y, z: eye + x + y + z, n_b, n_b2, n_b3)
    blk = base
    while blk < c:
        off = (t_idx // (2 * blk) == s_idx // (2 * blk)) & (t_idx // blk != s_idx // blk)
        x = _each(lambda t, n: _dot(t, _bd_rows(jnp.where(off, n, 0.0))), tm, a_ab)
        tm = _each(lambda t, xx: t + _dot(xx, _bd_rows(t)), tm, x)
        blk *= 2

    av = _each(lambda x, y, vv: _dot(jnp.concatenate([x, y], axis=0), _bd_rows(vv)), a_ak, a_rk, v)
    ta = _each(lambda t, x: _dot(t, _bd_rows(x)), tm, at)
    u0 = _each(lambda t, x: _dot(t, _bd_rows(x[:c])), tm, av)
    r1 = _each(lambda x, ar, t: x + _dot(ar, _bd_rows(t)), rt, a_rb, ta)
    y0 = _each(lambda x, ar, u: x[c:] + _dot(ar, _bd_rows(u)), av, a_rb, u0)

    blk_r = lax.broadcasted_iota(jnp.int32, (RW_WIDTH, RW_WIDTH), 0) // HEAD_DIM
    blk_c = lax.broadcasted_iota(jnp.int32, (RW_WIDTH, RW_WIDTH), 1) // HEAD_DIM
    m_bd = _each(lambda t, x: jnp.where(blk_r == blk_c, _dot_tn(t, x), 0.0), ta, bh)
    z = _each(lambda u, vv, x, y: _dot_tn(jnp.concatenate([u, vv], axis=0),
                                          jnp.concatenate([x, y], axis=0)), u0, v, bh, kh)
    lane_head = lax.broadcasted_iota(jnp.int32, (HEAD_DIM, RW_WIDTH), 1) // HEAD_DIM

    def diag_blocks(zz):
        out = zz[:HEAD_DIM]
        for h in range(1, RW_HEADS):
            out = jnp.where(lane_head == h, zz[h * HEAD_DIM:(h + 1) * HEAD_DIM], out)
        return out

    sadd = _each(diag_blocks, z)
    return r1, y0, m_bd, sadd, wlast


def _rw_summary_kernel(r_ref, lw_ref, k_ref, v_ref, a_ref, b_ref,
                       r1_ref, y0_ref, m_ref, sadd_ref, wl_ref, *, reverse):
    c = RW_CHUNK

    def chunks(ref):
        return [ref[0, pl.ds(j * c, c), :] for j in range(RW_CHUNKS_PER_STEP)]

    r1, y0, m_bd, sadd, wlast = _chunk_summaries(
        chunks(r_ref), chunks(lw_ref), chunks(k_ref), chunks(v_ref), chunks(a_ref), chunks(b_ref),
        reverse)
    r1_ref[0] = jnp.concatenate(r1, axis=0)
    y0_ref[0] = jnp.concatenate(y0, axis=0)
    m_ref[0] = jnp.stack(m_bd, axis=0).astype(_BF)
    sadd_ref[0] = jnp.stack(sadd, axis=0)
    wl_ref[0] = jnp.stack(wlast, axis=0)


def _rw_state_kernel(*refs):
    s_ref = refs[-1]
    y_refs = refs[10:12]

    @pl.when(pl.program_id(0) == 0)
    def _():
        s_ref[...] = jnp.zeros_like(s_ref)

    n_b = y_refs[0].shape[0]
    seqs = [(d, q) for d in range(2) for q in range(n_b)]

    def part(j):
        return [refs[5 * d + j][(q, 0) if j >= 2 else q] for d, q in seqs]

    r1, y0, m_bd, sadd, wlast = (part(j) for j in range(5))
    s = [s_ref[i] for i in range(len(seqs))]
    y = _each(lambda rr, ss, yy: _dot_nt(rr, _bd_rows(ss)) + yy, r1, s, y0)
    s_new = _each(lambda ss, ww, mm, aa: ss * ww + _dot(ss, mm) + aa, s, wlast, m_bd, sadd)
    for i, (d, q) in enumerate(seqs):
        y_refs[d][q] = y[i]
        s_ref[i] = s_new[i]


def _rw_summaries(seq, reverse):
    n_b, t, w = seq[0].shape
    c, cps = RW_CHUNK, RW_CHUNKS_PER_STEP
    n_chunks = t // c
    assert t % (c * cps) == 0 and w == RW_WIDTH
    seq_spec = pl.BlockSpec((1, c * cps, w), lambda q, i: (q, i, 0))
    return pl.pallas_call(
        functools.partial(_rw_summary_kernel, reverse=reverse),
        grid=(n_b, n_chunks // cps),
        in_specs=[seq_spec] * 6,
        out_specs=[seq_spec, seq_spec,
                   pl.BlockSpec((1, cps, w, w), lambda q, i: (q, i, 0, 0)),
                   pl.BlockSpec((1, cps, HEAD_DIM, w), lambda q, i: (q, i, 0, 0)),
                   pl.BlockSpec((1, cps, 1, w), lambda q, i: (q, i, 0, 0))],
        out_shape=[jax.ShapeDtypeStruct((n_b, t, w), _F32),
                   jax.ShapeDtypeStruct((n_b, t, w), _F32),
                   jax.ShapeDtypeStruct((n_b, n_chunks, w, w), _BF),
                   jax.ShapeDtypeStruct((n_b, n_chunks, HEAD_DIM, w), _F32),
                   jax.ShapeDtypeStruct((n_b, n_chunks, 1, w), _F32)],
        compiler_params=pltpu.CompilerParams(
            dimension_semantics=("parallel", "parallel"),
            vmem_limit_bytes=48 * 1024 * 1024),
        name="rw_chunk_summary",
    )(*seq)


def rwkv_scan_chunked(seq_fwd, seq_bwd):
    n_b, t, w = seq_fwd[0].shape
    c = RW_CHUNK
    n_chunks = t // c
    summaries = _rw_summaries(seq_fwd, False) + _rw_summaries(seq_bwd, True)

    def specs(pos):
        return [pl.BlockSpec((n_b, c, w), lambda i: (0, pos(i), 0)),
                pl.BlockSpec((n_b, c, w), lambda i: (0, pos(i), 0)),
                pl.BlockSpec((n_b, 1, w, w), lambda i: (0, pos(i), 0, 0)),
                pl.BlockSpec((n_b, 1, HEAD_DIM, w), lambda i: (0, pos(i), 0, 0)),
                pl.BlockSpec((n_b, 1, 1, w), lambda i: (0, pos(i), 0, 0))]

    def fwd_pos(i):
        return i

    def bwd_pos(i):
        return n_chunks - 1 - i

    return pl.pallas_call(
        _rw_state_kernel,
        grid=(n_chunks,),
        in_specs=specs(fwd_pos) + specs(bwd_pos),
        out_specs=[pl.BlockSpec((n_b, c, w), lambda i: (0, fwd_pos(i), 0)),
                   pl.BlockSpec((n_b, c, w), lambda i: (0, bwd_pos(i), 0))],
        out_shape=[jax.ShapeDtypeStruct((n_b, t, w), _F32)] * 2,
        scratch_shapes=[pltpu.VMEM((2 * n_b, HEAD_DIM, w), _F32)],
        compiler_params=pltpu.CompilerParams(
            dimension_semantics=("arbitrary",),
            vmem_limit_bytes=48 * 1024 * 1024),
        name="rw_state_pass",
    )(*summaries)


ATTN_TQ = 256
ATTN_TK = 768


LANES = 128
LOG2_E = 1.4426950408889634


def _dense_attn_kernel(q_ref, k_ref, v_ref, o_ref, *, tk):
    g, tq, dq = q_ref.shape[2:]
    dv = o_ref.shape[-1]
    n_keys = k_ref.shape[2]
    q = q_ref[0, 0].reshape(g * tq, dq)

    def scores(i):
        kc = k_ref[0, 0, pl.ds(i * tk, tk), :]
        return lax.dot_general(q, kc, (((1,), (1,)), ((), ())), preferred_element_type=_F32)

    def update(i, m, acc, s):
        vc = v_ref[0, 0, pl.ds(i * tk, tk), :]
        m_new = jnp.maximum(m, jnp.max(s, axis=-1, keepdims=True))
        p = jnp.exp2(s - m_new).astype(_BF)
        return m_new, jnp.exp2(m - m_new) * acc + jnp.dot(p, vc, preferred_element_type=_F32)

    n_chunks = n_keys // tk
    m = jnp.full((g * tq, 1), -jnp.inf, _F32)
    acc = jnp.zeros((g * tq, v_ref.shape[-1]), _F32)
    s = scores(0)
    for i in range(n_chunks):
        s_next = scores(i + 1) if i + 1 < n_chunks else None
        m, acc = update(i, m, acc, s)
        s = s_next
    o_ref[0, 0] = (acc[:, :dv] / acc[:, dv:dv + 1]).reshape(g, tq, dv)


def dense_block_attention(q, k, v, scale):
    b, s_len, hk, g, dq = q.shape
    n_keys, dv = k.shape[1], v.shape[-1]
    tq = min(ATTN_TQ, s_len)
    tk = ATTN_TK if n_keys % ATTN_TK == 0 else n_keys
    assert s_len % tq == 0
    qt = jnp.transpose(q * (scale * LOG2_E), (0, 2, 3, 1, 4)).astype(_BF)
    kt = jnp.transpose(k, (0, 2, 1, 3)).astype(_BF)
    vt = jnp.transpose(v, (0, 2, 1, 3))
    vt = jnp.concatenate([vt, jnp.ones_like(vt[..., :1]),
                          jnp.zeros(vt.shape[:-1] + (LANES - dv - 1,), vt.dtype)], axis=-1).astype(_BF)
    o = pl.pallas_call(
        functools.partial(_dense_attn_kernel, tk=tk),
        grid=(b, hk, s_len // tq),
        in_specs=[pl.BlockSpec((1, 1, g, tq, dq), lambda bi, h, i: (bi, h, 0, i, 0)),
                  pl.BlockSpec((1, 1, n_keys, dq), lambda bi, h, i: (bi, h, 0, 0)),
                  pl.BlockSpec((1, 1, n_keys, LANES), lambda bi, h, i: (bi, h, 0, 0))],
        out_specs=pl.BlockSpec((1, 1, g, tq, dv), lambda bi, h, i: (bi, h, 0, i, 0)),
        out_shape=jax.ShapeDtypeStruct((b, hk, g, s_len, dv), _F32),
        compiler_params=pltpu.CompilerParams(
            dimension_semantics=("parallel", "parallel", "parallel"),
            vmem_limit_bytes=48 * 1024 * 1024),
        name="dense_attention",
    )(qt, kt, vt)
    return jnp.transpose(o, (0, 3, 1, 2, 4)).reshape(b, s_len, hk * g * dv)


def rms_norm(x, g):
    xf = x.astype(jnp.float32)
    y = xf * lax.rsqrt(jnp.mean(xf * xf, axis=-1, keepdims=True) + NORM_EPS)
    return (y * g.astype(jnp.float32)).astype(x.dtype)


def split_cols(p, widths):
    out, off = [], 0
    for w in widths:
        out.append(p[..., off:off + w])
        off += w
    return out


def axial_rope_tables(n_tokens, rot_dim):
    rows = n_tokens // GRID_W
    row = jnp.repeat(jnp.arange(rows, dtype=jnp.float32), GRID_W)
    col = jnp.tile(jnp.arange(GRID_W, dtype=jnp.float32), rows)
    n_freq = rot_dim // 4
    inv_freq = ROPE_BASE ** (-jnp.arange(n_freq, dtype=jnp.float32) / n_freq)
    ang = jnp.concatenate([row[:, None] * inv_freq, col[:, None] * inv_freq], axis=-1)
    return jnp.cos(ang), jnp.sin(ang)


def apply_rope(x, cos, sin):
    half = x.shape[-1] // 2
    c = cos[None, :, None, :].astype(x.dtype)
    s = sin[None, :, None, :].astype(x.dtype)
    x1, x2 = x[..., :half], x[..., half:]
    return jnp.concatenate([x1 * c - x2 * s, x1 * s + x2 * c], axis=-1)


def context_attention(q, k, v, scale, sink=None):
    b, n = q.shape[:2]
    s = jnp.einsum('bqhgd,bkhd->bhgqk', q, k).astype(jnp.float32) * scale
    if sink is not None:
        s_sink = jnp.broadcast_to(sink[None, :, :, None, None].astype(jnp.float32), s.shape[:-1] + (1,))
        p = jax.nn.softmax(jnp.concatenate([s_sink, s], axis=-1), axis=-1)[..., 1:]
    else:
        p = jax.nn.softmax(s, axis=-1)
    o = jnp.einsum('bhgqk,bkhd->bqhgd', p.astype(v.dtype), v)
    return o.reshape(b, n, -1)


def banded_window_attention(q, k, v, k_ctx, v_ctx, sink, scale):
    b, s_len, hk, g, d = q.shape
    nb = s_len // BLOCK
    qb = q.reshape(b, nb, BLOCK, hk, g, d)

    def neighbours(t):
        tb = jnp.pad(t.reshape(b, nb, BLOCK, hk, t.shape[-1]), ((0, 0), (1, 1), (0, 0), (0, 0), (0, 0)))
        return jnp.concatenate([tb[:, :-2], tb[:, 1:-1], tb[:, 2:]], axis=2)

    kb, vb = neighbours(k), neighbours(v)
    blk = jnp.arange(nb)
    q_pos = blk[:, None] * BLOCK + jnp.arange(BLOCK)[None, :]
    k_pos = (blk[:, None] - 1) * BLOCK + jnp.arange(3 * BLOCK)[None, :]
    valid = ((jnp.abs(q_pos[:, :, None] - k_pos[:, None, :]) <= WINDOW)
             & (k_pos >= 0)[:, None, :] & (k_pos < s_len)[:, None, :])
    s_win = jnp.einsum('bnqhgd,bnkhd->bnhgqk', qb, kb).astype(jnp.float32) * scale
    s_win = jnp.where(valid[None, :, None, None], s_win, NEG_INF)
    s_ctx = jnp.einsum('bnqhgd,bchd->bnhgqc', qb, k_ctx).astype(jnp.float32) * scale
    s_sink = jnp.broadcast_to(sink[None, None, :, :, None, None].astype(jnp.float32), s_ctx.shape[:-1] + (1,))
    p = jax.nn.softmax(jnp.concatenate([s_sink, s_ctx, s_win], axis=-1), axis=-1).astype(v.dtype)
    n_ctx = k_ctx.shape[1]
    o = (jnp.einsum('bnhgqc,bchd->bnqhgd', p[..., 1:1 + n_ctx], v_ctx)
         + jnp.einsum('bnhgqk,bnkhd->bnqhgd', p[..., 1 + n_ctx:], vb))
    return o.reshape(b, s_len, -1)


def gqa_heads(q, k, v, n_heads, n_kv, rope=None, q_gain=None, k_gain=None):
    b, n, _ = q.shape
    q = q.reshape(b, n, n_heads, HEAD_DIM)
    k = k.reshape(b, n, n_kv, HEAD_DIM)
    v = v.reshape(b, n, n_kv, HEAD_DIM)
    if q_gain is not None:
        q = rms_norm(q, q_gain)
        k = rms_norm(k, k_gain)
    if rope is not None:
        q = apply_rope(q, rope[0], rope[1])
        k = apply_rope(k, rope[0], rope[1])
    return q.reshape(b, n, n_kv, n_heads // n_kv, HEAD_DIM), k, v


def window_mixer(pc, pl_, sink, rope, ctx_out):
    scale = HEAD_DIM ** -0.5
    qc, kc, vc = gqa_heads(pc[0], pc[1], pc[2], SW_HEADS, SW_KV_HEADS)
    ql, kl, vl = gqa_heads(pl_[0], pl_[1], pl_[2], SW_HEADS, SW_KV_HEADS, rope)
    sink = sink.reshape(SW_KV_HEADS, SW_HEADS // SW_KV_HEADS)
    y_lat = banded_window_attention(ql, kl, vl, kc, vc, sink, scale)
    y_ctx = context_attention(qc, kc, vc, scale, sink) if ctx_out else None
    return y_ctx, y_lat


def grid_attention_mixer(pc, pl_, q_gain, k_gain, rope, ctx_out):
    scale = HEAD_DIM ** -0.5
    qc, kc, vc = gqa_heads(pc[0], pc[1], pc[2], GA_HEADS, GA_KV_HEADS, None, q_gain, k_gain)
    ql, kl, vl = gqa_heads(pl_[0], pl_[1], pl_[2], GA_HEADS, GA_KV_HEADS, rope, q_gain, k_gain)
    y_lat = dense_block_attention(ql, jnp.concatenate([kc, kl], axis=1), jnp.concatenate([vc, vl], axis=1), scale)
    y_ctx = context_attention(qc, kc, vc, scale) if ctx_out else None
    return y_ctx, y_lat


def mla_mixer(pc, pl_, q_gain, w_uq, kv_gain, w_ukv, rope, ctx_out):
    scale = (MLA_NOPE + MLA_ROPE) ** -0.5

    def project(cq, ckv, k_rope, rope_tab):
        b, n, _ = cq.shape
        q = (rms_norm(cq, q_gain) @ w_uq).reshape(b, n, MLA_HEADS, MLA_NOPE + MLA_ROPE)
        kv = (rms_norm(ckv, kv_gain) @ w_ukv).reshape(b, n, MLA_HEADS, MLA_NOPE + MLA_V)
        q_nope, q_pe = q[..., :MLA_NOPE], q[..., MLA_NOPE:]
        k_nope, v = kv[..., :MLA_NOPE], kv[..., MLA_NOPE:]
        k_pe = k_rope[:, :, None, :]
        if rope_tab is not None:
            q_pe = apply_rope(q_pe, rope_tab[0], rope_tab[1])
            k_pe = apply_rope(k_pe, rope_tab[0], rope_tab[1])
        q = jnp.concatenate([q_nope, q_pe], axis=-1)
        k = jnp.concatenate([k_nope, jnp.broadcast_to(k_pe, (b, n, MLA_HEADS, MLA_ROPE))], axis=-1)
        return q[:, :, :, None, :], k, v

    qc, kc, vc = project(pc[0], pc[1], pc[2], None)
    ql, kl, vl = project(pl_[0], pl_[1], pl_[2], rope)
    y_lat = dense_block_attention(ql, jnp.concatenate([kc, kl], axis=1), jnp.concatenate([vc, vl], axis=1), scale)
    y_ctx = context_attention(qc, kc, vc, scale) if ctx_out else None
    return y_ctx, y_lat


def centred_token_shift(f, mu):
    prev = jnp.pad(f[:, :-1], ((0, 0), (1, 0), (0, 0)))
    nxt = jnp.pad(f[:, 1:], ((0, 0), (0, 1), (0, 0)))
    return f + mu[0] * (prev - f) + mu[1] * (nxt - f)


def rwkv7_mixer(f_ctx, f_lat, vf_ctx, vf_lat, rw, ctx_out):
    def heads(t):
        return t.reshape(t.shape[:-1] + (RW_HEADS, HEAD_DIM))

    r_k = rw['r_k'].reshape(RW_HEADS, HEAD_DIM)

    def features(f, vf):
        f = centred_token_shift(f, rw['mu'])
        r, k, v, z = jnp.split(f, 4, axis=-1)
        if vf is None:
            vf = v
        else:
            v = v + (vf - v) * jax.nn.sigmoid(rw['v0'] + (z @ rw['v1']) @ rw['v2'])
        kk = heads(k * rw['k_k']).astype(jnp.float32)
        kk = (kk * lax.rsqrt(jnp.sum(kk * kk, axis=-1, keepdims=True) + 1e-12)).astype(f.dtype)
        g = jax.nn.sigmoid(z @ rw['g1']) @ rw['g2']
        per_dir = []
        for d in range(2):
            w = -jax.nn.softplus(-(rw['w0'][d] + jnp.tanh(z @ rw['w1'][d]) @ rw['w2'][d])) - 0.5
            log_decay = -jnp.exp(w.astype(jnp.float32))
            a = jax.nn.sigmoid(rw['a0'][d] + (z @ rw['a1'][d]) @ rw['a2'][d])
            k_d = k * (1 + (a - 1) * rw['k_a'])
            per_dir.append((heads(log_decay), heads(a), heads(k_d)))
        return heads(r), heads(v), kk, g, per_dir, vf

    r_c, v_c, kk_c, g_c, dir_c, vf_ctx = features(f_ctx, vf_ctx)
    r_l, v_l, kk_l, g_l, dir_l, vf_lat = features(f_lat, vf_lat)
    n_c, n_l = f_ctx.shape[1], f_lat.shape[1]
    n_b = f_lat.shape[0]

    def sequence(t_ctx, t_lat, rev):
        parts = [t_lat, t_ctx] if rev else [t_ctx, t_lat]
        return jnp.concatenate(parts, axis=1).reshape(n_b, n_c + n_l, RW_WIDTH)

    seqs = []
    for d in range(2):
        lw_c, a_c, kd_c = dir_c[d]
        lw_l, a_l, kd_l = dir_l[d]
        seqs.append(tuple(sequence(tc, tl, d == 1) for tc, tl in (
            (r_c, r_l), (lw_c, lw_l), (kd_c, kd_l), (v_c, v_l),
            (-kk_c, -kk_l), (kk_c * a_c, kk_l * a_l))))
    y_fwd, y_bwd = rwkv_scan_chunked(*seqs)
    y_fwd = y_fwd.reshape(n_b, n_c + n_l, RW_HEADS, HEAD_DIM)
    y_bwd = y_bwd.reshape(n_b, n_c + n_l, RW_HEADS, HEAD_DIM)
    y_parts = ((y_fwd[:, :n_c], y_fwd[:, n_c:]), (y_bwd[:, n_l:], y_bwd[:, :n_l]))
    y_c, y_l, bonus_c, bonus_l = 0.0, 0.0, 0.0, 0.0
    for d in range(2):
        y_dc, y_dl = y_parts[d]
        _, _, k_d = dir_c[d]
        if ctx_out:
            y_c = y_c + y_dc
            bonus_c = bonus_c + jnp.sum(r_c * k_d * r_k, axis=-1, keepdims=True) * v_c
        _, _, k_d = dir_l[d]
        y_l = y_l + y_dl
        bonus_l = bonus_l + jnp.sum(r_l * k_d * r_k, axis=-1, keepdims=True) * v_l

    ln_w = rw['ln_w'].reshape(RW_HEADS, HEAD_DIM).astype(jnp.float32)
    ln_b = rw['ln_b'].reshape(RW_HEADS, HEAD_DIM).astype(jnp.float32)

    def finish(y, bonus, g):
        mu = jnp.mean(y, axis=-1, keepdims=True)
        var = jnp.mean(jnp.square(y - mu), axis=-1, keepdims=True)
        yn = (y - mu) * lax.rsqrt(var + RW_GN_EPS) * ln_w + ln_b
        o = yn.astype(g.dtype) + bonus
        return o.reshape(g.shape) * g

    y_lat = finish(y_l, bonus_l, g_l)
    y_ctx = finish(y_c, bonus_c, g_c) if ctx_out else None
    return y_ctx, y_lat, vf_ctx, vf_lat


PROJ_ROWS = 512
MERGE_ROWS = 256
MIXER_COLS = sum(IN_WIDTHS[:-1])
PROJ_GROUPS = ((0, 4 * RW_WIDTH), (4 * RW_WIDTH, 512), (4 * RW_WIDTH + 512, 416),
               (4 * RW_WIDTH + 928, 512))
VMEM_LIMIT_BIG = 56 * 1024 * 1024


def _rms_mod(x, gain, shift, scale):
    y = x * lax.rsqrt(jnp.mean(x * x, axis=-1, keepdims=True) + NORM_EPS) * gain
    return y * (1 + scale) + shift


def _in_proj_kernel(idx_ref, x_ref, mod_ref, gain_ref, *refs):
    del idx_ref
    d = x_ref.shape[-1]
    n_out = len(refs) // 2
    mod = mod_ref[0]
    h = _rms_mod(x_ref[...], gain_ref[...], mod[:, 0:d], mod[:, d:2 * d]).astype(_BF)
    for w_ref, o_ref in zip(refs[:n_out], refs[n_out:]):
        o_ref[...] = jnp.dot(h, w_ref[...], preferred_element_type=_F32)


def fused_in_proj(x_tok, tile_mod, mod_tab, gain, w_groups):
    t, d = x_tok.shape
    full = lambda i, idx: (0, 0)
    grid_spec = pltpu.PrefetchScalarGridSpec(
        num_scalar_prefetch=1,
        grid=(t // PROJ_ROWS,),
        in_specs=[pl.BlockSpec((PROJ_ROWS, d), lambda i, idx: (i, 0)),
                  pl.BlockSpec((1, 1, mod_tab.shape[-1]), lambda i, idx: (idx[i], 0, 0)),
                  pl.BlockSpec((1, d), full)]
        + [pl.BlockSpec(w.shape, full) for w in w_groups],
        out_specs=[pl.BlockSpec((PROJ_ROWS, w.shape[1]), lambda i, idx: (i, 0)) for w in w_groups])
    return pl.pallas_call(
        _in_proj_kernel,
        grid_spec=grid_spec,
        out_shape=[jax.ShapeDtypeStruct((t, w.shape[1]), _F32) for w in w_groups],
        compiler_params=pltpu.CompilerParams(
            dimension_semantics=("parallel",), vmem_limit_bytes=VMEM_LIMIT_BIG),
        name="in_proj",
    )(tile_mod, x_tok, mod_tab, gain[None], *w_groups)


def _merge_kernel(idx_ref, x_ref, mod_ref, gain1_ref, gain2_ref, ya_ref, yb_ref, yc_ref, yd_ref,
                  wg_ref, wb_ref, wo_ref, xo_ref, f_ref):
    del idx_ref
    d = x_ref.shape[-1]
    x = x_ref[...]
    mod = mod_ref[0]
    sh1, sc1, g1, sh2, sc2 = (mod[:, j * d:(j + 1) * d] for j in range(5))
    h = _rms_mod(x, gain1_ref[...], sh1, sc1).astype(_BF)
    m = None
    for i, y_ref in enumerate((ya_ref, yb_ref, yc_ref, yd_ref)):
        gate = jax.nn.sigmoid(jnp.dot(h, wg_ref[:, i * d:(i + 1) * d], preferred_element_type=_F32))
        term = gate * jnp.dot(y_ref[...].astype(_BF), wb_ref[i], preferred_element_type=_F32)
        m = term if m is None else m + term
    x_new = x + g1 * jnp.dot(m.astype(_BF), wo_ref[...], preferred_element_type=_F32)
    xo_ref[...] = x_new
    f_ref[...] = _rms_mod(x_new, gain2_ref[...], sh2, sc2).astype(_BF)


def fused_merge(x_tok, tile_mod, mod_tab, gain1, gain2, ys, w_gate, w_branch, w_out):
    t, d = x_tok.shape
    full2 = lambda i, idx: (0, 0)
    rows = lambda i, idx: (i, 0)
    grid_spec = pltpu.PrefetchScalarGridSpec(
        num_scalar_prefetch=1,
        grid=(t // MERGE_ROWS,),
        in_specs=[pl.BlockSpec((MERGE_ROWS, d), rows),
                  pl.BlockSpec((1, 1, mod_tab.shape[-1]), lambda i, idx: (idx[i], 0, 0)),
                  pl.BlockSpec((1, d), full2), pl.BlockSpec((1, d), full2)]
        + [pl.BlockSpec((MERGE_ROWS, BRANCH_WIDTH), rows)] * N_BRANCH
        + [pl.BlockSpec(w_gate.shape, full2),
           pl.BlockSpec(w_branch.shape, lambda i, idx: (0, 0, 0)),
           pl.BlockSpec(w_out.shape, full2)],
        out_specs=[pl.BlockSpec((MERGE_ROWS, d), rows), pl.BlockSpec((MERGE_ROWS, d), rows)])
    return pl.pallas_call(
        _merge_kernel,
        grid_spec=grid_spec,
        out_shape=[jax.ShapeDtypeStruct((t, d), _F32), jax.ShapeDtypeStruct((t, d), _BF)],
        compiler_params=pltpu.CompilerParams(
            dimension_semantics=("parallel",), vmem_limit_bytes=VMEM_LIMIT_BIG),
        name="merge_proj",
    )(tile_mod, x_tok, mod_tab, gain1[None], gain2[None], *ys, w_gate, w_branch, w_out)


FFN_ROWS = 512
FFN_COLS = 512


def _swiglu_accumulate(x_ref, w1_ref, w3_ref, w2_ref, acc_ref):
    @pl.when(pl.program_id(1) == 0)
    def _():
        acc_ref[...] = jnp.zeros_like(acc_ref)

    x = x_ref[...]
    h1 = jnp.dot(x, w1_ref[0], preferred_element_type=_F32)
    h3 = jnp.dot(x, w3_ref[0], preferred_element_type=_F32)
    h = (h1 * jax.nn.sigmoid(h1) * h3).astype(_BF)
    acc_ref[...] += jnp.dot(h, w2_ref[0], preferred_element_type=_F32)


def _swiglu_kernel(be_ref, x_ref, w1_ref, w3_ref, w2_ref, g_ref, o_ref, acc_ref):
    del be_ref
    _swiglu_accumulate(x_ref, w1_ref, w3_ref, w2_ref, acc_ref)

    @pl.when(pl.program_id(1) == pl.num_programs(1) - 1)
    def _():
        o_ref[...] = acc_ref[...] * g_ref[...]


def _ffn_residual_kernel(idx_ref, f_ref, x_ref, mod_ref, w1_ref, w3_ref, w2_ref, o_ref, acc_ref):
    del idx_ref
    _swiglu_accumulate(f_ref, w1_ref, w3_ref, w2_ref, acc_ref)

    @pl.when(pl.program_id(1) == pl.num_programs(1) - 1)
    def _():
        d = x_ref.shape[-1]
        o_ref[...] = x_ref[...] + mod_ref[0][:, 5 * d:6 * d] * acc_ref[...]


def ffn_residual(f_tok, x_tok, tile_mod, mod_tab, w1, w3, w2):
    t, d = x_tok.shape
    n_f = w1.shape[-1]
    rows = lambda i, f, idx: (i, 0)
    grid_spec = pltpu.PrefetchScalarGridSpec(
        num_scalar_prefetch=1,
        grid=(t // PROJ_ROWS, n_f // FFN_COLS),
        in_specs=[pl.BlockSpec((PROJ_ROWS, d), rows), pl.BlockSpec((PROJ_ROWS, d), rows),
                  pl.BlockSpec((1, 1, mod_tab.shape[-1]), lambda i, f, idx: (idx[i], 0, 0)),
                  pl.BlockSpec((1, d, FFN_COLS), lambda i, f, idx: (0, 0, f)),
                  pl.BlockSpec((1, d, FFN_COLS), lambda i, f, idx: (0, 0, f)),
                  pl.BlockSpec((1, FFN_COLS, d), lambda i, f, idx: (0, f, 0))],
        out_specs=pl.BlockSpec((PROJ_ROWS, d), rows),
        scratch_shapes=[pltpu.VMEM((PROJ_ROWS, d), _F32)])
    return pl.pallas_call(
        _ffn_residual_kernel,
        grid_spec=grid_spec,
        out_shape=jax.ShapeDtypeStruct((t, d), _F32),
        compiler_params=pltpu.CompilerParams(
            dimension_semantics=("parallel", "arbitrary"),
            vmem_limit_bytes=48 * 1024 * 1024),
        name="ffn_residual",
    )(tile_mod, f_tok, x_tok, mod_tab, w1.astype(_BF)[None], w3.astype(_BF)[None], w2.astype(_BF)[None])


def grouped_swiglu(x, block_e, gate, w1, w3, w2):
    n_rows, d = x.shape
    n_f = w1.shape[-1]
    assert n_rows % FFN_ROWS == 0 and n_f % FFN_COLS == 0
    grid_spec = pltpu.PrefetchScalarGridSpec(
        num_scalar_prefetch=1,
        grid=(n_rows // FFN_ROWS, n_f // FFN_COLS),
        in_specs=[pl.BlockSpec((FFN_ROWS, d), lambda i, f, be: (i, 0)),
                  pl.BlockSpec((1, d, FFN_COLS), lambda i, f, be: (be[i], 0, f)),
                  pl.BlockSpec((1, d, FFN_COLS), lambda i, f, be: (be[i], 0, f)),
                  pl.BlockSpec((1, FFN_COLS, d), lambda i, f, be: (be[i], f, 0)),
                  pl.BlockSpec((FFN_ROWS, 1), lambda i, f, be: (i, 0))],
        out_specs=pl.BlockSpec((FFN_ROWS, d), lambda i, f, be: (i, 0)),
        scratch_shapes=[pltpu.VMEM((FFN_ROWS, d), _F32)])
    return pl.pallas_call(
        _swiglu_kernel,
        grid_spec=grid_spec,
        out_shape=jax.ShapeDtypeStruct((n_rows, d), _F32),
        compiler_params=pltpu.CompilerParams(
            dimension_semantics=("parallel", "arbitrary"),
            vmem_limit_bytes=48 * 1024 * 1024),
        name="grouped_swiglu",
    )(block_e, x.astype(_BF), w1.astype(_BF), w3.astype(_BF), w2.astype(_BF), gate)


def moe_swiglu(h, w_router, w1, w3, w2):
    n_tok, d = h.shape
    logits = jnp.dot(h, w_router.astype(_BF), preferred_element_type=jnp.float32)
    top_logit, top_idx = lax.top_k(logits, TOP_K)
    gates = jax.nn.softmax(top_logit, axis=-1)
    n_assign = n_tok * TOP_K
    flat_e = top_idx.reshape(-1)
    order = jnp.argsort(flat_e)
    sorted_e = flat_e[order]
    sorted_tok = (order // TOP_K).astype(jnp.int32)
    sorted_gate = gates.reshape(-1)[order]
    counts = jnp.bincount(flat_e, length=N_EXPERTS)
    padded = (counts + FFN_ROWS - 1) // FFN_ROWS * FFN_ROWS
    start = jnp.cumsum(counts) - counts
    pend = jnp.cumsum(padded)
    pstart = pend - padded
    dest = pstart[sorted_e] + jnp.arange(n_assign) - start[sorted_e]
    n_blocks = -(-n_assign // FFN_ROWS) + N_EXPERTS
    n_rows = n_blocks * FFN_ROWS
    tok = jnp.zeros((n_rows,), jnp.int32).at[dest].set(sorted_tok)
    gate = jnp.zeros((n_rows,), _F32).at[dest].set(sorted_gate)
    block_e = jnp.minimum(jnp.searchsorted(pend, jnp.arange(n_blocks) * FFN_ROWS, side='right'),
                          N_EXPERTS - 1).astype(jnp.int32)
    yb = grouped_swiglu(h[tok], block_e, gate[:, None], w1, w3, w2)
    return jnp.zeros((n_tok, d), _F32).at[tok].add(yb)


def kernel(x, c, ctx, c_ctx, w_mod, b_mod, norm1, norm2, w_in, w_branch, w_out,
           rw_mu, rw_w0, rw_w1, rw_w2, rw_a0, rw_a1, rw_a2, rw_v0, rw_v1, rw_v2,
           rw_g1, rw_g2, rw_k_k, rw_k_a, rw_r_k, rw_ln_w, rw_ln_b, sw_sink,
           mla_q_norm, mla_w_uq, mla_kv_norm, mla_w_ukv, ga_q_norm, ga_k_norm,
           ffn_w1, ffn_w3, ffn_w2, moe_router, moe_w1, moe_w3, moe_w2, final_norm):
    b, s_len, d = x.shape
    n_ctx_tok = ctx.shape[1]
    n_c = b * n_ctx_tok
    assert n_c % PROJ_ROWS == 0 and s_len % PROJ_ROWS == 0
    rope64 = axial_rope_tables(s_len, HEAD_DIM)
    rope32 = axial_rope_tables(s_len, MLA_ROPE)
    silu_c = jax.nn.silu(c)
    silu_cc = jax.nn.silu(c_ctx)
    x_tok = jnp.concatenate([ctx.reshape(n_c, d), x.reshape(b * s_len, d)], axis=0)
    seg_rows = [n_c] + [s_len] * b

    def tile_table(rows):
        return jnp.concatenate([jnp.full((n // rows,), i, jnp.int32) for i, n in enumerate(seg_rows)])

    tile_mod_proj, tile_mod_merge = tile_table(PROJ_ROWS), tile_table(MERGE_ROWS)

    def split_tokens(t):
        return t[:n_c].reshape(b, n_ctx_tok, -1), t[n_c:].reshape(b, s_len, -1)

    vf_lat, vf_ctx = None, None
    for l in range(DEPTH):
        ctx_out = l < DEPTH - 1
        mod_l = silu_c @ w_mod[l] + b_mod[l]
        mod_c = silu_cc @ w_mod[l] + b_mod[l]
        mod_tab = jnp.concatenate([mod_c[None], mod_l], axis=0)[:, None, :]

        w_groups = []
        for off, width in PROJ_GROUPS:
            w = w_in[l][:, off:off + width].astype(_BF)
            w_groups.append(jnp.pad(w, ((0, 0), (0, _round_up(width, LANES) - width))))
        p_rw, p_sw, p_mla, p_ga = fused_in_proj(x_tok, tile_mod_proj, mod_tab, norm1[l], w_groups)
        pc, pl_ = [], []
        for p, widths in ((p_rw, IN_WIDTHS[0:1]), (p_sw, IN_WIDTHS[1:4]),
                          (p_mla, IN_WIDTHS[4:7]), (p_ga, IN_WIDTHS[7:10])):
            p_ctx, p_lat = split_tokens(p)
            pc += split_cols(p_ctx, widths)
            pl_ += split_cols(p_lat, widths)
        rw = dict(mu=rw_mu[l], w0=rw_w0[l], w1=rw_w1[l], w2=rw_w2[l],
                  a0=rw_a0[l], a1=rw_a1[l], a2=rw_a2[l],
                  v0=rw_v0[l - 1] if l > 0 else None,
                  v1=rw_v1[l - 1] if l > 0 else None,
                  v2=rw_v2[l - 1] if l > 0 else None,
                  g1=rw_g1[l], g2=rw_g2[l], k_k=rw_k_k[l], k_a=rw_k_a[l], r_k=rw_r_k[l],
                  ln_w=rw_ln_w[l], ln_b=rw_ln_b[l])
        ya_c, ya_l, vf_ctx, vf_lat = rwkv7_mixer(pc[0], pl_[0], vf_ctx, vf_lat, rw, ctx_out)
        yb_c, yb_l = window_mixer(pc[1:4], pl_[1:4], sw_sink[l], rope64, ctx_out)
        yc_c, yc_l = mla_mixer(pc[4:7], pl_[4:7], mla_q_norm[l], mla_w_uq[l], mla_kv_norm[l],
                               mla_w_ukv[l], rope32, ctx_out)
        yd_c, yd_l = grid_attention_mixer(pc[7:10], pl_[7:10], ga_q_norm[l], ga_k_norm[l], rope64, ctx_out)

        def tokens_of(y_ctx, y_lat):
            y_ctx = jnp.zeros((n_c, BRANCH_WIDTH), _F32) if y_ctx is None else y_ctx.reshape(n_c, BRANCH_WIDTH)
            return jnp.concatenate([y_ctx, y_lat.reshape(b * s_len, BRANCH_WIDTH)], axis=0)

        ys = [tokens_of(yc_, yl_) for yc_, yl_ in ((ya_c, ya_l), (yb_c, yb_l), (yc_c, yc_l), (yd_c, yd_l))]
        x_tok, f_tok = fused_merge(x_tok, tile_mod_merge, mod_tab, norm1[l], norm2[l], ys,
                                   w_in[l][:, MIXER_COLS:].astype(_BF), w_branch[l].astype(_BF),
                                   w_out[l].astype(_BF))
        if l % 2 == 0:
            x_tok = ffn_residual(f_tok, x_tok, tile_mod_proj, mod_tab,
                                 ffn_w1[l // 2], ffn_w3[l // 2], ffn_w2[l // 2])
        else:
            first = 0 if ctx_out else n_c
            ffn_out = moe_swiglu(f_tok[first:], moe_router[l // 2], moe_w1[l // 2], moe_w3[l // 2],
                                 moe_w2[l // 2])
            pieces, row = [], 0
            for i, n in enumerate(seg_rows):
                if row >= first:
                    g2 = mod_tab[i, :, 5 * d:]
                    pieces.append(x_tok[row:row + n] + g2 * ffn_out[row - first:row - first + n])
                else:
                    pieces.append(x_tok[row:row + n])
                row += n
            x_tok = jnp.concatenate(pieces, axis=0)
    return rms_norm(x_tok[n_c:].reshape(b, s_len, d), final_norm)
```

```python
import functools

import jax
import jax.numpy as jnp
from jax import lax
from jax.experimental import pallas as pl
from jax.experimental.pallas import tpu as pltpu

D_MODEL = 1024
DEPTH = 4
GRID_W = 64
HEAD_DIM = 64
BLOCK = 128
WINDOW = 128
ROPE_BASE = 10000.0
NORM_EPS = 1e-6
NEG_INF = -1e30
N_BRANCH = 4
BRANCH_WIDTH = 256
RW_HEADS = 4
RW_WIDTH = RW_HEADS * HEAD_DIM
RW_GN_EPS = 64e-5
SW_HEADS = 4
SW_KV_HEADS = 2
MLA_HEADS = 4
MLA_NOPE = 64
MLA_ROPE = 32
MLA_V = 64
MLA_Q_RANK = 256
MLA_KV_RANK = 128
GA_HEADS = 4
GA_KV_HEADS = 2
N_EXPERTS = 8
TOP_K = 2
MOE_BLOCK = 128
IN_WIDTHS = (4 * RW_WIDTH,
             SW_HEADS * HEAD_DIM, SW_KV_HEADS * HEAD_DIM, SW_KV_HEADS * HEAD_DIM,
             MLA_Q_RANK, MLA_KV_RANK, MLA_ROPE,
             GA_HEADS * HEAD_DIM, GA_KV_HEADS * HEAD_DIM, GA_KV_HEADS * HEAD_DIM,
             N_BRANCH * D_MODEL)


def _mm_kernel(a_ref, b_ref, o_ref):
    o_ref[...] = jnp.dot(a_ref[...], b_ref[...], preferred_element_type=jnp.float32)


def _round_up(x, m):
    return (x + m - 1) // m * m


def pmm(a, b, tm=512, tn=512):
    m, k = a.shape
    _, n = b.shape
    tm = min(tm, _round_up(m, 8))
    tn = min(tn, _round_up(n, 128))
    mp, np_ = _round_up(m, tm), _round_up(n, tn)
    a = a.astype(jnp.bfloat16)
    b = b.astype(jnp.bfloat16)
    if mp != m:
        a = jnp.pad(a, ((0, mp - m), (0, 0)))
    if np_ != n:
        b = jnp.pad(b, ((0, 0), (0, np_ - n)))
    out = pl.pallas_call(
        _mm_kernel,
        grid=(np_ // tn, mp // tm),
        in_specs=[pl.BlockSpec((tm, k), lambda j, i: (i, 0)),
                  pl.BlockSpec((k, tn), lambda j, i: (0, j))],
        out_specs=pl.BlockSpec((tm, tn), lambda j, i: (i, j)),
        out_shape=jax.ShapeDtypeStruct((mp, np_), jnp.float32),
        compiler_params=pltpu.CompilerParams(
            dimension_semantics=("arbitrary", "arbitrary"),
            vmem_limit_bytes=48 * 1024 * 1024),
        name="pmm",
    )(a, b)
    return out[:m, :n]


def pmm_nd(a, b, **kw):
    lead = a.shape[:-1]
    return pmm(a.reshape(-1, a.shape[-1]), b, **kw).reshape(lead + (b.shape[-1],))


RW_CHUNK = 64
RW_CHUNKS_PER_STEP = 4
RW_BLOCK = RW_CHUNK * RW_CHUNKS_PER_STEP
RW_INV_BASE = 4

_BF = jnp.bfloat16
_F32 = jnp.float32


def _bd_rows(x):
    lane_head = lax.broadcasted_iota(jnp.int32, x.shape, 1) // HEAD_DIM
    return jnp.concatenate([jnp.where(lane_head == h, x, 0.0) for h in range(RW_HEADS)], axis=0)


def _dot(a, b):
    return jnp.dot(a.astype(_BF), b.astype(_BF), preferred_element_type=_F32)


def _dot_nt(a, b):
    return lax.dot_general(a.astype(_BF), b.astype(_BF), (((1,), (1,)), ((), ())),
                           preferred_element_type=_F32)


def _dot_tn(a, b):
    return lax.dot_general(a.astype(_BF), b.astype(_BF), (((0,), (0,)), ((), ())),
                           preferred_element_type=_F32)


def _split3(x):
    h1 = x.astype(_BF)
    r1 = x - h1.astype(_F32)
    h2 = r1.astype(_BF)
    h3 = (r1 - h2.astype(_F32)).astype(_BF)
    return h1, h2, h3


def _each(fn, *lists):
    return [fn(*args) for args in zip(*lists)]


def _chunk_summaries(r, lw, k, v, a, b, reverse):
    c = RW_CHUNK
    row = lax.broadcasted_iota(jnp.int32, (c, c), 0)
    col = lax.broadcasted_iota(jnp.int32, (c, c), 1)
    tri = ((row <= col) if reverse else (row >= col)).astype(_BF)
    cum = _each(lambda x: sum(jnp.dot(tri, p, preferred_element_type=_F32) for p in _split3(x)), lw)
    lwlast = _each(lambda x: x[0:1, :] if reverse else x[c - 1:c, :], cum)
    rt = _each(lambda x, cu: x * jnp.exp(cu), r, cum)
    at = _each(lambda x, cu, l: x * jnp.exp(cu - l), a, cum, lw)
    w_inv = _each(lambda cu: jnp.exp(-cu), cum)
    w_tail = _each(lambda cu, ll: jnp.exp(ll - cu), cum, lwlast)
    wlast = _each(jnp.exp, lwlast)
    bt = _each(jnp.multiply, b, w_inv)
    kt = _each(jnp.multiply, k, w_inv)
    bh = _each(jnp.multiply, b, w_tail)
    kh = _each(jnp.multiply, k, w_tail)

    t_idx = lax.broadcasted_iota(jnp.int32, (c, RW_WIDTH), 0)
    s_idx = lax.broadcasted_iota(jnp.int32, (c, RW_WIDTH), 1) % HEAD_DIM
    strict = (s_idx > t_idx) if reverse else (s_idx < t_idx)
    incl = (s_idx >= t_idx) if reverse else (s_idx <= t_idx)

    l1 = _each(lambda x, y: jnp.concatenate([x, y], axis=0), at, rt)
    g_b = _each(lambda x, y: _dot_nt(x, _bd_rows(y)), l1, bt)
    g_k = _each(lambda x, y: _dot_nt(x, _bd_rows(y)), l1, kt)
    a_ab = _each(lambda g: jnp.where(strict, g[:c], 0.0), g_b)
    a_rb = _each(lambda g: jnp.where(incl, g[c:], 0.0), g_b)
    a_ak = _each(lambda g: jnp.where(strict, g[:c], 0.0), g_k)
    a_rk = _each(lambda g: jnp.where(incl, g[c:], 0.0), g_k)

    eye = jnp.where(s_idx == t_idx, 1.0, 0.0)
    base = RW_INV_BASE
    n_b = _each(lambda x: jnp.where(t_idx // base == s_idx // base, x, 0.0), a_ab)
    n_b2 = _each(lambda x: _dot(x, _bd_rows(x)), n_b)
    n_b3 = _each(lambda x, y: _dot(x, _bd_rows(y)), n_b, n_b2)
    tm = _each(lambda x, y, z: eye + x + y + z, n_b, n_b2, n_b3)
    blk = base
    while blk < c:
        off = (t_idx // (2 * blk) == s_idx // (2 * blk)) & (t_idx // blk != s_idx // blk)
        x = _each(lambda t, n: _dot(t, _bd_rows(jnp.where(off, n, 0.0))), tm, a_ab)
        tm = _each(lambda t, xx: t + _dot(xx, _bd_rows(t)), tm, x)
        blk *= 2

    av = _each(lambda x, y, vv: _dot(jnp.concatenate([x, y], axis=0), _bd_rows(vv)), a_ak, a_rk, v)
    ta = _each(lambda t, x: _dot(t, _bd_rows(x)), tm, at)
    u0 = _each(lambda t, x: _dot(t, _bd_rows(x[:c])), tm, av)
    r1 = _each(lambda x, ar, t: x + _dot(ar, _bd_rows(t)), rt, a_rb, ta)
    y0 = _each(lambda x, ar, u: x[c:] + _dot(ar, _bd_rows(u)), av, a_rb, u0)

    blk_r = lax.broadcasted_iota(jnp.int32, (RW_WIDTH, RW_WIDTH), 0) // HEAD_DIM
    blk_c = lax.broadcasted_iota(jnp.int32, (RW_WIDTH, RW_WIDTH), 1) // HEAD_DIM
    m_bd = _each(lambda t, x: jnp.where(blk_r == blk_c, _dot_tn(t, x), 0.0), ta, bh)
    z = _each(lambda u, vv, x, y: _dot_tn(jnp.concatenate([u, vv], axis=0),
                                          jnp.concatenate([x, y], axis=0)), u0, v, bh, kh)
    lane_head = lax.broadcasted_iota(jnp.int32, (HEAD_DIM, RW_WIDTH), 1) // HEAD_DIM

    def diag_blocks(zz):
        out = zz[:HEAD_DIM]
        for h in range(1, RW_HEADS):
            out = jnp.where(lane_head == h, zz[h * HEAD_DIM:(h + 1) * HEAD_DIM], out)
        return out

    sadd = _each(diag_blocks, z)
    return r1, y0, m_bd, sadd, wlast


def _rw_summary_kernel(r_ref, lw_ref, k_ref, v_ref, a_ref, b_ref,
                       r1_ref, y0_ref, m_ref, sadd_ref, wl_ref, *, reverse):
    c = RW_CHUNK

    def chunks(ref):
        return [ref[pl.ds(j * c, c), :] for j in range(RW_CHUNKS_PER_STEP)]

    r1, y0, m_bd, sadd, wlast = _chunk_summaries(
        chunks(r_ref), chunks(lw_ref), chunks(k_ref), chunks(v_ref), chunks(a_ref), chunks(b_ref),
        reverse)
    r1_ref[...] = jnp.concatenate(r1, axis=0)
    y0_ref[...] = jnp.concatenate(y0, axis=0)
    m_ref[...] = jnp.stack(m_bd, axis=0).astype(_BF)
    sadd_ref[...] = jnp.stack(sadd, axis=0)
    wl_ref[...] = jnp.stack(wlast, axis=0)


def rw_summaries(r_src, lw, k, v, a, b, reverse):
    t, w = lw.shape
    c, cps = RW_CHUNK, RW_CHUNKS_PER_STEP
    assert t % RW_BLOCK == 0 and w == RW_WIDTH
    rows = pl.BlockSpec((RW_BLOCK, w), lambda i: (i, 0))
    return pl.pallas_call(
        functools.partial(_rw_summary_kernel, reverse=reverse),
        grid=(t // RW_BLOCK,),
        in_specs=[rows] * 6,
        out_specs=[rows, rows,
                   pl.BlockSpec((cps, w, w), lambda i: (i, 0, 0)),
                   pl.BlockSpec((cps, HEAD_DIM, w), lambda i: (i, 0, 0)),
                   pl.BlockSpec((cps, 1, w), lambda i: (i, 0, 0))],
        out_shape=[jax.ShapeDtypeStruct((t, w), _F32),
                   jax.ShapeDtypeStruct((t, w), _F32),
                   jax.ShapeDtypeStruct((t // c, w, w), _BF),
                   jax.ShapeDtypeStruct((t // c, HEAD_DIM, w), _F32),
                   jax.ShapeDtypeStruct((t // c, 1, w), _F32)],
        compiler_params=pltpu.CompilerParams(
            dimension_semantics=("parallel",), vmem_limit_bytes=48 * 1024 * 1024),
        name="rw_chunk_summary",
    )(r_src, lw, k, v, a, b)


def _rw_state_kernel(*refs):
    s_ref = refs[-1]
    y_refs = refs[10:12]
    c, cps = RW_CHUNK, RW_CHUNKS_PER_STEP

    @pl.when(pl.program_id(1) == 0)
    def _():
        s_ref[...] = jnp.zeros_like(s_ref)

    s = [s_ref[0], s_ref[1]]
    ys = [[None] * cps, [None] * cps]
    for step in range(cps):
        js = (step, cps - 1 - step)
        r1, y0 = ([refs[5 * d + n][pl.ds(js[d] * c, c), :] for d in range(2)] for n in range(2))
        m_bd, sadd, wlast = ([refs[5 * d + n][js[d]] for d in range(2)] for n in range(2, 5))
        y = _each(lambda rr, ss, yy: _dot_nt(rr, _bd_rows(ss)) + yy, r1, s, y0)
        s = _each(lambda ss, ww, mm, aa: ss * ww + _dot(ss, mm) + aa, s, wlast, m_bd, sadd)
        for d in range(2):
            ys[d][js[d]] = y[d]
    for d in range(2):
        y_refs[d][...] = jnp.concatenate(ys[d], axis=0)
        s_ref[d] = s[d]


def rw_state_pass(sum_fwd, sum_bwd, n_batch, n_ctx, n_lat):
    t, w = sum_fwd[0].shape
    cps = RW_CHUNKS_PER_STEP
    assert n_ctx % RW_BLOCK == 0 and n_lat % RW_BLOCK == 0
    n_cb, n_lb = n_ctx // RW_BLOCK, n_lat // RW_BLOCK
    lat0 = n_batch * n_cb

    def fwd_blk(q, i):
        return jnp.where(i < n_cb, q * n_cb + i, lat0 + q * n_lb + (i - n_cb))

    def bwd_blk(q, i):
        return jnp.where(i < n_cb, q * n_cb + (n_cb - 1 - i), lat0 + q * n_lb + (n_lb - 1 - (i - n_cb)))

    def specs(blk):
        return [pl.BlockSpec((RW_BLOCK, w), lambda q, i: (blk(q, i), 0)),
                pl.BlockSpec((RW_BLOCK, w), lambda q, i: (blk(q, i), 0)),
                pl.BlockSpec((cps, w, w), lambda q, i: (blk(q, i), 0, 0)),
                pl.BlockSpec((cps, HEAD_DIM, w), lambda q, i: (blk(q, i), 0, 0)),
                pl.BlockSpec((cps, 1, w), lambda q, i: (blk(q, i), 0, 0))]

    return pl.pallas_call(
        _rw_state_kernel,
        grid=(n_batch, n_cb + n_lb),
        in_specs=specs(fwd_blk) + specs(bwd_blk),
        out_specs=[pl.BlockSpec((RW_BLOCK, w), lambda q, i: (fwd_blk(q, i), 0)),
                   pl.BlockSpec((RW_BLOCK, w), lambda q, i: (bwd_blk(q, i), 0))],
        out_shape=[jax.ShapeDtypeStruct((t, w), _F32)] * 2,
        scratch_shapes=[pltpu.VMEM((2, HEAD_DIM, w), _F32)],
        compiler_params=pltpu.CompilerParams(
            dimension_semantics=("arbitrary", "arbitrary"),
            vmem_limit_bytes=48 * 1024 * 1024),
        name="rw_state_pass",
    )(*sum_fwd, *sum_bwd)


ATTN_TQ = 256
ATTN_TK = 768


LANES = 128
LOG2_E = 1.4426950408889634


def _dense_attn_kernel(q_ref, k_ref, v_ref, o_ref, *, tk):
    g, tq, dq = q_ref.shape[2:]
    dv = o_ref.shape[-1]
    n_keys = k_ref.shape[2]
    q = q_ref[0, 0].reshape(g * tq, dq)

    def scores(i):
        kc = k_ref[0, 0, pl.ds(i * tk, tk), :]
        return lax.dot_general(q, kc, (((1,), (1,)), ((), ())), preferred_element_type=_F32)

    def update(i, m, acc, s):
        vc = v_ref[0, 0, pl.ds(i * tk, tk), :]
        m_new = jnp.maximum(m, jnp.max(s, axis=-1, keepdims=True))
        p = jnp.exp2(s - m_new).astype(_BF)
        return m_new, jnp.exp2(m - m_new) * acc + jnp.dot(p, vc, preferred_element_type=_F32)

    n_chunks = n_keys // tk
    m = jnp.full((g * tq, 1), -jnp.inf, _F32)
    acc = jnp.zeros((g * tq, v_ref.shape[-1]), _F32)
    s = scores(0)
    for i in range(n_chunks):
        s_next = scores(i + 1) if i + 1 < n_chunks else None
        m, acc = update(i, m, acc, s)
        s = s_next
    o_ref[0, 0] = (acc[:, :dv] / acc[:, dv:dv + 1]).reshape(g, tq, dv)


def dense_block_attention(q, k, v, scale):
    b, s_len, hk, g, dq = q.shape
    n_keys, dv = k.shape[1], v.shape[-1]
    tq = min(ATTN_TQ, s_len)
    tk = ATTN_TK if n_keys % ATTN_TK == 0 else n_keys
    assert s_len % tq == 0
    qt = jnp.transpose(q * (scale * LOG2_E), (0, 2, 3, 1, 4)).astype(_BF)
    kt = jnp.transpose(k, (0, 2, 1, 3)).astype(_BF)
    vt = jnp.transpose(v, (0, 2, 1, 3))
    vt = jnp.concatenate([vt, jnp.ones_like(vt[..., :1]),
                          jnp.zeros(vt.shape[:-1] + (LANES - dv - 1,), vt.dtype)], axis=-1).astype(_BF)
    o = pl.pallas_call(
        functools.partial(_dense_attn_kernel, tk=tk),
        grid=(b, hk, s_len // tq),
        in_specs=[pl.BlockSpec((1, 1, g, tq, dq), lambda bi, h, i: (bi, h, 0, i, 0)),
                  pl.BlockSpec((1, 1, n_keys, dq), lambda bi, h, i: (bi, h, 0, 0)),
                  pl.BlockSpec((1, 1, n_keys, LANES), lambda bi, h, i: (bi, h, 0, 0))],
        out_specs=pl.BlockSpec((1, 1, g, tq, dv), lambda bi, h, i: (bi, h, 0, i, 0)),
        out_shape=jax.ShapeDtypeStruct((b, hk, g, s_len, dv), _F32),
        compiler_params=pltpu.CompilerParams(
            dimension_semantics=("parallel", "parallel", "parallel"),
            vmem_limit_bytes=48 * 1024 * 1024),
        name="dense_attention",
    )(qt, kt, vt)
    return jnp.transpose(o, (0, 3, 1, 2, 4)).reshape(b, s_len, hk * g * dv)


def rms_norm(x, g):
    xf = x.astype(jnp.float32)
    y = xf * lax.rsqrt(jnp.mean(xf * xf, axis=-1, keepdims=True) + NORM_EPS)
    return (y * g.astype(jnp.float32)).astype(x.dtype)


def split_cols(p, widths):
    out, off = [], 0
    for w in widths:
        out.append(p[..., off:off + w])
        off += w
    return out


def axial_rope_tables(n_tokens, rot_dim):
    rows = n_tokens // GRID_W
    row = jnp.repeat(jnp.arange(rows, dtype=jnp.float32), GRID_W)
    col = jnp.tile(jnp.arange(GRID_W, dtype=jnp.float32), rows)
    n_freq = rot_dim // 4
    inv_freq = ROPE_BASE ** (-jnp.arange(n_freq, dtype=jnp.float32) / n_freq)
    ang = jnp.concatenate([row[:, None] * inv_freq, col[:, None] * inv_freq], axis=-1)
    return jnp.cos(ang), jnp.sin(ang)


def apply_rope(x, cos, sin):
    half = x.shape[-1] // 2
    c = cos[None, :, None, :].astype(x.dtype)
    s = sin[None, :, None, :].astype(x.dtype)
    x1, x2 = x[..., :half], x[..., half:]
    return jnp.concatenate([x1 * c - x2 * s, x1 * s + x2 * c], axis=-1)


def context_attention(q, k, v, scale, sink=None):
    b, n = q.shape[:2]
    s = jnp.einsum('bqhgd,bkhd->bhgqk', q, k).astype(jnp.float32) * scale
    if sink is not None:
        s_sink = jnp.broadcast_to(sink[None, :, :, None, None].astype(jnp.float32), s.shape[:-1] + (1,))
        p = jax.nn.softmax(jnp.concatenate([s_sink, s], axis=-1), axis=-1)[..., 1:]
    else:
        p = jax.nn.softmax(s, axis=-1)
    o = jnp.einsum('bhgqk,bkhd->bqhgd', p.astype(v.dtype), v)
    return o.reshape(b, n, -1)


def banded_window_attention(q, k, v, k_ctx, v_ctx, sink, scale):
    b, s_len, hk, g, d = q.shape
    nb = s_len // BLOCK
    qb = q.reshape(b, nb, BLOCK, hk, g, d)

    def neighbours(t):
        tb = jnp.pad(t.reshape(b, nb, BLOCK, hk, t.shape[-1]), ((0, 0), (1, 1), (0, 0), (0, 0), (0, 0)))
        return jnp.concatenate([tb[:, :-2], tb[:, 1:-1], tb[:, 2:]], axis=2)

    kb, vb = neighbours(k), neighbours(v)
    blk = jnp.arange(nb)
    q_pos = blk[:, None] * BLOCK + jnp.arange(BLOCK)[None, :]
    k_pos = (blk[:, None] - 1) * BLOCK + jnp.arange(3 * BLOCK)[None, :]
    valid = ((jnp.abs(q_pos[:, :, None] - k_pos[:, None, :]) <= WINDOW)
             & (k_pos >= 0)[:, None, :] & (k_pos < s_len)[:, None, :])
    s_win = jnp.einsum('bnqhgd,bnkhd->bnhgqk', qb, kb).astype(jnp.float32) * scale
    s_win = jnp.where(valid[None, :, None, None], s_win, NEG_INF)
    s_ctx = jnp.einsum('bnqhgd,bchd->bnhgqc', qb, k_ctx).astype(jnp.float32) * scale
    s_sink = jnp.broadcast_to(sink[None, None, :, :, None, None].astype(jnp.float32), s_ctx.shape[:-1] + (1,))
    p = jax.nn.softmax(jnp.concatenate([s_sink, s_ctx, s_win], axis=-1), axis=-1).astype(v.dtype)
    n_ctx = k_ctx.shape[1]
    o = (jnp.einsum('bnhgqc,bchd->bnqhgd', p[..., 1:1 + n_ctx], v_ctx)
         + jnp.einsum('bnhgqk,bnkhd->bnqhgd', p[..., 1 + n_ctx:], vb))
    return o.reshape(b, s_len, -1)


def gqa_heads(q, k, v, n_heads, n_kv, rope=None, q_gain=None, k_gain=None):
    b, n, _ = q.shape
    q = q.reshape(b, n, n_heads, HEAD_DIM)
    k = k.reshape(b, n, n_kv, HEAD_DIM)
    v = v.reshape(b, n, n_kv, HEAD_DIM)
    if q_gain is not None:
        q = rms_norm(q, q_gain)
        k = rms_norm(k, k_gain)
    if rope is not None:
        q = apply_rope(q, rope[0], rope[1])
        k = apply_rope(k, rope[0], rope[1])
    return q.reshape(b, n, n_kv, n_heads // n_kv, HEAD_DIM), k, v


def window_mixer(pc, pl_, sink, rope, ctx_out):
    scale = HEAD_DIM ** -0.5
    qc, kc, vc = gqa_heads(pc[0], pc[1], pc[2], SW_HEADS, SW_KV_HEADS)
    ql, kl, vl = gqa_heads(pl_[0], pl_[1], pl_[2], SW_HEADS, SW_KV_HEADS, rope)
    sink = sink.reshape(SW_KV_HEADS, SW_HEADS // SW_KV_HEADS)
    y_lat = banded_window_attention(ql, kl, vl, kc, vc, sink, scale)
    y_ctx = context_attention(qc, kc, vc, scale, sink) if ctx_out else None
    return y_ctx, y_lat


def grid_attention_mixer(pc, pl_, q_gain, k_gain, rope, ctx_out):
    scale = HEAD_DIM ** -0.5
    qc, kc, vc = gqa_heads(pc[0], pc[1], pc[2], GA_HEADS, GA_KV_HEADS, None, q_gain, k_gain)
    ql, kl, vl = gqa_heads(pl_[0], pl_[1], pl_[2], GA_HEADS, GA_KV_HEADS, rope, q_gain, k_gain)
    y_lat = dense_block_attention(ql, jnp.concatenate([kc, kl], axis=1), jnp.concatenate([vc, vl], axis=1), scale)
    y_ctx = context_attention(qc, kc, vc, scale) if ctx_out else None
    return y_ctx, y_lat


def mla_mixer(pc, pl_, q_gain, w_uq, kv_gain, w_ukv, rope, ctx_out):
    scale = (MLA_NOPE + MLA_ROPE) ** -0.5

    def project(cq, ckv, k_rope, rope_tab):
        b, n, _ = cq.shape
        q = (rms_norm(cq, q_gain) @ w_uq).reshape(b, n, MLA_HEADS, MLA_NOPE + MLA_ROPE)
        kv = (rms_norm(ckv, kv_gain) @ w_ukv).reshape(b, n, MLA_HEADS, MLA_NOPE + MLA_V)
        q_nope, q_pe = q[..., :MLA_NOPE], q[..., MLA_NOPE:]
        k_nope, v = kv[..., :MLA_NOPE], kv[..., MLA_NOPE:]
        k_pe = k_rope[:, :, None, :]
        if rope_tab is not None:
            q_pe = apply_rope(q_pe, rope_tab[0], rope_tab[1])
            k_pe = apply_rope(k_pe, rope_tab[0], rope_tab[1])
        q = jnp.concatenate([q_nope, q_pe], axis=-1)
        k = jnp.concatenate([k_nope, jnp.broadcast_to(k_pe, (b, n, MLA_HEADS, MLA_ROPE))], axis=-1)
        return q[:, :, :, None, :], k, v

    qc, kc, vc = project(pc[0], pc[1], pc[2], None)
    ql, kl, vl = project(pl_[0], pl_[1], pl_[2], rope)
    y_lat = dense_block_attention(ql, jnp.concatenate([kc, kl], axis=1), jnp.concatenate([vc, vl], axis=1), scale)
    y_ctx = context_attention(qc, kc, vc, scale) if ctx_out else None
    return y_ctx, y_lat


def centred_token_shift(f, mu):
    prev = jnp.pad(f[:, :-1], ((0, 0), (1, 0), (0, 0)))
    nxt = jnp.pad(f[:, 1:], ((0, 0), (0, 1), (0, 0)))
    return f + mu[0] * (prev - f) + mu[1] * (nxt - f)


def _head_sums(x):
    w = x.shape[-1]
    blk_r = lax.broadcasted_iota(jnp.int32, (w, w), 0) // HEAD_DIM
    blk_c = lax.broadcasted_iota(jnp.int32, (w, w), 1) // HEAD_DIM
    ones_bd = (blk_r == blk_c).astype(_BF)
    hi = x.astype(_BF)
    lo = (x - hi.astype(_F32)).astype(_BF)
    return (jnp.dot(hi, ones_bd, preferred_element_type=_F32)
            + jnp.dot(lo, ones_bd, preferred_element_type=_F32))


RW_FEATURE_ROWS = 512
_VEC_V0, _VEC_W0, _VEC_A0, _VEC_KK, _VEC_KA, _VEC_RK = 0, 1, 3, 5, 6, 7


def _rw_feature_kernel(f_ref, vf_ref, vec_ref, v1_ref, v2_ref, g1_ref, g2_ref, w1_ref, w2_ref,
                       a1_ref, a2_ref, v_ref, nkk_ref, g_ref, bonus_ref,
                       lw0_ref, kd0_ref, b0_ref, lw1_ref, kd1_ref, b1_ref, *, value_residual):
    w = RW_WIDTH
    f = f_ref[...]
    r, k, v, z = (f[:, j * w:(j + 1) * w] for j in range(4))
    vec = vec_ref[...]

    def row(i):
        return vec[i:i + 1, :]

    zb = z.astype(_BF)

    def lora(a, b, act=None):
        h = jnp.dot(zb, a.astype(_BF), preferred_element_type=_F32)
        h = h if act is None else act(h)
        return jnp.dot(h.astype(_BF), b.astype(_BF), preferred_element_type=_F32)

    if value_residual:
        v = v + (vf_ref[...] - v) * jax.nn.sigmoid(row(_VEC_V0) + lora(v1_ref[...], v2_ref[...]))
    kk = k * row(_VEC_KK)
    kk = kk * lax.rsqrt(_head_sums(kk * kk) + 1e-12)
    bonus = jnp.zeros_like(v)
    for d, (lw_ref, kd_ref, b_ref) in enumerate(((lw0_ref, kd0_ref, b0_ref), (lw1_ref, kd1_ref, b1_ref))):
        wl = -jax.nn.softplus(-(row(_VEC_W0 + d) + lora(w1_ref[d], w2_ref[d], jnp.tanh))) - 0.5
        a = jax.nn.sigmoid(row(_VEC_A0 + d) + lora(a1_ref[d], a2_ref[d]))
        k_d = k * (1 + (a - 1) * row(_VEC_KA))
        bonus = bonus + _head_sums(r * k_d * row(_VEC_RK)) * v
        lw_ref[...] = -jnp.exp(wl)
        kd_ref[...] = k_d
        b_ref[...] = kk * a
    v_ref[...] = v
    nkk_ref[...] = -kk
    g_ref[...] = lora(g1_ref[...], g2_ref[...], jax.nn.sigmoid)
    bonus_ref[...] = bonus


def rw_features(f_tok, vf_tok, rw, value_residual):
    t = f_tok.shape[0]
    w = RW_WIDTH
    zeros = jnp.zeros((w,), _F32)
    vec = jnp.stack([rw['v0'] if value_residual else zeros, rw['w0'][0], rw['w0'][1],
                     rw['a0'][0], rw['a0'][1], rw['k_k'], rw['k_a'], rw['r_k']])
    if value_residual:
        v1, v2 = rw['v1'], rw['v2']
    else:
        vf_tok = f_tok
        v1, v2 = jnp.zeros((w, 8), _F32), jnp.zeros((8, w), _F32)
    weights = [v1, v2, rw['g1'], rw['g2'], rw['w1'], rw['w2'], rw['a1'], rw['a2']]

    def full(a):
        return pl.BlockSpec(a.shape, lambda i, n=a.ndim: (0,) * n)

    rows = pl.BlockSpec((RW_FEATURE_ROWS, w), lambda i: (i, 0))
    return pl.pallas_call(
        functools.partial(_rw_feature_kernel, value_residual=value_residual),
        grid=(t // RW_FEATURE_ROWS,),
        in_specs=[pl.BlockSpec((RW_FEATURE_ROWS, 4 * w), lambda i: (i, 0)), rows, full(vec)]
        + [full(a) for a in weights],
        out_specs=[rows] * 10,
        out_shape=[jax.ShapeDtypeStruct((t, w), _F32)] * 10,
        compiler_params=pltpu.CompilerParams(
            dimension_semantics=("parallel",), vmem_limit_bytes=48 * 1024 * 1024),
        name="rw_features",
    )(f_tok, vf_tok, vec, *weights)


def rwkv_branch(p_rw, vf_tok, rw, n_batch, n_ctx, n_lat):
    n_c = n_batch * n_ctx
    value_residual = vf_tok is not None
    f_tok = jnp.concatenate(
        [centred_token_shift(p.reshape(n_batch, n, -1), rw['mu']).reshape(n_batch * n, -1)
         for p, n in ((p_rw[:n_c], n_ctx), (p_rw[n_c:], n_lat))], axis=0)
    v, nkk, g, bonus, lw0, kd0, b0, lw1, kd1, b1 = rw_features(f_tok, vf_tok, rw, value_residual)
    sum_fwd = rw_summaries(f_tok, lw0, kd0, v, nkk, b0, False)
    sum_bwd = rw_summaries(f_tok, lw1, kd1, v, nkk, b1, True)
    y_fwd, y_bwd = rw_state_pass(sum_fwd, sum_bwd, n_batch, n_ctx, n_lat)
    return y_fwd, y_bwd, bonus, g, v


PROJ_ROWS = 512
MERGE_ROWS = 256
MIXER_COLS = sum(IN_WIDTHS[:-1])
PROJ_GROUPS = ((0, 4 * RW_WIDTH), (4 * RW_WIDTH, 512), (4 * RW_WIDTH + 512, 416),
               (4 * RW_WIDTH + 928, 512))
VMEM_LIMIT_BIG = 56 * 1024 * 1024


def _rms_mod(x, gain, shift, scale):
    y = x * lax.rsqrt(jnp.mean(x * x, axis=-1, keepdims=True) + NORM_EPS) * gain
    return y * (1 + scale) + shift


def _in_proj_kernel(idx_ref, x_ref, mod_ref, gain_ref, *refs):
    del idx_ref
    d = x_ref.shape[-1]
    n_out = len(refs) // 2
    mod = mod_ref[0]
    h = _rms_mod(x_ref[...], gain_ref[...], mod[:, 0:d], mod[:, d:2 * d]).astype(_BF)
    for w_ref, o_ref in zip(refs[:n_out], refs[n_out:]):
        o_ref[...] = jnp.dot(h, w_ref[...], preferred_element_type=_F32)


def fused_in_proj(x_tok, tile_mod, mod_tab, gain, w_groups):
    t, d = x_tok.shape
    full = lambda i, idx: (0, 0)
    grid_spec = pltpu.PrefetchScalarGridSpec(
        num_scalar_prefetch=1,
        grid=(t // PROJ_ROWS,),
        in_specs=[pl.BlockSpec((PROJ_ROWS, d), lambda i, idx: (i, 0)),
                  pl.BlockSpec((1, 1, mod_tab.shape[-1]), lambda i, idx: (idx[i], 0, 0)),
                  pl.BlockSpec((1, d), full)]
        + [pl.BlockSpec(w.shape, full) for w in w_groups],
        out_specs=[pl.BlockSpec((PROJ_ROWS, w.shape[1]), lambda i, idx: (i, 0)) for w in w_groups])
    return pl.pallas_call(
        _in_proj_kernel,
        grid_spec=grid_spec,
        out_shape=[jax.ShapeDtypeStruct((t, w.shape[1]), _F32) for w in w_groups],
        compiler_params=pltpu.CompilerParams(
            dimension_semantics=("parallel",), vmem_limit_bytes=VMEM_LIMIT_BIG),
        name="in_proj",
    )(tile_mod, x_tok, mod_tab, gain[None], *w_groups)


def _merge_kernel(idx_ref, x_ref, mod_ref, gain1_ref, gain2_ref, rwf_ref, rwb_ref, rwbonus_ref,
                  rwgate_ref, rwn_ref, yb_ref, yc_ref, yd_ref, wg_ref, wb_ref, wo_ref, xo_ref, f_ref):
    rwy_ref = (rwf_ref[...], rwb_ref[...], rwbonus_ref[...], rwgate_ref[...])
    del idx_ref
    d = x_ref.shape[-1]
    x = x_ref[...]
    mod = mod_ref[0]
    sh1, sc1, g1, sh2, sc2 = (mod[:, j * d:(j + 1) * d] for j in range(5))
    h = _rms_mod(x, gain1_ref[...], sh1, sc1).astype(_BF)
    y = rwy_ref[0] + rwy_ref[1]
    dev = y - _head_sums(y) * (1.0 / HEAD_DIM)
    var = _head_sums(dev * dev) * (1.0 / HEAD_DIM)
    yn = dev * lax.rsqrt(var + RW_GN_EPS) * rwn_ref[0:1, :] + rwn_ref[1:2, :]
    ys = [(yn + rwy_ref[2]) * rwy_ref[3], yb_ref[...], yc_ref[...], yd_ref[...]]
    m = None
    for i, y_i in enumerate(ys):
        gate = jax.nn.sigmoid(jnp.dot(h, wg_ref[:, i * d:(i + 1) * d], preferred_element_type=_F32))
        term = gate * jnp.dot(y_i.astype(_BF), wb_ref[i], preferred_element_type=_F32)
        m = term if m is None else m + term
    x_new = x + g1 * jnp.dot(m.astype(_BF), wo_ref[...], preferred_element_type=_F32)
    xo_ref[...] = x_new
    f_ref[...] = _rms_mod(x_new, gain2_ref[...], sh2, sc2).astype(_BF)


def fused_merge(x_tok, tile_mod, mod_tab, gain1, gain2, rw_parts, rw_norm, ys, w_gate, w_branch, w_out):
    t, d = x_tok.shape
    full2 = lambda i, idx: (0, 0)
    rows = lambda i, idx: (i, 0)
    branch = pl.BlockSpec((MERGE_ROWS, BRANCH_WIDTH), rows)
    grid_spec = pltpu.PrefetchScalarGridSpec(
        num_scalar_prefetch=1,
        grid=(t // MERGE_ROWS,),
        in_specs=[pl.BlockSpec((MERGE_ROWS, d), rows),
                  pl.BlockSpec((1, 1, mod_tab.shape[-1]), lambda i, idx: (idx[i], 0, 0)),
                  pl.BlockSpec((1, d), full2), pl.BlockSpec((1, d), full2)]
        + [branch] * 4 + [pl.BlockSpec(rw_norm.shape, full2)] + [branch] * (N_BRANCH - 1)
        + [pl.BlockSpec(w_gate.shape, full2),
           pl.BlockSpec(w_branch.shape, lambda i, idx: (0, 0, 0)),
           pl.BlockSpec(w_out.shape, full2)],
        out_specs=[pl.BlockSpec((MERGE_ROWS, d), rows), pl.BlockSpec((MERGE_ROWS, d), rows)])
    return pl.pallas_call(
        _merge_kernel,
        grid_spec=grid_spec,
        out_shape=[jax.ShapeDtypeStruct((t, d), _F32), jax.ShapeDtypeStruct((t, d), _BF)],
        compiler_params=pltpu.CompilerParams(
            dimension_semantics=("parallel",), vmem_limit_bytes=VMEM_LIMIT_BIG),
        name="merge_proj",
    )(tile_mod, x_tok, mod_tab, gain1[None], gain2[None], *rw_parts, rw_norm, *ys, w_gate, w_branch, w_out)


FFN_ROWS = 512
FFN_COLS = 512


def _swiglu_accumulate(x_ref, w1_ref, w3_ref, w2_ref, acc_ref):
    @pl.when(pl.program_id(1) == 0)
    def _():
        acc_ref[...] = jnp.zeros_like(acc_ref)

    x = x_ref[...]
    h1 = jnp.dot(x, w1_ref[0], preferred_element_type=_F32)
    h3 = jnp.dot(x, w3_ref[0], preferred_element_type=_F32)
    h = (h1 * jax.nn.sigmoid(h1) * h3).astype(_BF)
    acc_ref[...] += jnp.dot(h, w2_ref[0], preferred_element_type=_F32)


def _swiglu_kernel(be_ref, x_ref, w1_ref, w3_ref, w2_ref, g_ref, o_ref, acc_ref):
    del be_ref
    _swiglu_accumulate(x_ref, w1_ref, w3_ref, w2_ref, acc_ref)

    @pl.when(pl.program_id(1) == pl.num_programs(1) - 1)
    def _():
        o_ref[...] = acc_ref[...] * g_ref[...]


def _ffn_residual_kernel(idx_ref, f_ref, x_ref, mod_ref, w1_ref, w3_ref, w2_ref, o_ref, acc_ref):
    del idx_ref
    _swiglu_accumulate(f_ref, w1_ref, w3_ref, w2_ref, acc_ref)

    @pl.when(pl.program_id(1) == pl.num_programs(1) - 1)
    def _():
        d = x_ref.shape[-1]
        o_ref[...] = x_ref[...] + mod_ref[0][:, 5 * d:6 * d] * acc_ref[...]


def ffn_residual(f_tok, x_tok, tile_mod, mod_tab, w1, w3, w2):
    t, d = x_tok.shape
    n_f = w1.shape[-1]
    rows = lambda i, f, idx: (i, 0)
    grid_spec = pltpu.PrefetchScalarGridSpec(
        num_scalar_prefetch=1,
        grid=(t // PROJ_ROWS, n_f // FFN_COLS),
        in_specs=[pl.BlockSpec((PROJ_ROWS, d), rows), pl.BlockSpec((PROJ_ROWS, d), rows),
                  pl.BlockSpec((1, 1, mod_tab.shape[-1]), lambda i, f, idx: (idx[i], 0, 0)),
                  pl.BlockSpec((1, d, FFN_COLS), lambda i, f, idx: (0, 0, f)),
                  pl.BlockSpec((1, d, FFN_COLS), lambda i, f, idx: (0, 0, f)),
                  pl.BlockSpec((1, FFN_COLS, d), lambda i, f, idx: (0, f, 0))],
        out_specs=pl.BlockSpec((PROJ_ROWS, d), rows),
        scratch_shapes=[pltpu.VMEM((PROJ_ROWS, d), _F32)])
    return pl.pallas_call(
        _ffn_residual_kernel,
        grid_spec=grid_spec,
        out_shape=jax.ShapeDtypeStruct((t, d), _F32),
        compiler_params=pltpu.CompilerParams(
            dimension_semantics=("parallel", "arbitrary"),
            vmem_limit_bytes=48 * 1024 * 1024),
        name="ffn_residual",
    )(tile_mod, f_tok, x_tok, mod_tab, w1.astype(_BF)[None], w3.astype(_BF)[None], w2.astype(_BF)[None])


def grouped_swiglu(x, block_e, gate, w1, w3, w2):
    n_rows, d = x.shape
    n_f = w1.shape[-1]
    assert n_rows % FFN_ROWS == 0 and n_f % FFN_COLS == 0
    grid_spec = pltpu.PrefetchScalarGridSpec(
        num_scalar_prefetch=1,
        grid=(n_rows // FFN_ROWS, n_f // FFN_COLS),
        in_specs=[pl.BlockSpec((FFN_ROWS, d), lambda i, f, be: (i, 0)),
                  pl.BlockSpec((1, d, FFN_COLS), lambda i, f, be: (be[i], 0, f)),
                  pl.BlockSpec((1, d, FFN_COLS), lambda i, f, be: (be[i], 0, f)),
                  pl.BlockSpec((1, FFN_COLS, d), lambda i, f, be: (be[i], f, 0)),
                  pl.BlockSpec((FFN_ROWS, 1), lambda i, f, be: (i, 0))],
        out_specs=pl.BlockSpec((FFN_ROWS, d), lambda i, f, be: (i, 0)),
        scratch_shapes=[pltpu.VMEM((FFN_ROWS, d), _F32)])
    return pl.pallas_call(
        _swiglu_kernel,
        grid_spec=grid_spec,
        out_shape=jax.ShapeDtypeStruct((n_rows, d), _F32),
        compiler_params=pltpu.CompilerParams(
            dimension_semantics=("parallel", "arbitrary"),
            vmem_limit_bytes=48 * 1024 * 1024),
        name="grouped_swiglu",
    )(block_e, x.astype(_BF), w1.astype(_BF), w3.astype(_BF), w2.astype(_BF), gate)


def moe_swiglu(h, w_router, w1, w3, w2):
    n_tok, d = h.shape
    logits = jnp.dot(h, w_router.astype(_BF), preferred_element_type=jnp.float32)
    top_logit, top_idx = lax.top_k(logits, TOP_K)
    gates = jax.nn.softmax(top_logit, axis=-1)
    n_assign = n_tok * TOP_K
    flat_e = top_idx.reshape(-1)
    order = jnp.argsort(flat_e)
    sorted_e = flat_e[order]
    sorted_tok = (order // TOP_K).astype(jnp.int32)
    sorted_gate = gates.reshape(-1)[order]
    counts = jnp.bincount(flat_e, length=N_EXPERTS)
    padded = (counts + FFN_ROWS - 1) // FFN_ROWS * FFN_ROWS
    start = jnp.cumsum(counts) - counts
    pend = jnp.cumsum(padded)
    pstart = pend - padded
    dest = pstart[sorted_e] + jnp.arange(n_assign) - start[sorted_e]
    n_blocks = -(-n_assign // FFN_ROWS) + N_EXPERTS
    n_rows = n_blocks * FFN_ROWS
    tok = jnp.zeros((n_rows,), jnp.int32).at[dest].set(sorted_tok)
    gate = jnp.zeros((n_rows,), _F32).at[dest].set(sorted_gate)
    block_e = jnp.minimum(jnp.searchsorted(pend, jnp.arange(n_blocks) * FFN_ROWS, side='right'),
                          N_EXPERTS - 1).astype(jnp.int32)
    yb = grouped_swiglu(h[tok], block_e, gate[:, None], w1, w3, w2)
    return jnp.zeros((n_tok, d), _F32).at[tok].add(yb)


def kernel(x, c, ctx, c_ctx, w_mod, b_mod, norm1, norm2, w_in, w_branch, w_out,
           rw_mu, rw_w0, rw_w1, rw_w2, rw_a0, rw_a1, rw_a2, rw_v0, rw_v1, rw_v2,
           rw_g1, rw_g2, rw_k_k, rw_k_a, rw_r_k, rw_ln_w, rw_ln_b, sw_sink,
           mla_q_norm, mla_w_uq, mla_kv_norm, mla_w_ukv, ga_q_norm, ga_k_norm,
           ffn_w1, ffn_w3, ffn_w2, moe_router, moe_w1, moe_w3, moe_w2, final_norm):
    b, s_len, d = x.shape
    n_ctx_tok = ctx.shape[1]
    n_c = b * n_ctx_tok
    assert n_c % PROJ_ROWS == 0 and s_len % PROJ_ROWS == 0
    rope64 = axial_rope_tables(s_len, HEAD_DIM)
    rope32 = axial_rope_tables(s_len, MLA_ROPE)
    silu_c = jax.nn.silu(c)
    silu_cc = jax.nn.silu(c_ctx)
    x_tok = jnp.concatenate([ctx.reshape(n_c, d), x.reshape(b * s_len, d)], axis=0)
    seg_rows = [n_c] + [s_len] * b

    def tile_table(rows):
        return jnp.concatenate([jnp.full((n // rows,), i, jnp.int32) for i, n in enumerate(seg_rows)])

    tile_mod_proj, tile_mod_merge = tile_table(PROJ_ROWS), tile_table(MERGE_ROWS)

    def split_tokens(t):
        return t[:n_c].reshape(b, n_ctx_tok, -1), t[n_c:].reshape(b, s_len, -1)

    vf_tok = None
    for l in range(DEPTH):
        ctx_out = l < DEPTH - 1
        mod_l = silu_c @ w_mod[l] + b_mod[l]
        mod_c = silu_cc @ w_mod[l] + b_mod[l]
        mod_tab = jnp.concatenate([mod_c[None], mod_l], axis=0)[:, None, :]

        w_groups = []
        for off, width in PROJ_GROUPS:
            w = w_in[l][:, off:off + width].astype(_BF)
            w_groups.append(jnp.pad(w, ((0, 0), (0, _round_up(width, LANES) - width))))
        p_rw, p_sw, p_mla, p_ga = fused_in_proj(x_tok, tile_mod_proj, mod_tab, norm1[l], w_groups)
        pc, pl_ = [None], [None]
        for p, widths in ((p_sw, IN_WIDTHS[1:4]), (p_mla, IN_WIDTHS[4:7]), (p_ga, IN_WIDTHS[7:10])):
            p_ctx, p_lat = split_tokens(p)
            pc += split_cols(p_ctx, widths)
            pl_ += split_cols(p_lat, widths)
        rw = dict(mu=rw_mu[l], w0=rw_w0[l], w1=rw_w1[l], w2=rw_w2[l],
                  a0=rw_a0[l], a1=rw_a1[l], a2=rw_a2[l],
                  v0=rw_v0[l - 1] if l > 0 else None,
                  v1=rw_v1[l - 1] if l > 0 else None,
                  v2=rw_v2[l - 1] if l > 0 else None,
                  g1=rw_g1[l], g2=rw_g2[l], k_k=rw_k_k[l], k_a=rw_k_a[l], r_k=rw_r_k[l],
                  ln_w=rw_ln_w[l], ln_b=rw_ln_b[l])
        rw_yf, rw_yb, rw_bonus, rw_gate, rw_v = rwkv_branch(p_rw, vf_tok, rw, b, n_ctx_tok, s_len)
        if l == 0:
            vf_tok = rw_v
        yb_c, yb_l = window_mixer(pc[1:4], pl_[1:4], sw_sink[l], rope64, ctx_out)
        yc_c, yc_l = mla_mixer(pc[4:7], pl_[4:7], mla_q_norm[l], mla_w_uq[l], mla_kv_norm[l],
                               mla_w_ukv[l], rope32, ctx_out)
        yd_c, yd_l = grid_attention_mixer(pc[7:10], pl_[7:10], ga_q_norm[l], ga_k_norm[l], rope64, ctx_out)

        def tokens_of(y_ctx, y_lat):
            y_ctx = jnp.zeros((n_c, BRANCH_WIDTH), _F32) if y_ctx is None else y_ctx.reshape(n_c, BRANCH_WIDTH)
            return jnp.concatenate([y_ctx, y_lat.reshape(b * s_len, BRANCH_WIDTH)], axis=0)

        ys = [tokens_of(yc_, yl_) for yc_, yl_ in ((yb_c, yb_l), (yc_c, yc_l), (yd_c, yd_l))]
        x_tok, f_tok = fused_merge(x_tok, tile_mod_merge, mod_tab, norm1[l], norm2[l],
                                   (rw_yf, rw_yb, rw_bonus, rw_gate),
                                   jnp.stack([rw_ln_w[l], rw_ln_b[l]]), ys,
                                   w_in[l][:, MIXER_COLS:].astype(_BF), w_branch[l].astype(_BF),
                                   w_out[l].astype(_BF))
        if l % 2 == 0:
            x_tok = ffn_residual(f_tok, x_tok, tile_mod_proj, mod_tab,
                                 ffn_w1[l // 2], ffn_w3[l // 2], ffn_w2[l // 2])
        else:
            first = 0 if ctx_out else n_c
            ffn_out = moe_swiglu(f_tok[first:], moe_router[l // 2], moe_w1[l // 2], moe_w3[l // 2],
                                 moe_w2[l // 2])
            pieces, row = [], 0
            for i, n in enumerate(seg_rows):
                if row >= first:
                    g2 = mod_tab[i, :, 5 * d:]
                    pieces.append(x_tok[row:row + n] + g2 * ffn_out[row - first:row - first + n])
                else:
                    pieces.append(x_tok[row:row + n])
                row += n
            x_tok = jnp.concatenate(pieces, axis=0)
    return rms_norm(x_tok[n_c:].reshape(b, s_len, d), final_norm)
```

```python
import functools

import jax
import jax.numpy as jnp
from jax import lax
from jax.experimental import pallas as pl
from jax.experimental.pallas import tpu as pltpu

D_MODEL = 1024
DEPTH = 4
GRID_W = 64
HEAD_DIM = 64
BLOCK = 128
WINDOW = 128
ROPE_BASE = 10000.0
NORM_EPS = 1e-6
NEG_INF = -1e30
N_BRANCH = 4
BRANCH_WIDTH = 256
RW_HEADS = 4
RW_WIDTH = RW_HEADS * HEAD_DIM
RW_GN_EPS = 64e-5
SW_HEADS = 4
SW_KV_HEADS = 2
MLA_HEADS = 4
MLA_NOPE = 64
MLA_ROPE = 32
MLA_V = 64
MLA_Q_RANK = 256
MLA_KV_RANK = 128
GA_HEADS = 4
GA_KV_HEADS = 2
N_EXPERTS = 8
TOP_K = 2
MOE_BLOCK = 128
IN_WIDTHS = (4 * RW_WIDTH,
             SW_HEADS * HEAD_DIM, SW_KV_HEADS * HEAD_DIM, SW_KV_HEADS * HEAD_DIM,
             MLA_Q_RANK, MLA_KV_RANK, MLA_ROPE,
             GA_HEADS * HEAD_DIM, GA_KV_HEADS * HEAD_DIM, GA_KV_HEADS * HEAD_DIM,
             N_BRANCH * D_MODEL)


def _mm_kernel(a_ref, b_ref, o_ref):
    o_ref[...] = jnp.dot(a_ref[...], b_ref[...], preferred_element_type=jnp.float32)


def _round_up(x, m):
    return (x + m - 1) // m * m


def pmm(a, b, tm=512, tn=512):
    m, k = a.shape
    _, n = b.shape
    tm = min(tm, _round_up(m, 8))
    tn = min(tn, _round_up(n, 128))
    mp, np_ = _round_up(m, tm), _round_up(n, tn)
    a = a.astype(jnp.bfloat16)
    b = b.astype(jnp.bfloat16)
    if mp != m:
        a = jnp.pad(a, ((0, mp - m), (0, 0)))
    if np_ != n:
        b = jnp.pad(b, ((0, 0), (0, np_ - n)))
    out = pl.pallas_call(
        _mm_kernel,
        grid=(np_ // tn, mp // tm),
        in_specs=[pl.BlockSpec((tm, k), lambda j, i: (i, 0)),
                  pl.BlockSpec((k, tn), lambda j, i: (0, j))],
        out_specs=pl.BlockSpec((tm, tn), lambda j, i: (i, j)),
        out_shape=jax.ShapeDtypeStruct((mp, np_), jnp.float32),
        compiler_params=pltpu.CompilerParams(
            dimension_semantics=("arbitrary", "arbitrary"),
            vmem_limit_bytes=48 * 1024 * 1024),
        name="pmm",
    )(a, b)
    return out[:m, :n]


def pmm_nd(a, b, **kw):
    lead = a.shape[:-1]
    return pmm(a.reshape(-1, a.shape[-1]), b, **kw).reshape(lead + (b.shape[-1],))


RW_CHUNK = 64
RW_CHUNKS_PER_STEP = 4
RW_BLOCK = RW_CHUNK * RW_CHUNKS_PER_STEP
RW_INV_BASE = 4

_BF = jnp.bfloat16
_F32 = jnp.float32


def _bd_rows(x):
    lane_head = lax.broadcasted_iota(jnp.int32, x.shape, 1) // HEAD_DIM
    return jnp.concatenate([jnp.where(lane_head == h, x, 0.0) for h in range(RW_HEADS)], axis=0)


def _dot(a, b):
    return jnp.dot(a.astype(_BF), b.astype(_BF), preferred_element_type=_F32)


def _dot_nt(a, b):
    return lax.dot_general(a.astype(_BF), b.astype(_BF), (((1,), (1,)), ((), ())),
                           preferred_element_type=_F32)


def _dot_tn(a, b):
    return lax.dot_general(a.astype(_BF), b.astype(_BF), (((0,), (0,)), ((), ())),
                           preferred_element_type=_F32)


def _split3(x):
    h1 = x.astype(_BF)
    r1 = x - h1.astype(_F32)
    h2 = r1.astype(_BF)
    h3 = (r1 - h2.astype(_F32)).astype(_BF)
    return h1, h2, h3


def _each(fn, *lists):
    return [fn(*args) for args in zip(*lists)]


def _chunk_summaries(r, lw, k, v, a, b, reverse):
    c = RW_CHUNK
    row = lax.broadcasted_iota(jnp.int32, (c, c), 0)
    col = lax.broadcasted_iota(jnp.int32, (c, c), 1)
    tri = ((row <= col) if reverse else (row >= col)).astype(_BF)
    cum = _each(lambda x: sum(jnp.dot(tri, p, preferred_element_type=_F32) for p in _split3(x)), lw)
    lwlast = _each(lambda x: x[0:1, :] if reverse else x[c - 1:c, :], cum)
    rt = _each(lambda x, cu: x * jnp.exp(cu), r, cum)
    at = _each(lambda x, cu, l: x * jnp.exp(cu - l), a, cum, lw)
    w_inv = _each(lambda cu: jnp.exp(-cu), cum)
    w_tail = _each(lambda cu, ll: jnp.exp(ll - cu), cum, lwlast)
    wlast = _each(jnp.exp, lwlast)
    bt = _each(jnp.multiply, b, w_inv)
    kt = _each(jnp.multiply, k, w_inv)
    bh = _each(jnp.multiply, b, w_tail)
    kh = _each(jnp.multiply, k, w_tail)

    t_idx = lax.broadcasted_iota(jnp.int32, (c, RW_WIDTH), 0)
    s_idx = lax.broadcasted_iota(jnp.int32, (c, RW_WIDTH), 1) % HEAD_DIM
    strict = (s_idx > t_idx) if reverse else (s_idx < t_idx)
    incl = (s_idx >= t_idx) if reverse else (s_idx <= t_idx)

    l1 = _each(lambda x, y: jnp.concatenate([x, y], axis=0), at, rt)
    g_b = _each(lambda x, y: _dot_nt(x, _bd_rows(y)), l1, bt)
    g_k = _each(lambda x, y: _dot_nt(x, _bd_rows(y)), l1, kt)
    a_ab = _each(lambda g: jnp.where(strict, g[:c], 0.0), g_b)
    a_rb = _each(lambda g: jnp.where(incl, g[c:], 0.0), g_b)
    a_ak = _each(lambda g: jnp.where(strict, g[:c], 0.0), g_k)
    a_rk = _each(lambda g: jnp.where(incl, g[c:], 0.0), g_k)

    eye = jnp.where(s_idx == t_idx, 1.0, 0.0)
    base = RW_INV_BASE
    n_b = _each(lambda x: jnp.where(t_idx // base == s_idx // base, x, 0.0), a_ab)
    n_b2 = _each(lambda x: _dot(x, _bd_rows(x)), n_b)
    n_b3 = _each(lambda x, y: _dot(x, _bd_rows(y)), n_b, n_b2)
    tm = _each(lambda x, y, z: eye + x + y + z, n_b, n_b2, n_b3)
    blk = base
    while blk < c:
        off = (t_idx // (2 * blk) == s_idx // (2 * blk)) & (t_idx // blk != s_idx // blk)
        x = _each(lambda t, n: _dot(t, _bd_rows(jnp.where(off, n, 0.0))), tm, a_ab)
        tm = _each(lambda t, xx: t + _dot(xx, _bd_rows(t)), tm, x)
        blk *= 2

    av = _each(lambda x, y, vv: _dot(jnp.concatenate([x, y], axis=0), _bd_rows(vv)), a_ak, a_rk, v)
    ta = _each(lambda t, x: _dot(t, _bd_rows(x)), tm, at)
    u0 = _each(lambda t, x: _dot(t, _bd_rows(x[:c])), tm, av)
    r1 = _each(lambda x, ar, t: x + _dot(ar, _bd_rows(t)), rt, a_rb, ta)
    y0 = _each(lambda x, ar, u: x[c:] + _dot(ar, _bd_rows(u)), av, a_rb, u0)

    blk_r = lax.broadcasted_iota(jnp.int32, (RW_WIDTH, RW_WIDTH), 0) // HEAD_DIM
    blk_c = lax.broadcasted_iota(jnp.int32, (RW_WIDTH, RW_WIDTH), 1) // HEAD_DIM
    m_bd = _each(lambda t, x: jnp.where(blk_r == blk_c, _dot_tn(t, x), 0.0), ta, bh)
    z = _each(lambda u, vv, x, y: _dot_tn(jnp.concatenate([u, vv], axis=0),
                                          jnp.concatenate([x, y], axis=0)), u0, v, bh, kh)
    lane_head = lax.broadcasted_iota(jnp.int32, (HEAD_DIM, RW_WIDTH), 1) // HEAD_DIM

    def diag_blocks(zz):
        out = zz[:HEAD_DIM]
        for h in range(1, RW_HEADS):
            out = jnp.where(lane_head == h, zz[h * HEAD_DIM:(h + 1) * HEAD_DIM], out)
        return out

    sadd = _each(diag_blocks, z)
    return r1, y0, m_bd, sadd, wlast


def _rw_summary_kernel(r_ref, lw_ref, k_ref, v_ref, a_ref, b_ref,
                       r1_ref, y0_ref, m_ref, sadd_ref, wl_ref, *, reverse):
    c = RW_CHUNK

    def chunks(ref):
        return [ref[pl.ds(j * c, c), :] for j in range(RW_CHUNKS_PER_STEP)]

    r1, y0, m_bd, sadd, wlast = _chunk_summaries(
        chunks(r_ref), chunks(lw_ref), chunks(k_ref), chunks(v_ref), chunks(a_ref), chunks(b_ref),
        reverse)
    r1_ref[...] = jnp.concatenate(r1, axis=0)
    y0_ref[...] = jnp.concatenate(y0, axis=0)
    m_ref[...] = jnp.stack(m_bd, axis=0).astype(_BF)
    sadd_ref[...] = jnp.stack(sadd, axis=0)
    wl_ref[...] = jnp.stack(wlast, axis=0)


def rw_summaries(r_src, lw, k, v, a, b, reverse):
    t, w = lw.shape
    c, cps = RW_CHUNK, RW_CHUNKS_PER_STEP
    assert t % RW_BLOCK == 0 and w == RW_WIDTH
    rows = pl.BlockSpec((RW_BLOCK, w), lambda i: (i, 0))
    return pl.pallas_call(
        functools.partial(_rw_summary_kernel, reverse=reverse),
        grid=(t // RW_BLOCK,),
        in_specs=[rows] * 6,
        out_specs=[rows, rows,
                   pl.BlockSpec((cps, w, w), lambda i: (i, 0, 0)),
                   pl.BlockSpec((cps, HEAD_DIM, w), lambda i: (i, 0, 0)),
                   pl.BlockSpec((cps, 1, w), lambda i: (i, 0, 0))],
        out_shape=[jax.ShapeDtypeStruct((t, w), _F32),
                   jax.ShapeDtypeStruct((t, w), _F32),
                   jax.ShapeDtypeStruct((t // c, w, w), _BF),
                   jax.ShapeDtypeStruct((t // c, HEAD_DIM, w), _F32),
                   jax.ShapeDtypeStruct((t // c, 1, w), _F32)],
        compiler_params=pltpu.CompilerParams(
            dimension_semantics=("parallel",), vmem_limit_bytes=48 * 1024 * 1024),
        name="rw_chunk_summary",
    )(r_src, lw, k, v, a, b)


def _rw_state_kernel(*refs):
    s_ref = refs[-1]
    y_refs = refs[10:12]
    c, cps = RW_CHUNK, RW_CHUNKS_PER_STEP

    @pl.when(pl.program_id(1) == 0)
    def _():
        s_ref[...] = jnp.zeros_like(s_ref)

    s = [s_ref[0], s_ref[1]]
    ys = [[None] * cps, [None] * cps]
    for step in range(cps):
        js = (step, cps - 1 - step)
        r1, y0 = ([refs[5 * d + n][pl.ds(js[d] * c, c), :] for d in range(2)] for n in range(2))
        m_bd, sadd, wlast = ([refs[5 * d + n][js[d]] for d in range(2)] for n in range(2, 5))
        y = _each(lambda rr, ss, yy: _dot_nt(rr, _bd_rows(ss)) + yy, r1, s, y0)
        s = _each(lambda ss, ww, mm, aa: ss * ww + _dot(ss, mm) + aa, s, wlast, m_bd, sadd)
        for d in range(2):
            ys[d][js[d]] = y[d]
    for d in range(2):
        y_refs[d][...] = jnp.concatenate(ys[d], axis=0)
        s_ref[d] = s[d]


def rw_state_pass(sum_fwd, sum_bwd, n_batch, n_ctx, n_lat):
    t, w = sum_fwd[0].shape
    cps = RW_CHUNKS_PER_STEP
    assert n_ctx % RW_BLOCK == 0 and n_lat % RW_BLOCK == 0
    n_cb, n_lb = n_ctx // RW_BLOCK, n_lat // RW_BLOCK
    lat0 = n_batch * n_cb

    def fwd_blk(q, i):
        return jnp.where(i < n_cb, q * n_cb + i, lat0 + q * n_lb + (i - n_cb))

    def bwd_blk(q, i):
        return jnp.where(i < n_cb, q * n_cb + (n_cb - 1 - i), lat0 + q * n_lb + (n_lb - 1 - (i - n_cb)))

    def specs(blk):
        return [pl.BlockSpec((RW_BLOCK, w), lambda q, i: (blk(q, i), 0)),
                pl.BlockSpec((RW_BLOCK, w), lambda q, i: (blk(q, i), 0)),
                pl.BlockSpec((cps, w, w), lambda q, i: (blk(q, i), 0, 0)),
                pl.BlockSpec((cps, HEAD_DIM, w), lambda q, i: (blk(q, i), 0, 0)),
                pl.BlockSpec((cps, 1, w), lambda q, i: (blk(q, i), 0, 0))]

    return pl.pallas_call(
        _rw_state_kernel,
        grid=(n_batch, n_cb + n_lb),
        in_specs=specs(fwd_blk) + specs(bwd_blk),
        out_specs=[pl.BlockSpec((RW_BLOCK, w), lambda q, i: (fwd_blk(q, i), 0)),
                   pl.BlockSpec((RW_BLOCK, w), lambda q, i: (bwd_blk(q, i), 0))],
        out_shape=[jax.ShapeDtypeStruct((t, w), _F32)] * 2,
        scratch_shapes=[pltpu.VMEM((2, HEAD_DIM, w), _F32)],
        compiler_params=pltpu.CompilerParams(
            dimension_semantics=("arbitrary", "arbitrary"),
            vmem_limit_bytes=48 * 1024 * 1024),
        name="rw_state_pass",
    )(*sum_fwd, *sum_bwd)


ATTN_TQ = 256
ATTN_TK = 768


LANES = 128
LOG2_E = 1.4426950408889634


def _dense_attn_kernel(q_ref, k_ref, v_ref, o_ref, *, tk):
    g, tq, dq = q_ref.shape[2:]
    dv = o_ref.shape[-1]
    n_keys = k_ref.shape[2]
    q = q_ref[0, 0].reshape(g * tq, dq)

    def scores(i):
        kc = k_ref[0, 0, pl.ds(i * tk, tk), :]
        return lax.dot_general(q, kc, (((1,), (1,)), ((), ())), preferred_element_type=_F32)

    def update(i, m, acc, s):
        vc = v_ref[0, 0, pl.ds(i * tk, tk), :]
        m_new = jnp.maximum(m, jnp.max(s, axis=-1, keepdims=True))
        p = jnp.exp2(s - m_new).astype(_BF)
        return m_new, jnp.exp2(m - m_new) * acc + jnp.dot(p, vc, preferred_element_type=_F32)

    n_chunks = n_keys // tk
    m = jnp.full((g * tq, 1), -jnp.inf, _F32)
    acc = jnp.zeros((g * tq, v_ref.shape[-1]), _F32)
    s = scores(0)
    for i in range(n_chunks):
        s_next = scores(i + 1) if i + 1 < n_chunks else None
        m, acc = update(i, m, acc, s)
        s = s_next
    o_ref[0, 0] = (acc[:, :dv] / acc[:, dv:dv + 1]).reshape(g, tq, dv)


def dense_block_attention(q, k, v, scale):
    b, s_len, hk, g, dq = q.shape
    n_keys, dv = k.shape[1], v.shape[-1]
    tq = min(ATTN_TQ, s_len)
    tk = ATTN_TK if n_keys % ATTN_TK == 0 else n_keys
    assert s_len % tq == 0
    qt = jnp.transpose(q * (scale * LOG2_E), (0, 2, 3, 1, 4)).astype(_BF)
    kt = jnp.transpose(k, (0, 2, 1, 3)).astype(_BF)
    vt = jnp.transpose(v, (0, 2, 1, 3))
    vt = jnp.concatenate([vt, jnp.ones_like(vt[..., :1]),
                          jnp.zeros(vt.shape[:-1] + (LANES - dv - 1,), vt.dtype)], axis=-1).astype(_BF)
    o = pl.pallas_call(
        functools.partial(_dense_attn_kernel, tk=tk),
        grid=(b, hk, s_len // tq),
        in_specs=[pl.BlockSpec((1, 1, g, tq, dq), lambda bi, h, i: (bi, h, 0, i, 0)),
                  pl.BlockSpec((1, 1, n_keys, dq), lambda bi, h, i: (bi, h, 0, 0)),
                  pl.BlockSpec((1, 1, n_keys, LANES), lambda bi, h, i: (bi, h, 0, 0))],
        out_specs=pl.BlockSpec((1, 1, g, tq, dv), lambda bi, h, i: (bi, h, 0, i, 0)),
        out_shape=jax.ShapeDtypeStruct((b, hk, g, s_len, dv), _F32),
        compiler_params=pltpu.CompilerParams(
            dimension_semantics=("parallel", "parallel", "parallel"),
            vmem_limit_bytes=48 * 1024 * 1024),
        name="dense_attention",
    )(qt, kt, vt)
    return jnp.transpose(o, (0, 3, 1, 2, 4)).reshape(b, s_len, hk * g * dv)


def rms_norm(x, g):
    xf = x.astype(jnp.float32)
    y = xf * lax.rsqrt(jnp.mean(xf * xf, axis=-1, keepdims=True) + NORM_EPS)
    return (y * g.astype(jnp.float32)).astype(x.dtype)


def split_cols(p, widths):
    out, off = [], 0
    for w in widths:
        out.append(p[..., off:off + w])
        off += w
    return out


def axial_rope_tables(n_tokens, rot_dim):
    rows = n_tokens // GRID_W
    row = jnp.repeat(jnp.arange(rows, dtype=jnp.float32), GRID_W)
    col = jnp.tile(jnp.arange(GRID_W, dtype=jnp.float32), rows)
    n_freq = rot_dim // 4
    inv_freq = ROPE_BASE ** (-jnp.arange(n_freq, dtype=jnp.float32) / n_freq)
    ang = jnp.concatenate([row[:, None] * inv_freq, col[:, None] * inv_freq], axis=-1)
    return jnp.cos(ang), jnp.sin(ang)


def apply_rope(x, cos, sin):
    half = x.shape[-1] // 2
    c = cos[None, :, None, :].astype(x.dtype)
    s = sin[None, :, None, :].astype(x.dtype)
    x1, x2 = x[..., :half], x[..., half:]
    return jnp.concatenate([x1 * c - x2 * s, x1 * s + x2 * c], axis=-1)


def context_attention(q, k, v, scale, sink=None):
    b, n = q.shape[:2]
    s = jnp.einsum('bqhgd,bkhd->bhgqk', q, k).astype(jnp.float32) * scale
    if sink is not None:
        s_sink = jnp.broadcast_to(sink[None, :, :, None, None].astype(jnp.float32), s.shape[:-1] + (1,))
        p = jax.nn.softmax(jnp.concatenate([s_sink, s], axis=-1), axis=-1)[..., 1:]
    else:
        p = jax.nn.softmax(s, axis=-1)
    o = jnp.einsum('bhgqk,bkhd->bqhgd', p.astype(v.dtype), v)
    return o.reshape(b, n, -1)


def banded_window_attention(q, k, v, k_ctx, v_ctx, sink, scale):
    b, s_len, hk, g, d = q.shape
    nb = s_len // BLOCK
    qb = q.reshape(b, nb, BLOCK, hk, g, d)

    def neighbours(t):
        tb = jnp.pad(t.reshape(b, nb, BLOCK, hk, t.shape[-1]), ((0, 0), (1, 1), (0, 0), (0, 0), (0, 0)))
        return jnp.concatenate([tb[:, :-2], tb[:, 1:-1], tb[:, 2:]], axis=2)

    kb, vb = neighbours(k), neighbours(v)
    blk = jnp.arange(nb)
    q_pos = blk[:, None] * BLOCK + jnp.arange(BLOCK)[None, :]
    k_pos = (blk[:, None] - 1) * BLOCK + jnp.arange(3 * BLOCK)[None, :]
    valid = ((jnp.abs(q_pos[:, :, None] - k_pos[:, None, :]) <= WINDOW)
             & (k_pos >= 0)[:, None, :] & (k_pos < s_len)[:, None, :])
    s_win = jnp.einsum('bnqhgd,bnkhd->bnhgqk', qb, kb).astype(jnp.float32) * scale
    s_win = jnp.where(valid[None, :, None, None], s_win, NEG_INF)
    s_ctx = jnp.einsum('bnqhgd,bchd->bnhgqc', qb, k_ctx).astype(jnp.float32) * scale
    s_sink = jnp.broadcast_to(sink[None, None, :, :, None, None].astype(jnp.float32), s_ctx.shape[:-1] + (1,))
    p = jax.nn.softmax(jnp.concatenate([s_sink, s_ctx, s_win], axis=-1), axis=-1).astype(v.dtype)
    n_ctx = k_ctx.shape[1]
    o = (jnp.einsum('bnhgqc,bchd->bnqhgd', p[..., 1:1 + n_ctx], v_ctx)
         + jnp.einsum('bnhgqk,bnkhd->bnqhgd', p[..., 1 + n_ctx:], vb))
    return o.reshape(b, s_len, -1)


def gqa_heads(q, k, v, n_heads, n_kv, rope=None, q_gain=None, k_gain=None):
    b, n, _ = q.shape
    q = q.reshape(b, n, n_heads, HEAD_DIM)
    k = k.reshape(b, n, n_kv, HEAD_DIM)
    v = v.reshape(b, n, n_kv, HEAD_DIM)
    if q_gain is not None:
        q = rms_norm(q, q_gain)
        k = rms_norm(k, k_gain)
    if rope is not None:
        q = apply_rope(q, rope[0], rope[1])
        k = apply_rope(k, rope[0], rope[1])
    return q.reshape(b, n, n_kv, n_heads // n_kv, HEAD_DIM), k, v


def window_mixer(pc, pl_, sink, rope, ctx_out):
    scale = HEAD_DIM ** -0.5
    qc, kc, vc = gqa_heads(pc[0], pc[1], pc[2], SW_HEADS, SW_KV_HEADS)
    ql, kl, vl = gqa_heads(pl_[0], pl_[1], pl_[2], SW_HEADS, SW_KV_HEADS, rope)
    sink = sink.reshape(SW_KV_HEADS, SW_HEADS // SW_KV_HEADS)
    y_lat = banded_window_attention(ql, kl, vl, kc, vc, sink, scale)
    y_ctx = context_attention(qc, kc, vc, scale, sink) if ctx_out else None
    return y_ctx, y_lat


def grid_attention_mixer(pc, pl_, q_gain, k_gain, rope, ctx_out):
    scale = HEAD_DIM ** -0.5
    qc, kc, vc = gqa_heads(pc[0], pc[1], pc[2], GA_HEADS, GA_KV_HEADS, None, q_gain, k_gain)
    ql, kl, vl = gqa_heads(pl_[0], pl_[1], pl_[2], GA_HEADS, GA_KV_HEADS, rope, q_gain, k_gain)
    y_lat = dense_block_attention(ql, jnp.concatenate([kc, kl], axis=1), jnp.concatenate([vc, vl], axis=1), scale)
    y_ctx = context_attention(qc, kc, vc, scale) if ctx_out else None
    return y_ctx, y_lat


def mla_mixer(pc, pl_, q_gain, w_uq, kv_gain, w_ukv, rope, ctx_out):
    scale = (MLA_NOPE + MLA_ROPE) ** -0.5

    def project(cq, ckv, k_rope, rope_tab):
        b, n, _ = cq.shape
        q = (rms_norm(cq, q_gain) @ w_uq).reshape(b, n, MLA_HEADS, MLA_NOPE + MLA_ROPE)
        kv = (rms_norm(ckv, kv_gain) @ w_ukv).reshape(b, n, MLA_HEADS, MLA_NOPE + MLA_V)
        q_nope, q_pe = q[..., :MLA_NOPE], q[..., MLA_NOPE:]
        k_nope, v = kv[..., :MLA_NOPE], kv[..., MLA_NOPE:]
        k_pe = k_rope[:, :, None, :]
        if rope_tab is not None:
            q_pe = apply_rope(q_pe, rope_tab[0], rope_tab[1])
            k_pe = apply_rope(k_pe, rope_tab[0], rope_tab[1])
        q = jnp.concatenate([q_nope, q_pe], axis=-1)
        k = jnp.concatenate([k_nope, jnp.broadcast_to(k_pe, (b, n, MLA_HEADS, MLA_ROPE))], axis=-1)
        return q[:, :, :, None, :], k, v

    qc, kc, vc = project(pc[0], pc[1], pc[2], None)
    ql, kl, vl = project(pl_[0], pl_[1], pl_[2], rope)
    y_lat = dense_block_attention(ql, jnp.concatenate([kc, kl], axis=1), jnp.concatenate([vc, vl], axis=1), scale)
    y_ctx = context_attention(qc, kc, vc, scale) if ctx_out else None
    return y_ctx, y_lat


def centred_token_shift(f, mu):
    prev = jnp.pad(f[:, :-1], ((0, 0), (1, 0), (0, 0)))
    nxt = jnp.pad(f[:, 1:], ((0, 0), (0, 1), (0, 0)))
    return f + mu[0] * (prev - f) + mu[1] * (nxt - f)


def _head_sums(x):
    w = x.shape[-1]
    blk_r = lax.broadcasted_iota(jnp.int32, (w, w), 0) // HEAD_DIM
    blk_c = lax.broadcasted_iota(jnp.int32, (w, w), 1) // HEAD_DIM
    ones_bd = (blk_r == blk_c).astype(_BF)
    hi = x.astype(_BF)
    lo = (x - hi.astype(_F32)).astype(_BF)
    return (jnp.dot(hi, ones_bd, preferred_element_type=_F32)
            + jnp.dot(lo, ones_bd, preferred_element_type=_F32))


RW_FEATURE_ROWS = 512
_VEC_V0, _VEC_W0, _VEC_A0, _VEC_KK, _VEC_KA, _VEC_RK = 0, 1, 3, 5, 6, 7


def _rw_feature_kernel(f_ref, vf_ref, vec_ref, v1_ref, v2_ref, g1_ref, g2_ref, w1_ref, w2_ref,
                       a1_ref, a2_ref, v_ref, nkk_ref, g_ref, bonus_ref,
                       lw0_ref, kd0_ref, b0_ref, lw1_ref, kd1_ref, b1_ref, *, value_residual):
    w = RW_WIDTH
    f = f_ref[...]
    r, k, v, z = (f[:, j * w:(j + 1) * w] for j in range(4))
    vec = vec_ref[...]

    def row(i):
        return vec[i:i + 1, :]

    zb = z.astype(_BF)

    def lora(a, b, act=None):
        h = jnp.dot(zb, a.astype(_BF), preferred_element_type=_F32)
        h = h if act is None else act(h)
        return jnp.dot(h.astype(_BF), b.astype(_BF), preferred_element_type=_F32)

    if value_residual:
        v = v + (vf_ref[...] - v) * jax.nn.sigmoid(row(_VEC_V0) + lora(v1_ref[...], v2_ref[...]))
    kk = k * row(_VEC_KK)
    kk = kk * lax.rsqrt(_head_sums(kk * kk) + 1e-12)
    bonus = jnp.zeros_like(v)
    for d, (lw_ref, kd_ref, b_ref) in enumerate(((lw0_ref, kd0_ref, b0_ref), (lw1_ref, kd1_ref, b1_ref))):
        wl = -jax.nn.softplus(-(row(_VEC_W0 + d) + lora(w1_ref[d], w2_ref[d], jnp.tanh))) - 0.5
        a = jax.nn.sigmoid(row(_VEC_A0 + d) + lora(a1_ref[d], a2_ref[d]))
        k_d = k * (1 + (a - 1) * row(_VEC_KA))
        bonus = bonus + _head_sums(r * k_d * row(_VEC_RK)) * v
        lw_ref[...] = -jnp.exp(wl)
        kd_ref[...] = k_d
        b_ref[...] = kk * a
    v_ref[...] = v
    nkk_ref[...] = -kk
    g_ref[...] = lora(g1_ref[...], g2_ref[...], jax.nn.sigmoid)
    bonus_ref[...] = bonus


def rw_features(f_tok, vf_tok, rw, value_residual):
    t = f_tok.shape[0]
    w = RW_WIDTH
    zeros = jnp.zeros((w,), _F32)
    vec = jnp.stack([rw['v0'] if value_residual else zeros, rw['w0'][0], rw['w0'][1],
                     rw['a0'][0], rw['a0'][1], rw['k_k'], rw['k_a'], rw['r_k']])
    if value_residual:
        v1, v2 = rw['v1'], rw['v2']
    else:
        vf_tok = f_tok
        v1, v2 = jnp.zeros((w, 8), _F32), jnp.zeros((8, w), _F32)
    weights = [v1, v2, rw['g1'], rw['g2'], rw['w1'], rw['w2'], rw['a1'], rw['a2']]

    def full(a):
        return pl.BlockSpec(a.shape, lambda i, n=a.ndim: (0,) * n)

    rows = pl.BlockSpec((RW_FEATURE_ROWS, w), lambda i: (i, 0))
    return pl.pallas_call(
        functools.partial(_rw_feature_kernel, value_residual=value_residual),
        grid=(t // RW_FEATURE_ROWS,),
        in_specs=[pl.BlockSpec((RW_FEATURE_ROWS, 4 * w), lambda i: (i, 0)), rows, full(vec)]
        + [full(a) for a in weights],
        out_specs=[rows] * 10,
        out_shape=[jax.ShapeDtypeStruct((t, w), _F32)] * 10,
        compiler_params=pltpu.CompilerParams(
            dimension_semantics=("parallel",), vmem_limit_bytes=48 * 1024 * 1024),
        name="rw_features",
    )(f_tok, vf_tok, vec, *weights)


def rwkv_branch(p_rw, vf_tok, rw, n_batch, n_ctx, n_lat):
    n_c = n_batch * n_ctx
    value_residual = vf_tok is not None
    f_tok = jnp.concatenate(
        [centred_token_shift(p.reshape(n_batch, n, -1), rw['mu']).reshape(n_batch * n, -1)
         for p, n in ((p_rw[:n_c], n_ctx), (p_rw[n_c:], n_lat))], axis=0)
    v, nkk, g, bonus, lw0, kd0, b0, lw1, kd1, b1 = rw_features(f_tok, vf_tok, rw, value_residual)
    sum_fwd = rw_summaries(f_tok, lw0, kd0, v, nkk, b0, False)
    sum_bwd = rw_summaries(f_tok, lw1, kd1, v, nkk, b1, True)
    y_fwd, y_bwd = rw_state_pass(sum_fwd, sum_bwd, n_batch, n_ctx, n_lat)
    return y_fwd, y_bwd, bonus, g, v


PROJ_ROWS = 512
MERGE_ROWS = 256
MIXER_COLS = sum(IN_WIDTHS[:-1])
PROJ_GROUPS = ((0, 4 * RW_WIDTH), (4 * RW_WIDTH, 512), (4 * RW_WIDTH + 512, 416),
               (4 * RW_WIDTH + 928, 512))
VMEM_LIMIT_BIG = 56 * 1024 * 1024


def _rms_mod(x, gain, shift, scale):
    y = x * lax.rsqrt(jnp.mean(x * x, axis=-1, keepdims=True) + NORM_EPS) * gain
    return y * (1 + scale) + shift


def _in_proj_kernel(idx_ref, x_ref, mod_ref, gain_ref, *refs):
    del idx_ref
    d = x_ref.shape[-1]
    n_out = len(refs) // 2
    mod = mod_ref[0]
    h = _rms_mod(x_ref[...], gain_ref[...], mod[:, 0:d], mod[:, d:2 * d]).astype(_BF)
    for w_ref, o_ref in zip(refs[:n_out], refs[n_out:]):
        o_ref[...] = jnp.dot(h, w_ref[...], preferred_element_type=_F32)


def fused_in_proj(x_tok, tile_mod, mod_tab, gain, w_groups):
    t, d = x_tok.shape
    full = lambda i, idx: (0, 0)
    grid_spec = pltpu.PrefetchScalarGridSpec(
        num_scalar_prefetch=1,
        grid=(t // PROJ_ROWS,),
        in_specs=[pl.BlockSpec((PROJ_ROWS, d), lambda i, idx: (i, 0)),
                  pl.BlockSpec((1, 1, mod_tab.shape[-1]), lambda i, idx: (idx[i], 0, 0)),
                  pl.BlockSpec((1, d), full)]
        + [pl.BlockSpec(w.shape, full) for w in w_groups],
        out_specs=[pl.BlockSpec((PROJ_ROWS, w.shape[1]), lambda i, idx: (i, 0)) for w in w_groups])
    return pl.pallas_call(
        _in_proj_kernel,
        grid_spec=grid_spec,
        out_shape=[jax.ShapeDtypeStruct((t, w.shape[1]), _F32) for w in w_groups],
        compiler_params=pltpu.CompilerParams(
            dimension_semantics=("parallel",), vmem_limit_bytes=VMEM_LIMIT_BIG),
        name="in_proj",
    )(tile_mod, x_tok, mod_tab, gain[None], *w_groups)


def _merge_kernel(idx_ref, x_ref, mod_ref, gain1_ref, gain2_ref, rwf_ref, rwb_ref, rwbonus_ref,
                  rwgate_ref, rwn_ref, yb_ref, yc_ref, yd_ref, wg_ref, wb_ref, wo_ref, xo_ref, f_ref):
    rwy_ref = (rwf_ref[...], rwb_ref[...], rwbonus_ref[...], rwgate_ref[...])
    del idx_ref
    d = x_ref.shape[-1]
    x = x_ref[...]
    mod = mod_ref[0]
    sh1, sc1, g1, sh2, sc2 = (mod[:, j * d:(j + 1) * d] for j in range(5))
    h = _rms_mod(x, gain1_ref[...], sh1, sc1).astype(_BF)
    y = rwy_ref[0] + rwy_ref[1]
    dev = y - _head_sums(y) * (1.0 / HEAD_DIM)
    var = _head_sums(dev * dev) * (1.0 / HEAD_DIM)
    yn = dev * lax.rsqrt(var + RW_GN_EPS) * rwn_ref[0:1, :] + rwn_ref[1:2, :]
    ys = [(yn + rwy_ref[2]) * rwy_ref[3], yb_ref[...], yc_ref[...], yd_ref[...]]
    m = None
    for i, y_i in enumerate(ys):
        gate = jax.nn.sigmoid(jnp.dot(h, wg_ref[:, i * d:(i + 1) * d], preferred_element_type=_F32))
        term = gate * jnp.dot(y_i.astype(_BF), wb_ref[i], preferred_element_type=_F32)
        m = term if m is None else m + term
    x_new = x + g1 * jnp.dot(m.astype(_BF), wo_ref[...], preferred_element_type=_F32)
    xo_ref[...] = x_new
    f_ref[...] = _rms_mod(x_new, gain2_ref[...], sh2, sc2).astype(_BF)


def fused_merge(x_tok, tile_mod, mod_tab, gain1, gain2, rw_parts, rw_norm, ys, w_gate, w_branch, w_out):
    t, d = x_tok.shape
    full2 = lambda i, idx: (0, 0)
    rows = lambda i, idx: (i, 0)
    branch = pl.BlockSpec((MERGE_ROWS, BRANCH_WIDTH), rows)
    grid_spec = pltpu.PrefetchScalarGridSpec(
        num_scalar_prefetch=1,
        grid=(t // MERGE_ROWS,),
        in_specs=[pl.BlockSpec((MERGE_ROWS, d), rows),
                  pl.BlockSpec((1, 1, mod_tab.shape[-1]), lambda i, idx: (idx[i], 0, 0)),
                  pl.BlockSpec((1, d), full2), pl.BlockSpec((1, d), full2)]
        + [branch] * 4 + [pl.BlockSpec(rw_norm.shape, full2)] + [branch] * (N_BRANCH - 1)
        + [pl.BlockSpec(w_gate.shape, full2),
           pl.BlockSpec(w_branch.shape, lambda i, idx: (0, 0, 0)),
           pl.BlockSpec(w_out.shape, full2)],
        out_specs=[pl.BlockSpec((MERGE_ROWS, d), rows), pl.BlockSpec((MERGE_ROWS, d), rows)])
    return pl.pallas_call(
        _merge_kernel,
        grid_spec=grid_spec,
        out_shape=[jax.ShapeDtypeStruct((t, d), _F32), jax.ShapeDtypeStruct((t, d), _BF)],
        compiler_params=pltpu.CompilerParams(
            dimension_semantics=("parallel",), vmem_limit_bytes=VMEM_LIMIT_BIG),
        name="merge_proj",
    )(tile_mod, x_tok, mod_tab, gain1[None], gain2[None], *rw_parts, rw_norm, *ys, w_gate, w_branch, w_out)


FFN_ROWS = 512
FFN_COLS = 512


def _swiglu_accumulate(x_ref, w1_ref, w3_ref, w2_ref, acc_ref):
    @pl.when(pl.program_id(1) == 0)
    def _():
        acc_ref[...] = jnp.zeros_like(acc_ref)

    x = x_ref[...]
    h1 = jnp.dot(x, w1_ref[0], preferred_element_type=_F32)
    h3 = jnp.dot(x, w3_ref[0], preferred_element_type=_F32)
    h = (h1 * jax.nn.sigmoid(h1) * h3).astype(_BF)
    acc_ref[...] += jnp.dot(h, w2_ref[0], preferred_element_type=_F32)


def pack_bf16_pairs(x):
    half = x.shape[-1] // 2
    bits = lax.bitcast_convert_type(x.astype(_F32), jnp.uint32)
    return bits[:, :half] | (bits[:, half:] >> 16)


def _moe_swiglu_kernel(be_ref, tok_ref, src_ref, w1_ref, w3_ref, w2_ref, g_ref, o_ref,
                       acc_ref, rows_ref, x_ref):
    del be_ref

    @pl.when(pl.program_id(1) == 0)
    def _():
        def gather(r, carry):
            rows_ref[pl.ds(r, 1), :] = src_ref[pl.ds(tok_ref[0, 0, r], 1), :]
            return carry

        lax.fori_loop(0, rows_ref.shape[0], gather, 0, unroll=8)
        packed = rows_ref[...]
        hi = lax.bitcast_convert_type(packed & jnp.uint32(0xFFFF0000), _F32)
        lo = lax.bitcast_convert_type(packed << 16, _F32)
        x_ref[...] = jnp.concatenate([hi, lo], axis=1).astype(_BF)

    _swiglu_accumulate(x_ref, w1_ref, w3_ref, w2_ref, acc_ref)

    @pl.when(pl.program_id(1) == pl.num_programs(1) - 1)
    def _():
        o_ref[...] = acc_ref[...] * g_ref[...]


def _ffn_residual_kernel(idx_ref, f_ref, x_ref, mod_ref, w1_ref, w3_ref, w2_ref, o_ref, acc_ref):
    del idx_ref
    _swiglu_accumulate(f_ref, w1_ref, w3_ref, w2_ref, acc_ref)

    @pl.when(pl.program_id(1) == pl.num_programs(1) - 1)
    def _():
        d = x_ref.shape[-1]
        o_ref[...] = x_ref[...] + mod_ref[0][:, 5 * d:6 * d] * acc_ref[...]


def ffn_residual(f_tok, x_tok, tile_mod, mod_tab, w1, w3, w2):
    t, d = x_tok.shape
    n_f = w1.shape[-1]
    rows = lambda i, f, idx: (i, 0)
    grid_spec = pltpu.PrefetchScalarGridSpec(
        num_scalar_prefetch=1,
        grid=(t // PROJ_ROWS, n_f // FFN_COLS),
        in_specs=[pl.BlockSpec((PROJ_ROWS, d), rows), pl.BlockSpec((PROJ_ROWS, d), rows),
                  pl.BlockSpec((1, 1, mod_tab.shape[-1]), lambda i, f, idx: (idx[i], 0, 0)),
                  pl.BlockSpec((1, d, FFN_COLS), lambda i, f, idx: (0, 0, f)),
                  pl.BlockSpec((1, d, FFN_COLS), lambda i, f, idx: (0, 0, f)),
                  pl.BlockSpec((1, FFN_COLS, d), lambda i, f, idx: (0, f, 0))],
        out_specs=pl.BlockSpec((PROJ_ROWS, d), rows),
        scratch_shapes=[pltpu.VMEM((PROJ_ROWS, d), _F32)])
    return pl.pallas_call(
        _ffn_residual_kernel,
        grid_spec=grid_spec,
        out_shape=jax.ShapeDtypeStruct((t, d), _F32),
        compiler_params=pltpu.CompilerParams(
            dimension_semantics=("parallel", "arbitrary"),
            vmem_limit_bytes=48 * 1024 * 1024),
        name="ffn_residual",
    )(tile_mod, f_tok, x_tok, mod_tab, w1.astype(_BF)[None], w3.astype(_BF)[None], w2.astype(_BF)[None])


def grouped_swiglu(h, tok, block_e, gate, w1, w3, w2):
    n_rows = tok.shape[0]
    d = h.shape[1]
    n_f = w1.shape[-1]
    assert n_rows % FFN_ROWS == 0 and n_f % FFN_COLS == 0
    src = pack_bf16_pairs(h)
    grid_spec = pltpu.PrefetchScalarGridSpec(
        num_scalar_prefetch=1,
        grid=(n_rows // FFN_ROWS, n_f // FFN_COLS),
        in_specs=[pl.BlockSpec((1, 1, FFN_ROWS), lambda i, f, be: (i, 0, 0), memory_space=pltpu.SMEM),
                  pl.BlockSpec(src.shape, lambda i, f, be: (0, 0), pipeline_mode=pl.Buffered(1)),
                  pl.BlockSpec((1, d, FFN_COLS), lambda i, f, be: (be[i], 0, f)),
                  pl.BlockSpec((1, d, FFN_COLS), lambda i, f, be: (be[i], 0, f)),
                  pl.BlockSpec((1, FFN_COLS, d), lambda i, f, be: (be[i], f, 0)),
                  pl.BlockSpec((FFN_ROWS, 1), lambda i, f, be: (i, 0))],
        out_specs=pl.BlockSpec((FFN_ROWS, d), lambda i, f, be: (i, 0)),
        scratch_shapes=[pltpu.VMEM((FFN_ROWS, d), _F32),
                        pltpu.VMEM((FFN_ROWS, d // 2), jnp.uint32),
                        pltpu.VMEM((FFN_ROWS, d), _BF)])
    return pl.pallas_call(
        _moe_swiglu_kernel,
        grid_spec=grid_spec,
        out_shape=jax.ShapeDtypeStruct((n_rows, d), _F32),
        compiler_params=pltpu.CompilerParams(
            dimension_semantics=("arbitrary", "arbitrary"), vmem_limit_bytes=VMEM_LIMIT_BIG),
        name="grouped_swiglu",
    )(block_e, tok.reshape(n_rows // FFN_ROWS, 1, FFN_ROWS), src,
      w1.astype(_BF), w3.astype(_BF), w2.astype(_BF), gate)


def moe_swiglu(h, w_router, w1, w3, w2):
    n_tok, d = h.shape
    logits = jnp.dot(h, w_router.astype(_BF), preferred_element_type=jnp.float32)
    top_logit, top_idx = lax.top_k(logits, TOP_K)
    gates = jax.nn.softmax(top_logit, axis=-1)
    n_assign = n_tok * TOP_K
    flat_e = top_idx.reshape(-1)
    order = jnp.argsort(flat_e)
    sorted_e = flat_e[order]
    sorted_tok = (order // TOP_K).astype(jnp.int32)
    sorted_gate = gates.reshape(-1)[order]
    counts = jnp.bincount(flat_e, length=N_EXPERTS)
    padded = (counts + FFN_ROWS - 1) // FFN_ROWS * FFN_ROWS
    start = jnp.cumsum(counts) - counts
    pend = jnp.cumsum(padded)
    pstart = pend - padded
    dest = pstart[sorted_e] + jnp.arange(n_assign) - start[sorted_e]
    n_blocks = -(-n_assign // FFN_ROWS) + N_EXPERTS
    n_rows = n_blocks * FFN_ROWS
    tok = jnp.zeros((n_rows,), jnp.int32).at[dest].set(sorted_tok)
    gate = jnp.zeros((n_rows,), _F32).at[dest].set(sorted_gate)
    block_e = jnp.minimum(jnp.searchsorted(pend, jnp.arange(n_blocks) * FFN_ROWS, side='right'),
                          N_EXPERTS - 1).astype(jnp.int32)
    yb = grouped_swiglu(h, tok, block_e, gate[:, None], w1, w3, w2)
    return jnp.zeros((n_tok, d), _F32).at[tok].add(yb)


def kernel(x, c, ctx, c_ctx, w_mod, b_mod, norm1, norm2, w_in, w_branch, w_out,
           rw_mu, rw_w0, rw_w1, rw_w2, rw_a0, rw_a1, rw_a2, rw_v0, rw_v1, rw_v2,
           rw_g1, rw_g2, rw_k_k, rw_k_a, rw_r_k, rw_ln_w, rw_ln_b, sw_sink,
           mla_q_norm, mla_w_uq, mla_kv_norm, mla_w_ukv, ga_q_norm, ga_k_norm,
           ffn_w1, ffn_w3, ffn_w2, moe_router, moe_w1, moe_w3, moe_w2, final_norm):
    b, s_len, d = x.shape
    n_ctx_tok = ctx.shape[1]
    n_c = b * n_ctx_tok
    assert n_c % PROJ_ROWS == 0 and s_len % PROJ_ROWS == 0
    rope64 = axial_rope_tables(s_len, HEAD_DIM)
    rope32 = axial_rope_tables(s_len, MLA_ROPE)
    silu_c = jax.nn.silu(c)
    silu_cc = jax.nn.silu(c_ctx)
    x_tok = jnp.concatenate([ctx.reshape(n_c, d), x.reshape(b * s_len, d)], axis=0)
    seg_rows = [n_c] + [s_len] * b

    def tile_table(rows):
        return jnp.concatenate([jnp.full((n // rows,), i, jnp.int32) for i, n in enumerate(seg_rows)])

    tile_mod_proj, tile_mod_merge = tile_table(PROJ_ROWS), tile_table(MERGE_ROWS)

    def split_tokens(t):
        return t[:n_c].reshape(b, n_ctx_tok, -1), t[n_c:].reshape(b, s_len, -1)

    vf_tok = None
    for l in range(DEPTH):
        ctx_out = l < DEPTH - 1
        mod_l = silu_c @ w_mod[l] + b_mod[l]
        mod_c = silu_cc @ w_mod[l] + b_mod[l]
        mod_tab = jnp.concatenate([mod_c[None], mod_l], axis=0)[:, None, :]

        w_groups = []
        for off, width in PROJ_GROUPS:
            w = w_in[l][:, off:off + width].astype(_BF)
            w_groups.append(jnp.pad(w, ((0, 0), (0, _round_up(width, LANES) - width))))
        p_rw, p_sw, p_mla, p_ga = fused_in_proj(x_tok, tile_mod_proj, mod_tab, norm1[l], w_groups)
        pc, pl_ = [None], [None]
        for p, widths in ((p_sw, IN_WIDTHS[1:4]), (p_mla, IN_WIDTHS[4:7]), (p_ga, IN_WIDTHS[7:10])):
            p_ctx, p_lat = split_tokens(p)
            pc += split_cols(p_ctx, widths)
            pl_ += split_cols(p_lat, widths)
        rw = dict(mu=rw_mu[l], w0=rw_w0[l], w1=rw_w1[l], w2=rw_w2[l],
                  a0=rw_a0[l], a1=rw_a1[l], a2=rw_a2[l],
                  v0=rw_v0[l - 1] if l > 0 else None,
                  v1=rw_v1[l - 1] if l > 0 else None,
                  v2=rw_v2[l - 1] if l > 0 else None,
                  g1=rw_g1[l], g2=rw_g2[l], k_k=rw_k_k[l], k_a=rw_k_a[l], r_k=rw_r_k[l],
                  ln_w=rw_ln_w[l], ln_b=rw_ln_b[l])
        rw_yf, rw_yb, rw_bonus, rw_gate, rw_v = rwkv_branch(p_rw, vf_tok, rw, b, n_ctx_tok, s_len)
        if l == 0:
            vf_tok = rw_v
        yb_c, yb_l = window_mixer(pc[1:4], pl_[1:4], sw_sink[l], rope64, ctx_out)
        yc_c, yc_l = mla_mixer(pc[4:7], pl_[4:7], mla_q_norm[l], mla_w_uq[l], mla_kv_norm[l],
                               mla_w_ukv[l], rope32, ctx_out)
        yd_c, yd_l = grid_attention_mixer(pc[7:10], pl_[7:10], ga_q_norm[l], ga_k_norm[l], rope64, ctx_out)

        def tokens_of(y_ctx, y_lat):
            y_ctx = jnp.zeros((n_c, BRANCH_WIDTH), _F32) if y_ctx is None else y_ctx.reshape(n_c, BRANCH_WIDTH)
            return jnp.concatenate([y_ctx, y_lat.reshape(b * s_len, BRANCH_WIDTH)], axis=0)

        ys = [tokens_of(yc_, yl_) for yc_, yl_ in ((yb_c, yb_l), (yc_c, yc_l), (yd_c, yd_l))]
        x_tok, f_tok = fused_merge(x_tok, tile_mod_merge, mod_tab, norm1[l], norm2[l],
                                   (rw_yf, rw_yb, rw_bonus, rw_gate),
                                   jnp.stack([rw_ln_w[l], rw_ln_b[l]]), ys,
                                   w_in[l][:, MIXER_COLS:].astype(_BF), w_branch[l].astype(_BF),
                                   w_out[l].astype(_BF))
        if l % 2 == 0:
            x_tok = ffn_residual(f_tok, x_tok, tile_mod_proj, mod_tab,
                                 ffn_w1[l // 2], ffn_w3[l // 2], ffn_w2[l // 2])
        else:
            first = 0 if ctx_out else n_c
            ffn_out = moe_swiglu(f_tok[first:], moe_router[l // 2], moe_w1[l // 2], moe_w3[l // 2],
                                 moe_w2[l // 2])
            pieces, row = [], 0
            for i, n in enumerate(seg_rows):
                if row >= first:
                    g2 = mod_tab[i, :, 5 * d:]
                    pieces.append(x_tok[row:row + n] + g2 * ffn_out[row - first:row - first + n])
                else:
                    pieces.append(x_tok[row:row + n])
                row += n
            x_tok = jnp.concatenate(pieces, axis=0)
    return rms_norm(x_tok[n_c:].reshape(b, s_len, d), final_norm)
```

```python
import functools

import jax
import jax.numpy as jnp
from jax import lax
from jax.experimental import pallas as pl
from jax.experimental.pallas import tpu as pltpu

D_MODEL = 1024
DEPTH = 4
GRID_W = 64
HEAD_DIM = 64
BLOCK = 128
WINDOW = 128
ROPE_BASE = 10000.0
NORM_EPS = 1e-6
NEG_INF = -1e30
N_BRANCH = 4
BRANCH_WIDTH = 256
RW_HEADS = 4
RW_WIDTH = RW_HEADS * HEAD_DIM
RW_GN_EPS = 64e-5
SW_HEADS = 4
SW_KV_HEADS = 2
MLA_HEADS = 4
MLA_NOPE = 64
MLA_ROPE = 32
MLA_V = 64
MLA_Q_RANK = 256
MLA_KV_RANK = 128
GA_HEADS = 4
GA_KV_HEADS = 2
N_EXPERTS = 8
TOP_K = 2
MOE_BLOCK = 128
IN_WIDTHS = (4 * RW_WIDTH,
             SW_HEADS * HEAD_DIM, SW_KV_HEADS * HEAD_DIM, SW_KV_HEADS * HEAD_DIM,
             MLA_Q_RANK, MLA_KV_RANK, MLA_ROPE,
             GA_HEADS * HEAD_DIM, GA_KV_HEADS * HEAD_DIM, GA_KV_HEADS * HEAD_DIM,
             N_BRANCH * D_MODEL)


def _mm_kernel(a_ref, b_ref, o_ref):
    o_ref[...] = jnp.dot(a_ref[...], b_ref[...], preferred_element_type=jnp.float32)


def _round_up(x, m):
    return (x + m - 1) // m * m


def pmm(a, b, tm=512, tn=512):
    m, k = a.shape
    _, n = b.shape
    tm = min(tm, _round_up(m, 8))
    tn = min(tn, _round_up(n, 128))
    mp, np_ = _round_up(m, tm), _round_up(n, tn)
    a = a.astype(jnp.bfloat16)
    b = b.astype(jnp.bfloat16)
    if mp != m:
        a = jnp.pad(a, ((0, mp - m), (0, 0)))
    if np_ != n:
        b = jnp.pad(b, ((0, 0), (0, np_ - n)))
    out = pl.pallas_call(
        _mm_kernel,
        grid=(np_ // tn, mp // tm),
        in_specs=[pl.BlockSpec((tm, k), lambda j, i: (i, 0)),
                  pl.BlockSpec((k, tn), lambda j, i: (0, j))],
        out_specs=pl.BlockSpec((tm, tn), lambda j, i: (i, j)),
        out_shape=jax.ShapeDtypeStruct((mp, np_), jnp.float32),
        compiler_params=pltpu.CompilerParams(
            dimension_semantics=("arbitrary", "arbitrary"),
            vmem_limit_bytes=48 * 1024 * 1024),
        name="pmm",
    )(a, b)
    return out[:m, :n]


def pmm_nd(a, b, **kw):
    lead = a.shape[:-1]
    return pmm(a.reshape(-1, a.shape[-1]), b, **kw).reshape(lead + (b.shape[-1],))


RW_CHUNK = 64
RW_CHUNKS_PER_STEP = 4
RW_BLOCK = RW_CHUNK * RW_CHUNKS_PER_STEP
RW_INV_BASE = 4

_BF = jnp.bfloat16
_F32 = jnp.float32


def _bd_rows(x):
    lane_head = lax.broadcasted_iota(jnp.int32, x.shape, 1) // HEAD_DIM
    return jnp.concatenate([jnp.where(lane_head == h, x, 0.0) for h in range(RW_HEADS)], axis=0)


def _dot(a, b):
    return jnp.dot(a.astype(_BF), b.astype(_BF), preferred_element_type=_F32)


def _dot_nt(a, b):
    return lax.dot_general(a.astype(_BF), b.astype(_BF), (((1,), (1,)), ((), ())),
                           preferred_element_type=_F32)


def _dot_tn(a, b):
    return lax.dot_general(a.astype(_BF), b.astype(_BF), (((0,), (0,)), ((), ())),
                           preferred_element_type=_F32)


def _split3(x):
    h1 = x.astype(_BF)
    r1 = x - h1.astype(_F32)
    h2 = r1.astype(_BF)
    h3 = (r1 - h2.astype(_F32)).astype(_BF)
    return h1, h2, h3


def _each(fn, *lists):
    return [fn(*args) for args in zip(*lists)]


def _chunk_summaries(r, lw, k, v, a, b, reverse):
    c = RW_CHUNK
    row = lax.broadcasted_iota(jnp.int32, (c, c), 0)
    col = lax.broadcasted_iota(jnp.int32, (c, c), 1)
    tri = ((row <= col) if reverse else (row >= col)).astype(_BF)
    cum = _each(lambda x: sum(jnp.dot(tri, p, preferred_element_type=_F32) for p in _split3(x)), lw)
    lwlast = _each(lambda x: x[0:1, :] if reverse else x[c - 1:c, :], cum)
    rt = _each(lambda x, cu: x * jnp.exp(cu), r, cum)
    at = _each(lambda x, cu, l: x * jnp.exp(cu - l), a, cum, lw)
    w_inv = _each(lambda cu: jnp.exp(-cu), cum)
    w_tail = _each(lambda cu, ll: jnp.exp(ll - cu), cum, lwlast)
    wlast = _each(jnp.exp, lwlast)
    bt = _each(jnp.multiply, b, w_inv)
    kt = _each(jnp.multiply, k, w_inv)
    bh = _each(jnp.multiply, b, w_tail)
    kh = _each(jnp.multiply, k, w_tail)

    t_idx = lax.broadcasted_iota(jnp.int32, (c, RW_WIDTH), 0)
    s_idx = lax.broadcasted_iota(jnp.int32, (c, RW_WIDTH), 1) % HEAD_DIM
    strict = (s_idx > t_idx) if reverse else (s_idx < t_idx)
    incl = (s_idx >= t_idx) if reverse else (s_idx <= t_idx)

    l1 = _each(lambda x, y: jnp.concatenate([x, y], axis=0), at, rt)
    g_b = _each(lambda x, y: _dot_nt(x, _bd_rows(y)), l1, bt)
    g_k = _each(lambda x, y: _dot_nt(x, _bd_rows(y)), l1, kt)
    a_ab = _each(lambda g: jnp.where(strict, g[:c], 0.0), g_b)
    a_rb = _each(lambda g: jnp.where(incl, g[c:], 0.0), g_b)
    a_ak = _each(lambda g: jnp.where(strict, g[:c], 0.0), g_k)
    a_rk = _each(lambda g: jnp.where(incl, g[c:], 0.0), g_k)

    eye = jnp.where(s_idx == t_idx, 1.0, 0.0)
    base = RW_INV_BASE
    n_b = _each(lambda x: jnp.where(t_idx // base == s_idx // base, x, 0.0), a_ab)
    n_b2 = _each(lambda x: _dot(x, _bd_rows(x)), n_b)
    n_b3 = _each(lambda x, y: _dot(x, _bd_rows(y)), n_b, n_b2)
    tm = _each(lambda x, y, z: eye + x + y + z, n_b, n_b2, n_b3)
    blk = base
    while blk < c:
        off = (t_idx // (2 * blk) == s_idx // (2 * blk)) & (t_idx // blk != s_idx // blk)
        x = _each(lambda t, n: _dot(t, _bd_rows(jnp.where(off, n, 0.0))), tm, a_ab)
        tm = _each(lambda t, xx: t + _dot(xx, _bd_rows(t)), tm, x)
        blk *= 2

    av = _each(lambda x, y, vv: _dot(jnp.concatenate([x, y], axis=0), _bd_rows(vv)), a_ak, a_rk, v)
    ta = _each(lambda t, x: _dot(t, _bd_rows(x)), tm, at)
    u0 = _each(lambda t, x: _dot(t, _bd_rows(x[:c])), tm, av)
    r1 = _each(lambda x, ar, t: x + _dot(ar, _bd_rows(t)), rt, a_rb, ta)
    y0 = _each(lambda x, ar, u: x[c:] + _dot(ar, _bd_rows(u)), av, a_rb, u0)

    blk_r = lax.broadcasted_iota(jnp.int32, (RW_WIDTH, RW_WIDTH), 0) // HEAD_DIM
    blk_c = lax.broadcasted_iota(jnp.int32, (RW_WIDTH, RW_WIDTH), 1) // HEAD_DIM
    m_bd = _each(lambda t, x: jnp.where(blk_r == blk_c, _dot_tn(t, x), 0.0), ta, bh)
    z = _each(lambda u, vv, x, y: _dot_tn(jnp.concatenate([u, vv], axis=0),
                                          jnp.concatenate([x, y], axis=0)), u0, v, bh, kh)
    lane_head = lax.broadcasted_iota(jnp.int32, (HEAD_DIM, RW_WIDTH), 1) // HEAD_DIM

    def diag_blocks(zz):
        out = zz[:HEAD_DIM]
        for h in range(1, RW_HEADS):
            out = jnp.where(lane_head == h, zz[h * HEAD_DIM:(h + 1) * HEAD_DIM], out)
        return out

    sadd = _each(diag_blocks, z)
    return r1, y0, m_bd, sadd, wlast


def _rw_summary_kernel(r_ref, lw_ref, k_ref, v_ref, a_ref, b_ref,
                       r1_ref, y0_ref, m_ref, sadd_ref, wl_ref, *, reverse):
    c = RW_CHUNK

    def chunks(ref):
        return [ref[pl.ds(j * c, c), :] for j in range(RW_CHUNKS_PER_STEP)]

    r1, y0, m_bd, sadd, wlast = _chunk_summaries(
        chunks(r_ref), chunks(lw_ref), chunks(k_ref), chunks(v_ref), chunks(a_ref), chunks(b_ref),
        reverse)
    r1_ref[...] = jnp.concatenate(r1, axis=0)
    y0_ref[...] = jnp.concatenate(y0, axis=0)
    m_ref[...] = jnp.stack(m_bd, axis=0).astype(_BF)
    sadd_ref[...] = jnp.stack(sadd, axis=0)
    wl_ref[...] = jnp.stack(wlast, axis=0)


def rw_summaries(r_src, lw, k, v, a, b, reverse):
    t, w = lw.shape
    c, cps = RW_CHUNK, RW_CHUNKS_PER_STEP
    assert t % RW_BLOCK == 0 and w == RW_WIDTH
    rows = pl.BlockSpec((RW_BLOCK, w), lambda i: (i, 0))
    return pl.pallas_call(
        functools.partial(_rw_summary_kernel, reverse=reverse),
        grid=(t // RW_BLOCK,),
        in_specs=[rows] * 6,
        out_specs=[rows, rows,
                   pl.BlockSpec((cps, w, w), lambda i: (i, 0, 0)),
                   pl.BlockSpec((cps, HEAD_DIM, w), lambda i: (i, 0, 0)),
                   pl.BlockSpec((cps, 1, w), lambda i: (i, 0, 0))],
        out_shape=[jax.ShapeDtypeStruct((t, w), _F32),
                   jax.ShapeDtypeStruct((t, w), _F32),
                   jax.ShapeDtypeStruct((t // c, w, w), _BF),
                   jax.ShapeDtypeStruct((t // c, HEAD_DIM, w), _F32),
                   jax.ShapeDtypeStruct((t // c, 1, w), _F32)],
        compiler_params=pltpu.CompilerParams(
            dimension_semantics=("parallel",), vmem_limit_bytes=48 * 1024 * 1024),
        name="rw_chunk_summary",
    )(r_src, lw, k, v, a, b)


def _rw_state_kernel(*refs):
    s_ref = refs[-1]
    y_refs = refs[10:12]
    c, cps = RW_CHUNK, RW_CHUNKS_PER_STEP

    @pl.when(pl.program_id(1) == 0)
    def _():
        s_ref[...] = jnp.zeros_like(s_ref)

    s = [s_ref[0], s_ref[1]]
    ys = [[None] * cps, [None] * cps]
    for step in range(cps):
        js = (step, cps - 1 - step)
        r1, y0 = ([refs[5 * d + n][pl.ds(js[d] * c, c), :] for d in range(2)] for n in range(2))
        m_bd, sadd, wlast = ([refs[5 * d + n][js[d]] for d in range(2)] for n in range(2, 5))
        y = _each(lambda rr, ss, yy: _dot_nt(rr, _bd_rows(ss)) + yy, r1, s, y0)
        s = _each(lambda ss, ww, mm, aa: ss * ww + _dot(ss, mm) + aa, s, wlast, m_bd, sadd)
        for d in range(2):
            ys[d][js[d]] = y[d]
    for d in range(2):
        y_refs[d][...] = jnp.concatenate(ys[d], axis=0)
        s_ref[d] = s[d]


def rw_state_pass(sum_fwd, sum_bwd, n_batch, n_ctx, n_lat):
    t, w = sum_fwd[0].shape
    cps = RW_CHUNKS_PER_STEP
    assert n_ctx % RW_BLOCK == 0 and n_lat % RW_BLOCK == 0
    n_cb, n_lb = n_ctx // RW_BLOCK, n_lat // RW_BLOCK
    lat0 = n_batch * n_cb

    def fwd_blk(q, i):
        return jnp.where(i < n_cb, q * n_cb + i, lat0 + q * n_lb + (i - n_cb))

    def bwd_blk(q, i):
        return jnp.where(i < n_cb, q * n_cb + (n_cb - 1 - i), lat0 + q * n_lb + (n_lb - 1 - (i - n_cb)))

    def specs(blk):
        return [pl.BlockSpec((RW_BLOCK, w), lambda q, i: (blk(q, i), 0)),
                pl.BlockSpec((RW_BLOCK, w), lambda q, i: (blk(q, i), 0)),
                pl.BlockSpec((cps, w, w), lambda q, i: (blk(q, i), 0, 0)),
                pl.BlockSpec((cps, HEAD_DIM, w), lambda q, i: (blk(q, i), 0, 0)),
                pl.BlockSpec((cps, 1, w), lambda q, i: (blk(q, i), 0, 0))]

    return pl.pallas_call(
        _rw_state_kernel,
        grid=(n_batch, n_cb + n_lb),
        in_specs=specs(fwd_blk) + specs(bwd_blk),
        out_specs=[pl.BlockSpec((RW_BLOCK, w), lambda q, i: (fwd_blk(q, i), 0)),
                   pl.BlockSpec((RW_BLOCK, w), lambda q, i: (bwd_blk(q, i), 0))],
        out_shape=[jax.ShapeDtypeStruct((t, w), _F32)] * 2,
        scratch_shapes=[pltpu.VMEM((2, HEAD_DIM, w), _F32)],
        compiler_params=pltpu.CompilerParams(
            dimension_semantics=("arbitrary", "arbitrary"),
            vmem_limit_bytes=48 * 1024 * 1024),
        name="rw_state_pass",
    )(*sum_fwd, *sum_bwd)


ATTN_TQ = 256
ATTN_TK = 768


LANES = 128
LOG2_E = 1.4426950408889634


def _dense_attn_kernel(q_ref, k_ref, v_ref, o_ref, *, tk):
    g, tq, dq = q_ref.shape[2:]
    dv = o_ref.shape[-1]
    n_keys = k_ref.shape[2]
    q = q_ref[0, 0].reshape(g * tq, dq)

    def scores(i):
        kc = k_ref[0, 0, pl.ds(i * tk, tk), :]
        return lax.dot_general(q, kc, (((1,), (1,)), ((), ())), preferred_element_type=_F32)

    def update(i, m, acc, s):
        vc = v_ref[0, 0, pl.ds(i * tk, tk), :]
        m_new = jnp.maximum(m, jnp.max(s, axis=-1, keepdims=True))
        p = jnp.exp2(s - m_new).astype(_BF)
        return m_new, jnp.exp2(m - m_new) * acc + jnp.dot(p, vc, preferred_element_type=_F32)

    n_chunks = n_keys // tk
    m = jnp.full((g * tq, 1), -jnp.inf, _F32)
    acc = jnp.zeros((g * tq, v_ref.shape[-1]), _F32)
    s = scores(0)
    for i in range(n_chunks):
        s_next = scores(i + 1) if i + 1 < n_chunks else None
        m, acc = update(i, m, acc, s)
        s = s_next
    o_ref[0, 0] = (acc[:, :dv] / acc[:, dv:dv + 1]).reshape(g, tq, dv)


def dense_block_attention(q, k, v, scale):
    b, s_len, hk, g, dq = q.shape
    n_keys, dv = k.shape[1], v.shape[-1]
    tq = min(ATTN_TQ, s_len)
    tk = ATTN_TK if n_keys % ATTN_TK == 0 else n_keys
    assert s_len % tq == 0
    qt = jnp.transpose(q * (scale * LOG2_E), (0, 2, 3, 1, 4)).astype(_BF)
    kt = jnp.transpose(k, (0, 2, 1, 3)).astype(_BF)
    vt = jnp.transpose(v, (0, 2, 1, 3))
    vt = jnp.concatenate([vt, jnp.ones_like(vt[..., :1]),
                          jnp.zeros(vt.shape[:-1] + (LANES - dv - 1,), vt.dtype)], axis=-1).astype(_BF)
    o = pl.pallas_call(
        functools.partial(_dense_attn_kernel, tk=tk),
        grid=(b, hk, s_len // tq),
        in_specs=[pl.BlockSpec((1, 1, g, tq, dq), lambda bi, h, i: (bi, h, 0, i, 0)),
                  pl.BlockSpec((1, 1, n_keys, dq), lambda bi, h, i: (bi, h, 0, 0)),
                  pl.BlockSpec((1, 1, n_keys, LANES), lambda bi, h, i: (bi, h, 0, 0))],
        out_specs=pl.BlockSpec((1, 1, g, tq, dv), lambda bi, h, i: (bi, h, 0, i, 0)),
        out_shape=jax.ShapeDtypeStruct((b, hk, g, s_len, dv), _F32),
        compiler_params=pltpu.CompilerParams(
            dimension_semantics=("parallel", "parallel", "parallel"),
            vmem_limit_bytes=48 * 1024 * 1024),
        name="dense_attention",
    )(qt, kt, vt)
    return jnp.transpose(o, (0, 3, 1, 2, 4)).reshape(b, s_len, hk * g * dv)


def rms_norm(x, g):
    xf = x.astype(jnp.float32)
    y = xf * lax.rsqrt(jnp.mean(xf * xf, axis=-1, keepdims=True) + NORM_EPS)
    return (y * g.astype(jnp.float32)).astype(x.dtype)


def split_cols(p, widths):
    out, off = [], 0
    for w in widths:
        out.append(p[..., off:off + w])
        off += w
    return out


def axial_rope_tables(n_tokens, rot_dim):
    rows = n_tokens // GRID_W
    row = jnp.repeat(jnp.arange(rows, dtype=jnp.float32), GRID_W)
    col = jnp.tile(jnp.arange(GRID_W, dtype=jnp.float32), rows)
    n_freq = rot_dim // 4
    inv_freq = ROPE_BASE ** (-jnp.arange(n_freq, dtype=jnp.float32) / n_freq)
    ang = jnp.concatenate([row[:, None] * inv_freq, col[:, None] * inv_freq], axis=-1)
    return jnp.cos(ang), jnp.sin(ang)


def apply_rope(x, cos, sin):
    half = x.shape[-1] // 2
    c = cos[None, :, None, :].astype(x.dtype)
    s = sin[None, :, None, :].astype(x.dtype)
    x1, x2 = x[..., :half], x[..., half:]
    return jnp.concatenate([x1 * c - x2 * s, x1 * s + x2 * c], axis=-1)


def context_attention(q, k, v, scale, sink=None):
    b, n = q.shape[:2]
    s = jnp.einsum('bqhgd,bkhd->bhgqk', q, k).astype(jnp.float32) * scale
    if sink is not None:
        s_sink = jnp.broadcast_to(sink[None, :, :, None, None].astype(jnp.float32), s.shape[:-1] + (1,))
        p = jax.nn.softmax(jnp.concatenate([s_sink, s], axis=-1), axis=-1)[..., 1:]
    else:
        p = jax.nn.softmax(s, axis=-1)
    o = jnp.einsum('bhgqk,bkhd->bqhgd', p.astype(v.dtype), v)
    return o.reshape(b, n, -1)


WIN_TQ = 256
WIN_SPAN = WIN_TQ + 2 * WINDOW


def _window_attn_kernel(sink_ref, q_ref, k_ref, v_ref, kc_ref, vc_ref, o_ref):
    g, tq, _ = q_ref.shape[2:]
    dv = o_ref.shape[-1]
    n_lat = k_ref.shape[2]
    h, i = pl.program_id(1), pl.program_id(2)
    q = q_ref[0, 0].reshape(g * tq, q_ref.shape[-1])
    start = pl.multiple_of(jnp.clip(i * tq - WINDOW, 0, n_lat - WIN_SPAN), WINDOW)
    kw = k_ref[0, 0, pl.ds(start, WIN_SPAN), :]
    vw = v_ref[0, 0, pl.ds(start, WIN_SPAN), :]
    nt = (((1,), (1,)), ((), ()))
    s_win = lax.dot_general(q, kw, nt, preferred_element_type=_F32)
    s_ctx = lax.dot_general(q, kc_ref[0, 0], nt, preferred_element_type=_F32)
    row = lax.broadcasted_iota(jnp.int32, s_win.shape, 0)
    q_pos = i * tq + row % tq
    k_pos = start + lax.broadcasted_iota(jnp.int32, s_win.shape, 1)
    s_win = jnp.where(jnp.abs(q_pos - k_pos) <= WINDOW, s_win, NEG_INF)
    grp = lax.broadcasted_iota(jnp.int32, (g * tq, 1), 0) // tq
    sink = jnp.zeros((g * tq, 1), _F32)
    for j in range(g):
        sink = jnp.where(grp == j, sink_ref[h * g + j], sink)
    m = jnp.maximum(jnp.maximum(jnp.max(s_win, axis=-1, keepdims=True),
                                jnp.max(s_ctx, axis=-1, keepdims=True)), sink)
    acc = (jnp.dot(jnp.exp2(s_win - m).astype(_BF), vw, preferred_element_type=_F32)
           + jnp.dot(jnp.exp2(s_ctx - m).astype(_BF), vc_ref[0, 0], preferred_element_type=_F32))
    total = acc[:, dv:dv + 1] + jnp.exp2(sink - m)
    o_ref[0, 0] = (acc[:, :dv] / total).reshape(g, tq, dv)


def _heads_first(t, scale=None):
    t = t if scale is None else t * scale
    return jnp.moveaxis(t, 1, -2).astype(_BF)


def _with_ones_column(v):
    dv = v.shape[-1]
    return jnp.concatenate([v, jnp.ones_like(v[..., :1]),
                            jnp.zeros(v.shape[:-1] + (LANES - dv - 1,), v.dtype)], axis=-1).astype(_BF)


def banded_window_attention(q, k, v, k_ctx, v_ctx, sink, scale):
    b, s_len, hk, g, d = q.shape
    n_ctx = k_ctx.shape[1]
    assert s_len % WIN_TQ == 0 and s_len >= WIN_SPAN
    qt = _heads_first(q, scale * LOG2_E)
    kt, kct = _heads_first(k), _heads_first(k_ctx)
    vt, vct = _with_ones_column(jnp.moveaxis(v, 1, -2)), _with_ones_column(jnp.moveaxis(v_ctx, 1, -2))
    kv_map = lambda bi, h, i, sk: (bi, h, 0, 0)
    grid_spec = pltpu.PrefetchScalarGridSpec(
        num_scalar_prefetch=1,
        grid=(b, hk, s_len // WIN_TQ),
        in_specs=[pl.BlockSpec((1, 1, g, WIN_TQ, d), lambda bi, h, i, sk: (bi, h, 0, i, 0)),
                  pl.BlockSpec((1, 1, s_len, d), kv_map),
                  pl.BlockSpec((1, 1, s_len, LANES), kv_map),
                  pl.BlockSpec((1, 1, n_ctx, d), kv_map),
                  pl.BlockSpec((1, 1, n_ctx, LANES), kv_map)],
        out_specs=pl.BlockSpec((1, 1, g, WIN_TQ, d), lambda bi, h, i, sk: (bi, h, 0, i, 0)))
    o = pl.pallas_call(
        _window_attn_kernel,
        grid_spec=grid_spec,
        out_shape=jax.ShapeDtypeStruct((b, hk, g, s_len, d), _F32),
        compiler_params=pltpu.CompilerParams(
            dimension_semantics=("parallel", "parallel", "parallel"),
            vmem_limit_bytes=48 * 1024 * 1024),
        name="window_attention",
    )((sink.astype(_F32) * LOG2_E).reshape(-1), qt, kt, vt, kct, vct)
    return jnp.transpose(o, (0, 3, 1, 2, 4)).reshape(b, s_len, hk * g * d)


def gqa_heads(q, k, v, n_heads, n_kv, rope=None, q_gain=None, k_gain=None):
    b, n, _ = q.shape
    q = q.reshape(b, n, n_heads, HEAD_DIM)
    k = k.reshape(b, n, n_kv, HEAD_DIM)
    v = v.reshape(b, n, n_kv, HEAD_DIM)
    if q_gain is not None:
        q = rms_norm(q, q_gain)
        k = rms_norm(k, k_gain)
    if rope is not None:
        q = apply_rope(q, rope[0], rope[1])
        k = apply_rope(k, rope[0], rope[1])
    return q.reshape(b, n, n_kv, n_heads // n_kv, HEAD_DIM), k, v


def window_mixer(pc, pl_, sink, rope, ctx_out):
    scale = HEAD_DIM ** -0.5
    qc, kc, vc = gqa_heads(pc[0], pc[1], pc[2], SW_HEADS, SW_KV_HEADS)
    ql, kl, vl = gqa_heads(pl_[0], pl_[1], pl_[2], SW_HEADS, SW_KV_HEADS, rope)
    sink = sink.reshape(SW_KV_HEADS, SW_HEADS // SW_KV_HEADS)
    y_lat = banded_window_attention(ql, kl, vl, kc, vc, sink, scale)
    y_ctx = context_attention(qc, kc, vc, scale, sink) if ctx_out else None
    return y_ctx, y_lat


def grid_attention_mixer(pc, pl_, q_gain, k_gain, rope, ctx_out):
    scale = HEAD_DIM ** -0.5
    qc, kc, vc = gqa_heads(pc[0], pc[1], pc[2], GA_HEADS, GA_KV_HEADS, None, q_gain, k_gain)
    ql, kl, vl = gqa_heads(pl_[0], pl_[1], pl_[2], GA_HEADS, GA_KV_HEADS, rope, q_gain, k_gain)
    y_lat = dense_block_attention(ql, jnp.concatenate([kc, kl], axis=1), jnp.concatenate([vc, vl], axis=1), scale)
    y_ctx = context_attention(qc, kc, vc, scale) if ctx_out else None
    return y_ctx, y_lat


def mla_mixer(pc, pl_, q_gain, w_uq, kv_gain, w_ukv, rope, ctx_out):
    scale = (MLA_NOPE + MLA_ROPE) ** -0.5

    def project(cq, ckv, k_rope, rope_tab):
        b, n, _ = cq.shape
        q = (rms_norm(cq, q_gain) @ w_uq).reshape(b, n, MLA_HEADS, MLA_NOPE + MLA_ROPE)
        kv = (rms_norm(ckv, kv_gain) @ w_ukv).reshape(b, n, MLA_HEADS, MLA_NOPE + MLA_V)
        q_nope, q_pe = q[..., :MLA_NOPE], q[..., MLA_NOPE:]
        k_nope, v = kv[..., :MLA_NOPE], kv[..., MLA_NOPE:]
        k_pe = k_rope[:, :, None, :]
        if rope_tab is not None:
            q_pe = apply_rope(q_pe, rope_tab[0], rope_tab[1])
            k_pe = apply_rope(k_pe, rope_tab[0], rope_tab[1])
        q = jnp.concatenate([q_nope, q_pe], axis=-1)
        k = jnp.concatenate([k_nope, jnp.broadcast_to(k_pe, (b, n, MLA_HEADS, MLA_ROPE))], axis=-1)
        return q[:, :, :, None, :], k, v

    qc, kc, vc = project(pc[0], pc[1], pc[2], None)
    ql, kl, vl = project(pl_[0], pl_[1], pl_[2], rope)
    y_lat = dense_block_attention(ql, jnp.concatenate([kc, kl], axis=1), jnp.concatenate([vc, vl], axis=1), scale)
    y_ctx = context_attention(qc, kc, vc, scale) if ctx_out else None
    return y_ctx, y_lat


def centred_token_shift(f, mu):
    prev = jnp.pad(f[:, :-1], ((0, 0), (1, 0), (0, 0)))
    nxt = jnp.pad(f[:, 1:], ((0, 0), (0, 1), (0, 0)))
    return f + mu[0] * (prev - f) + mu[1] * (nxt - f)


def _head_sums(x):
    w = x.shape[-1]
    blk_r = lax.broadcasted_iota(jnp.int32, (w, w), 0) // HEAD_DIM
    blk_c = lax.broadcasted_iota(jnp.int32, (w, w), 1) // HEAD_DIM
    ones_bd = (blk_r == blk_c).astype(_BF)
    hi = x.astype(_BF)
    lo = (x - hi.astype(_F32)).astype(_BF)
    return (jnp.dot(hi, ones_bd, preferred_element_type=_F32)
            + jnp.dot(lo, ones_bd, preferred_element_type=_F32))


RW_FEATURE_ROWS = 512
_VEC_V0, _VEC_W0, _VEC_A0, _VEC_KK, _VEC_KA, _VEC_RK = 0, 1, 3, 5, 6, 7


def _rw_feature_kernel(f_ref, vf_ref, vec_ref, v1_ref, v2_ref, g1_ref, g2_ref, w1_ref, w2_ref,
                       a1_ref, a2_ref, v_ref, nkk_ref, g_ref, bonus_ref,
                       lw0_ref, kd0_ref, b0_ref, lw1_ref, kd1_ref, b1_ref, *, value_residual):
    w = RW_WIDTH
    f = f_ref[...]
    r, k, v, z = (f[:, j * w:(j + 1) * w] for j in range(4))
    vec = vec_ref[...]

    def row(i):
        return vec[i:i + 1, :]

    zb = z.astype(_BF)

    def lora(a, b, act=None):
        h = jnp.dot(zb, a.astype(_BF), preferred_element_type=_F32)
        h = h if act is None else act(h)
        return jnp.dot(h.astype(_BF), b.astype(_BF), preferred_element_type=_F32)

    if value_residual:
        v = v + (vf_ref[...] - v) * jax.nn.sigmoid(row(_VEC_V0) + lora(v1_ref[...], v2_ref[...]))
    kk = k * row(_VEC_KK)
    kk = kk * lax.rsqrt(_head_sums(kk * kk) + 1e-12)
    bonus = jnp.zeros_like(v)
    for d, (lw_ref, kd_ref, b_ref) in enumerate(((lw0_ref, kd0_ref, b0_ref), (lw1_ref, kd1_ref, b1_ref))):
        wl = -jax.nn.softplus(-(row(_VEC_W0 + d) + lora(w1_ref[d], w2_ref[d], jnp.tanh))) - 0.5
        a = jax.nn.sigmoid(row(_VEC_A0 + d) + lora(a1_ref[d], a2_ref[d]))
        k_d = k * (1 + (a - 1) * row(_VEC_KA))
        bonus = bonus + _head_sums(r * k_d * row(_VEC_RK)) * v
        lw_ref[...] = -jnp.exp(wl)
        kd_ref[...] = k_d
        b_ref[...] = kk * a
    v_ref[...] = v
    nkk_ref[...] = -kk
    g_ref[...] = lora(g1_ref[...], g2_ref[...], jax.nn.sigmoid)
    bonus_ref[...] = bonus


def rw_features(f_tok, vf_tok, rw, value_residual):
    t = f_tok.shape[0]
    w = RW_WIDTH
    zeros = jnp.zeros((w,), _F32)
    vec = jnp.stack([rw['v0'] if value_residual else zeros, rw['w0'][0], rw['w0'][1],
                     rw['a0'][0], rw['a0'][1], rw['k_k'], rw['k_a'], rw['r_k']])
    if value_residual:
        v1, v2 = rw['v1'], rw['v2']
    else:
        vf_tok = f_tok
        v1, v2 = jnp.zeros((w, 8), _F32), jnp.zeros((8, w), _F32)
    weights = [v1, v2, rw['g1'], rw['g2'], rw['w1'], rw['w2'], rw['a1'], rw['a2']]

    def full(a):
        return pl.BlockSpec(a.shape, lambda i, n=a.ndim: (0,) * n)

    rows = pl.BlockSpec((RW_FEATURE_ROWS, w), lambda i: (i, 0))
    return pl.pallas_call(
        functools.partial(_rw_feature_kernel, value_residual=value_residual),
        grid=(t // RW_FEATURE_ROWS,),
        in_specs=[pl.BlockSpec((RW_FEATURE_ROWS, 4 * w), lambda i: (i, 0)), rows, full(vec)]
        + [full(a) for a in weights],
        out_specs=[rows] * 10,
        out_shape=[jax.ShapeDtypeStruct((t, w), _F32)] * 10,
        compiler_params=pltpu.CompilerParams(
            dimension_semantics=("parallel",), vmem_limit_bytes=48 * 1024 * 1024),
        name="rw_features",
    )(f_tok, vf_tok, vec, *weights)


def rwkv_branch(p_rw, vf_tok, rw, n_batch, n_ctx, n_lat):
    n_c = n_batch * n_ctx
    value_residual = vf_tok is not None
    f_tok = jnp.concatenate(
        [centred_token_shift(p.reshape(n_batch, n, -1), rw['mu']).reshape(n_batch * n, -1)
         for p, n in ((p_rw[:n_c], n_ctx), (p_rw[n_c:], n_lat))], axis=0)
    v, nkk, g, bonus, lw0, kd0, b0, lw1, kd1, b1 = rw_features(f_tok, vf_tok, rw, value_residual)
    sum_fwd = rw_summaries(f_tok, lw0, kd0, v, nkk, b0, False)
    sum_bwd = rw_summaries(f_tok, lw1, kd1, v, nkk, b1, True)
    y_fwd, y_bwd = rw_state_pass(sum_fwd, sum_bwd, n_batch, n_ctx, n_lat)
    return y_fwd, y_bwd, bonus, g, v


PROJ_ROWS = 512
MERGE_ROWS = 256
MIXER_COLS = sum(IN_WIDTHS[:-1])
PROJ_GROUPS = tuple((sum(IN_WIDTHS[:j]), IN_WIDTHS[j]) for j in range(len(IN_WIDTHS) - 1))
VMEM_LIMIT_BIG = 56 * 1024 * 1024


def _rms_mod(x, gain, shift, scale):
    y = x * lax.rsqrt(jnp.mean(x * x, axis=-1, keepdims=True) + NORM_EPS) * gain
    return y * (1 + scale) + shift


def _in_proj_kernel(idx_ref, x_ref, mod_ref, gain_ref, *refs):
    del idx_ref
    d = x_ref.shape[-1]
    n_out = len(refs) // 2
    mod = mod_ref[0]
    h = _rms_mod(x_ref[...], gain_ref[...], mod[:, 0:d], mod[:, d:2 * d]).astype(_BF)
    for w_ref, o_ref in zip(refs[:n_out], refs[n_out:]):
        o_ref[...] = jnp.dot(h, w_ref[...], preferred_element_type=_F32)


def fused_in_proj(x_tok, tile_mod, mod_tab, gain, w_groups):
    t, d = x_tok.shape
    full = lambda i, idx: (0, 0)
    grid_spec = pltpu.PrefetchScalarGridSpec(
        num_scalar_prefetch=1,
        grid=(t // PROJ_ROWS,),
        in_specs=[pl.BlockSpec((PROJ_ROWS, d), lambda i, idx: (i, 0)),
                  pl.BlockSpec((1, 1, mod_tab.shape[-1]), lambda i, idx: (idx[i], 0, 0)),
                  pl.BlockSpec((1, d), full)]
        + [pl.BlockSpec(w.shape, full) for w in w_groups],
        out_specs=[pl.BlockSpec((PROJ_ROWS, w.shape[1]), lambda i, idx: (i, 0)) for w in w_groups])
    return pl.pallas_call(
        _in_proj_kernel,
        grid_spec=grid_spec,
        out_shape=[jax.ShapeDtypeStruct((t, w.shape[1]), _F32) for w in w_groups],
        compiler_params=pltpu.CompilerParams(
            dimension_semantics=("parallel",), vmem_limit_bytes=VMEM_LIMIT_BIG),
        name="in_proj",
    )(tile_mod, x_tok, mod_tab, gain[None], *w_groups)


def _merge_kernel(idx_ref, x_ref, mod_ref, gain1_ref, gain2_ref, rwf_ref, rwb_ref, rwbonus_ref,
                  rwgate_ref, rwn_ref, yb_ref, yc_ref, yd_ref, wg_ref, wb_ref, wo_ref, xo_ref, f_ref):
    rwy_ref = (rwf_ref[...], rwb_ref[...], rwbonus_ref[...], rwgate_ref[...])
    del idx_ref
    d = x_ref.shape[-1]
    x = x_ref[...]
    mod = mod_ref[0]
    sh1, sc1, g1, sh2, sc2 = (mod[:, j * d:(j + 1) * d] for j in range(5))
    h = _rms_mod(x, gain1_ref[...], sh1, sc1).astype(_BF)
    y = rwy_ref[0] + rwy_ref[1]
    dev = y - _head_sums(y) * (1.0 / HEAD_DIM)
    var = _head_sums(dev * dev) * (1.0 / HEAD_DIM)
    yn = dev * lax.rsqrt(var + RW_GN_EPS) * rwn_ref[0:1, :] + rwn_ref[1:2, :]
    ys = [(yn + rwy_ref[2]) * rwy_ref[3], yb_ref[...], yc_ref[...], yd_ref[...]]
    m = None
    for i, y_i in enumerate(ys):
        gate = jax.nn.sigmoid(jnp.dot(h, wg_ref[:, i * d:(i + 1) * d], preferred_element_type=_F32))
        term = gate * jnp.dot(y_i.astype(_BF), wb_ref[i], preferred_element_type=_F32)
        m = term if m is None else m + term
    x_new = x + g1 * jnp.dot(m.astype(_BF), wo_ref[...], preferred_element_type=_F32)
    xo_ref[...] = x_new
    f_ref[...] = _rms_mod(x_new, gain2_ref[...], sh2, sc2).astype(_BF)


def fused_merge(x_tok, tile_mod, mod_tab, gain1, gain2, rw_parts, rw_norm, ys, w_gate, w_branch, w_out):
    t, d = x_tok.shape
    full2 = lambda i, idx: (0, 0)
    rows = lambda i, idx: (i, 0)
    branch = pl.BlockSpec((MERGE_ROWS, BRANCH_WIDTH), rows)
    grid_spec = pltpu.PrefetchScalarGridSpec(
        num_scalar_prefetch=1,
        grid=(t // MERGE_ROWS,),
        in_specs=[pl.BlockSpec((MERGE_ROWS, d), rows),
                  pl.BlockSpec((1, 1, mod_tab.shape[-1]), lambda i, idx: (idx[i], 0, 0)),
                  pl.BlockSpec((1, d), full2), pl.BlockSpec((1, d), full2)]
        + [branch] * 4 + [pl.BlockSpec(rw_norm.shape, full2)] + [branch] * (N_BRANCH - 1)
        + [pl.BlockSpec(w_gate.shape, full2),
           pl.BlockSpec(w_branch.shape, lambda i, idx: (0, 0, 0)),
           pl.BlockSpec(w_out.shape, full2)],
        out_specs=[pl.BlockSpec((MERGE_ROWS, d), rows), pl.BlockSpec((MERGE_ROWS, d), rows)])
    return pl.pallas_call(
        _merge_kernel,
        grid_spec=grid_spec,
        out_shape=[jax.ShapeDtypeStruct((t, d), _F32), jax.ShapeDtypeStruct((t, d), _BF)],
        compiler_params=pltpu.CompilerParams(
            dimension_semantics=("parallel",), vmem_limit_bytes=VMEM_LIMIT_BIG),
        name="merge_proj",
    )(tile_mod, x_tok, mod_tab, gain1[None], gain2[None], *rw_parts, rw_norm, *ys, w_gate, w_branch, w_out)


FFN_ROWS = 512
FFN_COLS = 512


def _swiglu_accumulate(x_ref, w1_ref, w3_ref, w2_ref, acc_ref):
    @pl.when(pl.program_id(1) == 0)
    def _():
        acc_ref[...] = jnp.zeros_like(acc_ref)

    x = x_ref[...]
    h1 = jnp.dot(x, w1_ref[0], preferred_element_type=_F32)
    h3 = jnp.dot(x, w3_ref[0], preferred_element_type=_F32)
    h = (h1 * jax.nn.sigmoid(h1) * h3).astype(_BF)
    acc_ref[...] += jnp.dot(h, w2_ref[0], preferred_element_type=_F32)


def pack_bf16_pairs(x):
    half = x.shape[-1] // 2
    bits = lax.bitcast_convert_type(x.astype(_F32), jnp.uint32)
    return bits[:, :half] | (bits[:, half:] >> 16)


def _moe_swiglu_kernel(be_ref, tok_ref, src_ref, w1_ref, w3_ref, w2_ref, g_ref, o_ref,
                       acc_ref, rows_ref, x_ref):
    del be_ref

    @pl.when(pl.program_id(1) == 0)
    def _():
        def gather(r, carry):
            rows_ref[pl.ds(r, 1), :] = src_ref[pl.ds(tok_ref[0, 0, r], 1), :]
            return carry

        lax.fori_loop(0, rows_ref.shape[0], gather, 0, unroll=8)
        packed = rows_ref[...]
        hi = lax.bitcast_convert_type(packed & jnp.uint32(0xFFFF0000), _F32)
        lo = lax.bitcast_convert_type(packed << 16, _F32)
        x_ref[...] = jnp.concatenate([hi, lo], axis=1).astype(_BF)

    _swiglu_accumulate(x_ref, w1_ref, w3_ref, w2_ref, acc_ref)

    @pl.when(pl.program_id(1) == pl.num_programs(1) - 1)
    def _():
        o_ref[...] = acc_ref[...] * g_ref[...]


def _ffn_residual_kernel(idx_ref, f_ref, x_ref, mod_ref, w1_ref, w3_ref, w2_ref, o_ref, acc_ref):
    del idx_ref
    _swiglu_accumulate(f_ref, w1_ref, w3_ref, w2_ref, acc_ref)

    @pl.when(pl.program_id(1) == pl.num_programs(1) - 1)
    def _():
        d = x_ref.shape[-1]
        o_ref[...] = x_ref[...] + mod_ref[0][:, 5 * d:6 * d] * acc_ref[...]


def ffn_residual(f_tok, x_tok, tile_mod, mod_tab, w1, w3, w2):
    t, d = x_tok.shape
    n_f = w1.shape[-1]
    rows = lambda i, f, idx: (i, 0)
    grid_spec = pltpu.PrefetchScalarGridSpec(
        num_scalar_prefetch=1,
        grid=(t // PROJ_ROWS, n_f // FFN_COLS),
        in_specs=[pl.BlockSpec((PROJ_ROWS, d), rows), pl.BlockSpec((PROJ_ROWS, d), rows),
                  pl.BlockSpec((1, 1, mod_tab.shape[-1]), lambda i, f, idx: (idx[i], 0, 0)),
                  pl.BlockSpec((1, d, FFN_COLS), lambda i, f, idx: (0, 0, f)),
                  pl.BlockSpec((1, d, FFN_COLS), lambda i, f, idx: (0, 0, f)),
                  pl.BlockSpec((1, FFN_COLS, d), lambda i, f, idx: (0, f, 0))],
        out_specs=pl.BlockSpec((PROJ_ROWS, d), rows),
        scratch_shapes=[pltpu.VMEM((PROJ_ROWS, d), _F32)])
    return pl.pallas_call(
        _ffn_residual_kernel,
        grid_spec=grid_spec,
        out_shape=jax.ShapeDtypeStruct((t, d), _F32),
        compiler_params=pltpu.CompilerParams(
            dimension_semantics=("parallel", "arbitrary"),
            vmem_limit_bytes=48 * 1024 * 1024),
        name="ffn_residual",
    )(tile_mod, f_tok, x_tok, mod_tab, w1.astype(_BF)[None], w3.astype(_BF)[None], w2.astype(_BF)[None])


def grouped_swiglu(h, tok, block_e, gate, w1, w3, w2):
    n_rows = tok.shape[0]
    d = h.shape[1]
    n_f = w1.shape[-1]
    assert n_rows % FFN_ROWS == 0 and n_f % FFN_COLS == 0
    src = pack_bf16_pairs(h)
    grid_spec = pltpu.PrefetchScalarGridSpec(
        num_scalar_prefetch=1,
        grid=(n_rows // FFN_ROWS, n_f // FFN_COLS),
        in_specs=[pl.BlockSpec((1, 1, FFN_ROWS), lambda i, f, be: (i, 0, 0), memory_space=pltpu.SMEM),
                  pl.BlockSpec(src.shape, lambda i, f, be: (0, 0), pipeline_mode=pl.Buffered(1)),
                  pl.BlockSpec((1, d, FFN_COLS), lambda i, f, be: (be[i], 0, f)),
                  pl.BlockSpec((1, d, FFN_COLS), lambda i, f, be: (be[i], 0, f)),
                  pl.BlockSpec((1, FFN_COLS, d), lambda i, f, be: (be[i], f, 0)),
                  pl.BlockSpec((FFN_ROWS, 1), lambda i, f, be: (i, 0))],
        out_specs=pl.BlockSpec((FFN_ROWS, d), lambda i, f, be: (i, 0)),
        scratch_shapes=[pltpu.VMEM((FFN_ROWS, d), _F32),
                        pltpu.VMEM((FFN_ROWS, d // 2), jnp.uint32),
                        pltpu.VMEM((FFN_ROWS, d), _BF)])
    return pl.pallas_call(
        _moe_swiglu_kernel,
        grid_spec=grid_spec,
        out_shape=jax.ShapeDtypeStruct((n_rows, d), _F32),
        compiler_params=pltpu.CompilerParams(
            dimension_semantics=("arbitrary", "arbitrary"), vmem_limit_bytes=VMEM_LIMIT_BIG),
        name="grouped_swiglu",
    )(block_e, tok.reshape(n_rows // FFN_ROWS, 1, FFN_ROWS), src,
      w1.astype(_BF), w3.astype(_BF), w2.astype(_BF), gate)


def moe_swiglu(h, w_router, w1, w3, w2):
    n_tok, d = h.shape
    logits = jnp.dot(h, w_router.astype(_BF), preferred_element_type=jnp.float32)
    top_logit, top_idx = lax.top_k(logits, TOP_K)
    gates = jax.nn.softmax(top_logit, axis=-1)
    n_assign = n_tok * TOP_K
    flat_e = top_idx.reshape(-1)
    order = jnp.argsort(flat_e)
    sorted_e = flat_e[order]
    sorted_tok = (order // TOP_K).astype(jnp.int32)
    sorted_gate = gates.reshape(-1)[order]
    counts = jnp.bincount(flat_e, length=N_EXPERTS)
    padded = (counts + FFN_ROWS - 1) // FFN_ROWS * FFN_ROWS
    start = jnp.cumsum(counts) - counts
    pend = jnp.cumsum(padded)
    pstart = pend - padded
    dest = pstart[sorted_e] + jnp.arange(n_assign) - start[sorted_e]
    n_blocks = -(-n_assign // FFN_ROWS) + N_EXPERTS
    n_rows = n_blocks * FFN_ROWS
    tok = jnp.zeros((n_rows,), jnp.int32).at[dest].set(sorted_tok)
    gate = jnp.zeros((n_rows,), _F32).at[dest].set(sorted_gate)
    block_e = jnp.minimum(jnp.searchsorted(pend, jnp.arange(n_blocks) * FFN_ROWS, side='right'),
                          N_EXPERTS - 1).astype(jnp.int32)
    yb = grouped_swiglu(h, tok, block_e, gate[:, None], w1, w3, w2)
    return jnp.zeros((n_tok, d), _F32).at[tok].add(yb)


def kernel(x, c, ctx, c_ctx, w_mod, b_mod, norm1, norm2, w_in, w_branch, w_out,
           rw_mu, rw_w0, rw_w1, rw_w2, rw_a0, rw_a1, rw_a2, rw_v0, rw_v1, rw_v2,
           rw_g1, rw_g2, rw_k_k, rw_k_a, rw_r_k, rw_ln_w, rw_ln_b, sw_sink,
           mla_q_norm, mla_w_uq, mla_kv_norm, mla_w_ukv, ga_q_norm, ga_k_norm,
           ffn_w1, ffn_w3, ffn_w2, moe_router, moe_w1, moe_w3, moe_w2, final_norm):
    b, s_len, d = x.shape
    n_ctx_tok = ctx.shape[1]
    n_c = b * n_ctx_tok
    assert n_c % PROJ_ROWS == 0 and s_len % PROJ_ROWS == 0
    rope64 = axial_rope_tables(s_len, HEAD_DIM)
    rope32 = axial_rope_tables(s_len, MLA_ROPE)
    silu_c = jax.nn.silu(c)
    silu_cc = jax.nn.silu(c_ctx)
    x_tok = jnp.concatenate([ctx.reshape(n_c, d), x.reshape(b * s_len, d)], axis=0)
    seg_rows = [n_c] + [s_len] * b

    def tile_table(rows):
        return jnp.concatenate([jnp.full((n // rows,), i, jnp.int32) for i, n in enumerate(seg_rows)])

    tile_mod_proj, tile_mod_merge = tile_table(PROJ_ROWS), tile_table(MERGE_ROWS)

    def split_tokens(t):
        return t[:n_c].reshape(b, n_ctx_tok, -1), t[n_c:].reshape(b, s_len, -1)

    vf_tok = None
    for l in range(DEPTH):
        ctx_out = l < DEPTH - 1
        mod_l = silu_c @ w_mod[l] + b_mod[l]
        mod_c = silu_cc @ w_mod[l] + b_mod[l]
        mod_tab = jnp.concatenate([mod_c[None], mod_l], axis=0)[:, None, :]

        w_groups = []
        for off, width in PROJ_GROUPS:
            w = w_in[l][:, off:off + width].astype(_BF)
            w_groups.append(jnp.pad(w, ((0, 0), (0, _round_up(width, LANES) - width))))
        proj = fused_in_proj(x_tok, tile_mod_proj, mod_tab, norm1[l], w_groups)
        p_rw = proj[0]
        pc, pl_ = [None], [None]
        for p, (_, width) in zip(proj[1:], PROJ_GROUPS[1:]):
            p_ctx, p_lat = split_tokens(p[:, :width])
            pc.append(p_ctx)
            pl_.append(p_lat)
        rw = dict(mu=rw_mu[l], w0=rw_w0[l], w1=rw_w1[l], w2=rw_w2[l],
                  a0=rw_a0[l], a1=rw_a1[l], a2=rw_a2[l],
                  v0=rw_v0[l - 1] if l > 0 else None,
                  v1=rw_v1[l - 1] if l > 0 else None,
                  v2=rw_v2[l - 1] if l > 0 else None,
                  g1=rw_g1[l], g2=rw_g2[l], k_k=rw_k_k[l], k_a=rw_k_a[l], r_k=rw_r_k[l],
                  ln_w=rw_ln_w[l], ln_b=rw_ln_b[l])
        rw_yf, rw_yb, rw_bonus, rw_gate, rw_v = rwkv_branch(p_rw, vf_tok, rw, b, n_ctx_tok, s_len)
        if l == 0:
            vf_tok = rw_v
        yb_c, yb_l = window_mixer(pc[1:4], pl_[1:4], sw_sink[l], rope64, ctx_out)
        yc_c, yc_l = mla_mixer(pc[4:7], pl_[4:7], mla_q_norm[l], mla_w_uq[l], mla_kv_norm[l],
                               mla_w_ukv[l], rope32, ctx_out)
        yd_c, yd_l = grid_attention_mixer(pc[7:10], pl_[7:10], ga_q_norm[l], ga_k_norm[l], rope64, ctx_out)

        def tokens_of(y_ctx, y_lat):
            y_ctx = jnp.zeros((n_c, BRANCH_WIDTH), _F32) if y_ctx is None else y_ctx.reshape(n_c, BRANCH_WIDTH)
            return jnp.concatenate([y_ctx, y_lat.reshape(b * s_len, BRANCH_WIDTH)], axis=0)

        ys = [tokens_of(yc_, yl_) for yc_, yl_ in ((yb_c, yb_l), (yc_c, yc_l), (yd_c, yd_l))]
        x_tok, f_tok = fused_merge(x_tok, tile_mod_merge, mod_tab, norm1[l], norm2[l],
                                   (rw_yf, rw_yb, rw_bonus, rw_gate),
                                   jnp.stack([rw_ln_w[l], rw_ln_b[l]]), ys,
                                   w_in[l][:, MIXER_COLS:].astype(_BF), w_branch[l].astype(_BF),
                                   w_out[l].astype(_BF))
        if l % 2 == 0:
            x_tok = ffn_residual(f_tok, x_tok, tile_mod_proj, mod_tab,
                                 ffn_w1[l // 2], ffn_w3[l // 2], ffn_w2[l // 2])
        else:
            first = 0 if ctx_out else n_c
            ffn_out = moe_swiglu(f_tok[first:], moe_router[l // 2], moe_w1[l // 2], moe_w3[l // 2],
                                 moe_w2[l // 2])
            pieces, row = [], 0
            for i, n in enumerate(seg_rows):
                if row >= first:
                    g2 = mod_tab[i, :, 5 * d:]
                    pieces.append(x_tok[row:row + n] + g2 * ffn_out[row - first:row - first + n])
                else:
                    pieces.append(x_tok[row:row + n])
                row += n
            x_tok = jnp.concatenate(pieces, axis=0)
    return rms_norm(x_tok[n_c:].reshape(b, s_len, d), final_norm)
```

```python
import functools

import jax
import jax.numpy as jnp
from jax import lax
from jax.experimental import pallas as pl
from jax.experimental.pallas import tpu as pltpu

D_MODEL = 1024
DEPTH = 4
GRID_W = 64
HEAD_DIM = 64
BLOCK = 128
WINDOW = 128
ROPE_BASE = 10000.0
NORM_EPS = 1e-6
NEG_INF = -1e30
N_BRANCH = 4
BRANCH_WIDTH = 256
RW_HEADS = 4
RW_WIDTH = RW_HEADS * HEAD_DIM
RW_GN_EPS = 64e-5
SW_HEADS = 4
SW_KV_HEADS = 2
MLA_HEADS = 4
MLA_NOPE = 64
MLA_ROPE = 32
MLA_V = 64
MLA_Q_RANK = 256
MLA_KV_RANK = 128
GA_HEADS = 4
GA_KV_HEADS = 2
N_EXPERTS = 8
TOP_K = 2
MOE_BLOCK = 128
IN_WIDTHS = (4 * RW_WIDTH,
             SW_HEADS * HEAD_DIM, SW_KV_HEADS * HEAD_DIM, SW_KV_HEADS * HEAD_DIM,
             MLA_Q_RANK, MLA_KV_RANK, MLA_ROPE,
             GA_HEADS * HEAD_DIM, GA_KV_HEADS * HEAD_DIM, GA_KV_HEADS * HEAD_DIM,
             N_BRANCH * D_MODEL)


def _mm_kernel(a_ref, b_ref, o_ref):
    o_ref[...] = jnp.dot(a_ref[...], b_ref[...], preferred_element_type=jnp.float32)


def _round_up(x, m):
    return (x + m - 1) // m * m


def pmm(a, b, tm=512, tn=512):
    m, k = a.shape
    _, n = b.shape
    tm = min(tm, _round_up(m, 8))
    tn = min(tn, _round_up(n, 128))
    mp, np_ = _round_up(m, tm), _round_up(n, tn)
    a = a.astype(jnp.bfloat16)
    b = b.astype(jnp.bfloat16)
    if mp != m:
        a = jnp.pad(a, ((0, mp - m), (0, 0)))
    if np_ != n:
        b = jnp.pad(b, ((0, 0), (0, np_ - n)))
    out = pl.pallas_call(
        _mm_kernel,
        grid=(np_ // tn, mp // tm),
        in_specs=[pl.BlockSpec((tm, k), lambda j, i: (i, 0)),
                  pl.BlockSpec((k, tn), lambda j, i: (0, j))],
        out_specs=pl.BlockSpec((tm, tn), lambda j, i: (i, j)),
        out_shape=jax.ShapeDtypeStruct((mp, np_), jnp.float32),
        compiler_params=pltpu.CompilerParams(
            dimension_semantics=("arbitrary", "arbitrary"),
            vmem_limit_bytes=48 * 1024 * 1024),
        name="pmm",
    )(a, b)
    return out[:m, :n]


def pmm_nd(a, b, **kw):
    lead = a.shape[:-1]
    return pmm(a.reshape(-1, a.shape[-1]), b, **kw).reshape(lead + (b.shape[-1],))


RW_CHUNK = 64
RW_CHUNKS_PER_STEP = 4
RW_BLOCK = RW_CHUNK * RW_CHUNKS_PER_STEP
RW_INV_BASE = 4

_BF = jnp.bfloat16
_F32 = jnp.float32


def _bd_rows(x):
    lane_head = lax.broadcasted_iota(jnp.int32, x.shape, 1) // HEAD_DIM
    return jnp.concatenate([jnp.where(lane_head == h, x, 0.0) for h in range(RW_HEADS)], axis=0)


def _dot(a, b):
    return jnp.dot(a.astype(_BF), b.astype(_BF), preferred_element_type=_F32)


def _dot_nt(a, b):
    return lax.dot_general(a.astype(_BF), b.astype(_BF), (((1,), (1,)), ((), ())),
                           preferred_element_type=_F32)


def _dot_tn(a, b):
    return lax.dot_general(a.astype(_BF), b.astype(_BF), (((0,), (0,)), ((), ())),
                           preferred_element_type=_F32)


def _split3(x):
    h1 = x.astype(_BF)
    r1 = x - h1.astype(_F32)
    h2 = r1.astype(_BF)
    h3 = (r1 - h2.astype(_F32)).astype(_BF)
    return h1, h2, h3


def _each(fn, *lists):
    return [fn(*args) for args in zip(*lists)]


def _chunk_summaries(r, lw, k, v, a, b, reverse):
    c = RW_CHUNK
    row = lax.broadcasted_iota(jnp.int32, (c, c), 0)
    col = lax.broadcasted_iota(jnp.int32, (c, c), 1)
    tri = ((row <= col) if reverse else (row >= col)).astype(_BF)
    cum = _each(lambda x: sum(jnp.dot(tri, p, preferred_element_type=_F32) for p in _split3(x)), lw)
    lwlast = _each(lambda x: x[0:1, :] if reverse else x[c - 1:c, :], cum)
    rt = _each(lambda x, cu: x * jnp.exp(cu), r, cum)
    at = _each(lambda x, cu, l: x * jnp.exp(cu - l), a, cum, lw)
    w_inv = _each(lambda cu: jnp.exp(-cu), cum)
    w_tail = _each(lambda cu, ll: jnp.exp(ll - cu), cum, lwlast)
    wlast = _each(jnp.exp, lwlast)
    bt = _each(jnp.multiply, b, w_inv)
    kt = _each(jnp.multiply, k, w_inv)
    bh = _each(jnp.multiply, b, w_tail)
    kh = _each(jnp.multiply, k, w_tail)

    t_idx = lax.broadcasted_iota(jnp.int32, (c, RW_WIDTH), 0)
    s_idx = lax.broadcasted_iota(jnp.int32, (c, RW_WIDTH), 1) % HEAD_DIM
    strict = (s_idx > t_idx) if reverse else (s_idx < t_idx)
    incl = (s_idx >= t_idx) if reverse else (s_idx <= t_idx)

    l1 = _each(lambda x, y: jnp.concatenate([x, y], axis=0), at, rt)
    g_b = _each(lambda x, y: _dot_nt(x, _bd_rows(y)), l1, bt)
    g_k = _each(lambda x, y: _dot_nt(x, _bd_rows(y)), l1, kt)
    a_ab = _each(lambda g: jnp.where(strict, g[:c], 0.0), g_b)
    a_rb = _each(lambda g: jnp.where(incl, g[c:], 0.0), g_b)
    a_ak = _each(lambda g: jnp.where(strict, g[:c], 0.0), g_k)
    a_rk = _each(lambda g: jnp.where(incl, g[c:], 0.0), g_k)

    eye = jnp.where(s_idx == t_idx, 1.0, 0.0)
    base = RW_INV_BASE
    n_b = _each(lambda x: jnp.where(t_idx // base == s_idx // base, x, 0.0), a_ab)
    n_b2 = _each(lambda x: _dot(x, _bd_rows(x)), n_b)
    n_b3 = _each(lambda x, y: _dot(x, _bd_rows(y)), n_b, n_b2)
    tm = _each(lambda x, y, z: eye + x + y + z, n_b, n_b2, n_b3)
    blk = base
    while blk < c:
        off = (t_idx // (2 * blk) == s_idx // (2 * blk)) & (t_idx // blk != s_idx // blk)
        x = _each(lambda t, n: _dot(t, _bd_rows(jnp.where(off, n, 0.0))), tm, a_ab)
        tm = _each(lambda t, xx: t + _dot(xx, _bd_rows(t)), tm, x)
        blk *= 2

    av = _each(lambda x, y, vv: _dot(jnp.concatenate([x, y], axis=0), _bd_rows(vv)), a_ak, a_rk, v)
    ta = _each(lambda t, x: _dot(t, _bd_rows(x)), tm, at)
    u0 = _each(lambda t, x: _dot(t, _bd_rows(x[:c])), tm, av)
    r1 = _each(lambda x, ar, t: x + _dot(ar, _bd_rows(t)), rt, a_rb, ta)
    y0 = _each(lambda x, ar, u: x[c:] + _dot(ar, _bd_rows(u)), av, a_rb, u0)

    blk_r = lax.broadcasted_iota(jnp.int32, (RW_WIDTH, RW_WIDTH), 0) // HEAD_DIM
    blk_c = lax.broadcasted_iota(jnp.int32, (RW_WIDTH, RW_WIDTH), 1) // HEAD_DIM
    m_bd = _each(lambda t, x: jnp.where(blk_r == blk_c, _dot_tn(t, x), 0.0), ta, bh)
    z = _each(lambda u, vv, x, y: _dot_tn(jnp.concatenate([u, vv], axis=0),
                                          jnp.concatenate([x, y], axis=0)), u0, v, bh, kh)
    lane_head = lax.broadcasted_iota(jnp.int32, (HEAD_DIM, RW_WIDTH), 1) // HEAD_DIM

    def diag_blocks(zz):
        out = zz[:HEAD_DIM]
        for h in range(1, RW_HEADS):
            out = jnp.where(lane_head == h, zz[h * HEAD_DIM:(h + 1) * HEAD_DIM], out)
        return out

    sadd = _each(diag_blocks, z)
    return r1, y0, m_bd, sadd, wlast


def _rw_summary_kernel(r_ref, lw_ref, k_ref, v_ref, a_ref, b_ref,
                       r1_ref, y0_ref, m_ref, sadd_ref, wl_ref, *, reverse):
    c = RW_CHUNK

    def chunks(ref):
        return [ref[pl.ds(j * c, c), :] for j in range(RW_CHUNKS_PER_STEP)]

    r1, y0, m_bd, sadd, wlast = _chunk_summaries(
        chunks(r_ref), chunks(lw_ref), chunks(k_ref), chunks(v_ref), chunks(a_ref), chunks(b_ref),
        reverse)
    r1_ref[...] = jnp.concatenate(r1, axis=0)
    y0_ref[...] = jnp.concatenate(y0, axis=0)
    m_ref[...] = jnp.stack(m_bd, axis=0).astype(_BF)
    sadd_ref[...] = jnp.stack(sadd, axis=0)
    wl_ref[...] = jnp.stack(wlast, axis=0)


def rw_summaries(r_src, lw, k, v, a, b, reverse):
    t, w = lw.shape
    c, cps = RW_CHUNK, RW_CHUNKS_PER_STEP
    assert t % RW_BLOCK == 0 and w == RW_WIDTH
    rows = pl.BlockSpec((RW_BLOCK, w), lambda i: (i, 0))
    return pl.pallas_call(
        functools.partial(_rw_summary_kernel, reverse=reverse),
        grid=(t // RW_BLOCK,),
        in_specs=[rows] * 6,
        out_specs=[rows, rows,
                   pl.BlockSpec((cps, w, w), lambda i: (i, 0, 0)),
                   pl.BlockSpec((cps, HEAD_DIM, w), lambda i: (i, 0, 0)),
                   pl.BlockSpec((cps, 1, w), lambda i: (i, 0, 0))],
        out_shape=[jax.ShapeDtypeStruct((t, w), _F32),
                   jax.ShapeDtypeStruct((t, w), _F32),
                   jax.ShapeDtypeStruct((t // c, w, w), _BF),
                   jax.ShapeDtypeStruct((t // c, HEAD_DIM, w), _F32),
                   jax.ShapeDtypeStruct((t // c, 1, w), _F32)],
        compiler_params=pltpu.CompilerParams(
            dimension_semantics=("parallel",), vmem_limit_bytes=48 * 1024 * 1024),
        name="rw_chunk_summary",
    )(r_src, lw, k, v, a, b)


def _rw_state_kernel(*refs):
    s_ref = refs[-1]
    y_refs = refs[10:12]
    c, cps = RW_CHUNK, RW_CHUNKS_PER_STEP

    @pl.when(pl.program_id(1) == 0)
    def _():
        s_ref[...] = jnp.zeros_like(s_ref)

    s = [s_ref[0], s_ref[1]]
    ys = [[None] * cps, [None] * cps]
    for step in range(cps):
        js = (step, cps - 1 - step)
        r1, y0 = ([refs[5 * d + n][pl.ds(js[d] * c, c), :] for d in range(2)] for n in range(2))
        m_bd, sadd, wlast = ([refs[5 * d + n][js[d]] for d in range(2)] for n in range(2, 5))
        y = _each(lambda rr, ss, yy: _dot_nt(rr, _bd_rows(ss)) + yy, r1, s, y0)
        s = _each(lambda ss, ww, mm, aa: ss * ww + _dot(ss, mm) + aa, s, wlast, m_bd, sadd)
        for d in range(2):
            ys[d][js[d]] = y[d]
    for d in range(2):
        y_refs[d][...] = jnp.concatenate(ys[d], axis=0)
        s_ref[d] = s[d]


def rw_state_pass(sum_fwd, sum_bwd, n_batch, n_ctx, n_lat):
    t, w = sum_fwd[0].shape
    cps = RW_CHUNKS_PER_STEP
    assert n_ctx % RW_BLOCK == 0 and n_lat % RW_BLOCK == 0
    n_cb, n_lb = n_ctx // RW_BLOCK, n_lat // RW_BLOCK
    lat0 = n_batch * n_cb

    def fwd_blk(q, i):
        return jnp.where(i < n_cb, q * n_cb + i, lat0 + q * n_lb + (i - n_cb))

    def bwd_blk(q, i):
        return jnp.where(i < n_cb, q * n_cb + (n_cb - 1 - i), lat0 + q * n_lb + (n_lb - 1 - (i - n_cb)))

    def specs(blk):
        return [pl.BlockSpec((RW_BLOCK, w), lambda q, i: (blk(q, i), 0)),
                pl.BlockSpec((RW_BLOCK, w), lambda q, i: (blk(q, i), 0)),
                pl.BlockSpec((cps, w, w), lambda q, i: (blk(q, i), 0, 0)),
                pl.BlockSpec((cps, HEAD_DIM, w), lambda q, i: (blk(q, i), 0, 0)),
                pl.BlockSpec((cps, 1, w), lambda q, i: (blk(q, i), 0, 0))]

    return pl.pallas_call(
        _rw_state_kernel,
        grid=(n_batch, n_cb + n_lb),
        in_specs=specs(fwd_blk) + specs(bwd_blk),
        out_specs=[pl.BlockSpec((RW_BLOCK, w), lambda q, i: (fwd_blk(q, i), 0)),
                   pl.BlockSpec((RW_BLOCK, w), lambda q, i: (bwd_blk(q, i), 0))],
        out_shape=[jax.ShapeDtypeStruct((t, w), _F32)] * 2,
        scratch_shapes=[pltpu.VMEM((2, HEAD_DIM, w), _F32)],
        compiler_params=pltpu.CompilerParams(
            dimension_semantics=("arbitrary", "arbitrary"),
            vmem_limit_bytes=48 * 1024 * 1024),
        name="rw_state_pass",
    )(*sum_fwd, *sum_bwd)


ATTN_TQ = 256
ATTN_TK = 768


LANES = 128
LOG2_E = 1.4426950408889634


def _dense_attn_kernel(q_ref, k_ref, v_ref, o_ref, *, tk):
    g, tq, dq = q_ref.shape[2:]
    dv = o_ref.shape[-1]
    n_keys = k_ref.shape[2]
    q = q_ref[0, 0].reshape(g * tq, dq)

    def scores(i):
        kc = k_ref[0, 0, pl.ds(i * tk, tk), :]
        return lax.dot_general(q, kc, (((1,), (1,)), ((), ())), preferred_element_type=_F32)

    def update(i, m, acc, s):
        vc = v_ref[0, 0, pl.ds(i * tk, tk), :]
        m_new = jnp.maximum(m, jnp.max(s, axis=-1, keepdims=True))
        p = jnp.exp2(s - m_new).astype(_BF)
        return m_new, jnp.exp2(m - m_new) * acc + jnp.dot(p, vc, preferred_element_type=_F32)

    n_chunks = n_keys // tk
    m = jnp.full((g * tq, 1), -jnp.inf, _F32)
    acc = jnp.zeros((g * tq, v_ref.shape[-1]), _F32)
    s = scores(0)
    for i in range(n_chunks):
        s_next = scores(i + 1) if i + 1 < n_chunks else None
        m, acc = update(i, m, acc, s)
        s = s_next
    o_ref[0, 0] = (acc[:, :dv] / acc[:, dv:dv + 1]).reshape(g, tq, dv)


def _with_ones_column(v):
    dv = v.shape[-1]
    return jnp.concatenate([v, jnp.ones_like(v[..., :1]),
                            jnp.zeros(v.shape[:-1] + (LANES - dv - 1,), v.dtype)], axis=-1).astype(_BF)


def split_heads(t, n_heads):
    d = t.shape[-1] // n_heads
    return jnp.stack([t[..., h * d:(h + 1) * d] for h in range(n_heads)], axis=1)


def merge_heads(o):
    return jnp.concatenate([o[:, h, j] for h in range(o.shape[1]) for j in range(o.shape[2])], axis=-1)


def dense_block_attention(q, k, v, scale):
    b, hk, g, s_len, dq = q.shape
    n_keys, dv = k.shape[2], v.shape[-1]
    tq = min(ATTN_TQ, s_len)
    tk = ATTN_TK if n_keys % ATTN_TK == 0 else n_keys
    assert s_len % tq == 0
    qt = (q * (scale * LOG2_E)).astype(_BF)
    kt = k.astype(_BF)
    vt = _with_ones_column(v)
    o = pl.pallas_call(
        functools.partial(_dense_attn_kernel, tk=tk),
        grid=(b, hk, s_len // tq),
        in_specs=[pl.BlockSpec((1, 1, g, tq, dq), lambda bi, h, i: (bi, h, 0, i, 0)),
                  pl.BlockSpec((1, 1, n_keys, dq), lambda bi, h, i: (bi, h, 0, 0)),
                  pl.BlockSpec((1, 1, n_keys, LANES), lambda bi, h, i: (bi, h, 0, 0))],
        out_specs=pl.BlockSpec((1, 1, g, tq, dv), lambda bi, h, i: (bi, h, 0, i, 0)),
        out_shape=jax.ShapeDtypeStruct((b, hk, g, s_len, dv), _F32),
        compiler_params=pltpu.CompilerParams(
            dimension_semantics=("parallel", "parallel", "parallel"),
            vmem_limit_bytes=48 * 1024 * 1024),
        name="dense_attention",
    )(qt, kt, vt)
    return merge_heads(o)


def rms_norm(x, g):
    xf = x.astype(jnp.float32)
    y = xf * lax.rsqrt(jnp.mean(xf * xf, axis=-1, keepdims=True) + NORM_EPS)
    return (y * g.astype(jnp.float32)).astype(x.dtype)


def split_cols(p, widths):
    out, off = [], 0
    for w in widths:
        out.append(p[..., off:off + w])
        off += w
    return out


def axial_rope_tables(n_tokens, rot_dim):
    rows = n_tokens // GRID_W
    row = jnp.repeat(jnp.arange(rows, dtype=jnp.float32), GRID_W)
    col = jnp.tile(jnp.arange(GRID_W, dtype=jnp.float32), rows)
    n_freq = rot_dim // 4
    inv_freq = ROPE_BASE ** (-jnp.arange(n_freq, dtype=jnp.float32) / n_freq)
    ang = jnp.concatenate([row[:, None] * inv_freq, col[:, None] * inv_freq], axis=-1)
    return jnp.cos(ang), jnp.sin(ang)


def apply_rope(x, cos, sin):
    half = x.shape[-1] // 2
    c = cos[None, None].astype(x.dtype)
    s = sin[None, None].astype(x.dtype)
    x1, x2 = x[..., :half], x[..., half:]
    return jnp.concatenate([x1 * c - x2 * s, x1 * s + x2 * c], axis=-1)


def context_attention(q, k, v, scale, sink=None):
    s = jnp.einsum('bhgqd,bhkd->bhgqk', q, k).astype(jnp.float32) * scale
    if sink is not None:
        s_sink = jnp.broadcast_to(sink[None, :, :, None, None].astype(jnp.float32), s.shape[:-1] + (1,))
        p = jax.nn.softmax(jnp.concatenate([s_sink, s], axis=-1), axis=-1)[..., 1:]
    else:
        p = jax.nn.softmax(s, axis=-1)
    return merge_heads(jnp.einsum('bhgqk,bhkd->bhgqd', p.astype(v.dtype), v))


WIN_TQ = 256
WIN_SPAN = WIN_TQ + 2 * WINDOW


def _window_attn_kernel(sink_ref, q_ref, k_ref, v_ref, kc_ref, vc_ref, o_ref):
    g, tq, _ = q_ref.shape[2:]
    dv = o_ref.shape[-1]
    n_lat = k_ref.shape[2]
    h, i = pl.program_id(1), pl.program_id(2)
    q = q_ref[0, 0].reshape(g * tq, q_ref.shape[-1])
    start = pl.multiple_of(jnp.clip(i * tq - WINDOW, 0, n_lat - WIN_SPAN), WINDOW)
    kw = k_ref[0, 0, pl.ds(start, WIN_SPAN), :]
    vw = v_ref[0, 0, pl.ds(start, WIN_SPAN), :]
    nt = (((1,), (1,)), ((), ()))
    s_win = lax.dot_general(q, kw, nt, preferred_element_type=_F32)
    s_ctx = lax.dot_general(q, kc_ref[0, 0], nt, preferred_element_type=_F32)
    row = lax.broadcasted_iota(jnp.int32, s_win.shape, 0)
    q_pos = i * tq + row % tq
    k_pos = start + lax.broadcasted_iota(jnp.int32, s_win.shape, 1)
    s_win = jnp.where(jnp.abs(q_pos - k_pos) <= WINDOW, s_win, NEG_INF)
    grp = lax.broadcasted_iota(jnp.int32, (g * tq, 1), 0) // tq
    sink = jnp.zeros((g * tq, 1), _F32)
    for j in range(g):
        sink = jnp.where(grp == j, sink_ref[h * g + j], sink)
    m = jnp.maximum(jnp.maximum(jnp.max(s_win, axis=-1, keepdims=True),
                                jnp.max(s_ctx, axis=-1, keepdims=True)), sink)
    acc = (jnp.dot(jnp.exp2(s_win - m).astype(_BF), vw, preferred_element_type=_F32)
           + jnp.dot(jnp.exp2(s_ctx - m).astype(_BF), vc_ref[0, 0], preferred_element_type=_F32))
    total = acc[:, dv:dv + 1] + jnp.exp2(sink - m)
    o_ref[0, 0] = (acc[:, :dv] / total).reshape(g, tq, dv)


def banded_window_attention(q, k, v, k_ctx, v_ctx, sink, scale):
    b, hk, g, s_len, d = q.shape
    n_ctx = k_ctx.shape[2]
    assert s_len % WIN_TQ == 0 and s_len >= WIN_SPAN
    qt = (q * (scale * LOG2_E)).astype(_BF)
    kt, kct = k.astype(_BF), k_ctx.astype(_BF)
    vt, vct = _with_ones_column(v), _with_ones_column(v_ctx)
    kv_map = lambda bi, h, i, sk: (bi, h, 0, 0)
    grid_spec = pltpu.PrefetchScalarGridSpec(
        num_scalar_prefetch=1,
        grid=(b, hk, s_len // WIN_TQ),
        in_specs=[pl.BlockSpec((1, 1, g, WIN_TQ, d), lambda bi, h, i, sk: (bi, h, 0, i, 0)),
                  pl.BlockSpec((1, 1, s_len, d), kv_map),
                  pl.BlockSpec((1, 1, s_len, LANES), kv_map),
                  pl.BlockSpec((1, 1, n_ctx, d), kv_map),
                  pl.BlockSpec((1, 1, n_ctx, LANES), kv_map)],
        out_specs=pl.BlockSpec((1, 1, g, WIN_TQ, d), lambda bi, h, i, sk: (bi, h, 0, i, 0)))
    o = pl.pallas_call(
        _window_attn_kernel,
        grid_spec=grid_spec,
        out_shape=jax.ShapeDtypeStruct((b, hk, g, s_len, d), _F32),
        compiler_params=pltpu.CompilerParams(
            dimension_semantics=("parallel", "parallel", "parallel"),
            vmem_limit_bytes=48 * 1024 * 1024),
        name="window_attention",
    )((sink.astype(_F32) * LOG2_E).reshape(-1), qt, kt, vt, kct, vct)
    return merge_heads(o)


def gqa_heads(q, k, v, n_heads, n_kv, rope=None, q_gain=None, k_gain=None):
    b, n, _ = q.shape
    q, k, v = split_heads(q, n_heads), split_heads(k, n_kv), split_heads(v, n_kv)
    if q_gain is not None:
        q = rms_norm(q, q_gain)
        k = rms_norm(k, k_gain)
    if rope is not None:
        q = apply_rope(q, rope[0], rope[1])
        k = apply_rope(k, rope[0], rope[1])
    return q.reshape(b, n_kv, n_heads // n_kv, n, HEAD_DIM), k, v


def window_mixer(pc, pl_, sink, rope, ctx_out):
    scale = HEAD_DIM ** -0.5
    qc, kc, vc = gqa_heads(pc[0], pc[1], pc[2], SW_HEADS, SW_KV_HEADS)
    ql, kl, vl = gqa_heads(pl_[0], pl_[1], pl_[2], SW_HEADS, SW_KV_HEADS, rope)
    sink = sink.reshape(SW_KV_HEADS, SW_HEADS // SW_KV_HEADS)
    y_lat = banded_window_attention(ql, kl, vl, kc, vc, sink, scale)
    y_ctx = context_attention(qc, kc, vc, scale, sink) if ctx_out else None
    return y_ctx, y_lat


def grid_attention_mixer(pc, pl_, q_gain, k_gain, rope, ctx_out):
    scale = HEAD_DIM ** -0.5
    qc, kc, vc = gqa_heads(pc[0], pc[1], pc[2], GA_HEADS, GA_KV_HEADS, None, q_gain, k_gain)
    ql, kl, vl = gqa_heads(pl_[0], pl_[1], pl_[2], GA_HEADS, GA_KV_HEADS, rope, q_gain, k_gain)
    y_lat = dense_block_attention(ql, jnp.concatenate([kc, kl], axis=2), jnp.concatenate([vc, vl], axis=2), scale)
    y_ctx = context_attention(qc, kc, vc, scale) if ctx_out else None
    return y_ctx, y_lat


def mla_mixer(pc, pl_, q_gain, w_uq, kv_gain, w_ukv, rope, ctx_out):
    scale = (MLA_NOPE + MLA_ROPE) ** -0.5

    def project(cq, ckv, k_rope, rope_tab):
        b, n, _ = cq.shape
        q = split_heads(rms_norm(cq, q_gain) @ w_uq, MLA_HEADS)
        kv = split_heads(rms_norm(ckv, kv_gain) @ w_ukv, MLA_HEADS)
        q_nope, q_pe = q[..., :MLA_NOPE], q[..., MLA_NOPE:]
        k_nope, v = kv[..., :MLA_NOPE], kv[..., MLA_NOPE:]
        k_pe = k_rope[:, None, :, :]
        if rope_tab is not None:
            q_pe = apply_rope(q_pe, rope_tab[0], rope_tab[1])
            k_pe = apply_rope(k_pe, rope_tab[0], rope_tab[1])
        q = jnp.concatenate([q_nope, q_pe], axis=-1)
        k = jnp.concatenate([k_nope, jnp.broadcast_to(k_pe, (b, MLA_HEADS, n, MLA_ROPE))], axis=-1)
        return q[:, :, None], k, v

    qc, kc, vc = project(pc[0], pc[1], pc[2], None)
    ql, kl, vl = project(pl_[0], pl_[1], pl_[2], rope)
    y_lat = dense_block_attention(ql, jnp.concatenate([kc, kl], axis=2), jnp.concatenate([vc, vl], axis=2), scale)
    y_ctx = context_attention(qc, kc, vc, scale) if ctx_out else None
    return y_ctx, y_lat


def centred_token_shift(f, mu):
    prev = jnp.pad(f[:, :-1], ((0, 0), (1, 0), (0, 0)))
    nxt = jnp.pad(f[:, 1:], ((0, 0), (0, 1), (0, 0)))
    return f + mu[0] * (prev - f) + mu[1] * (nxt - f)


def _head_sums(x):
    w = x.shape[-1]
    blk_r = lax.broadcasted_iota(jnp.int32, (w, w), 0) // HEAD_DIM
    blk_c = lax.broadcasted_iota(jnp.int32, (w, w), 1) // HEAD_DIM
    ones_bd = (blk_r == blk_c).astype(_BF)
    hi = x.astype(_BF)
    lo = (x - hi.astype(_F32)).astype(_BF)
    return (jnp.dot(hi, ones_bd, preferred_element_type=_F32)
            + jnp.dot(lo, ones_bd, preferred_element_type=_F32))


RW_FEATURE_ROWS = 512
_VEC_V0, _VEC_W0, _VEC_A0, _VEC_KK, _VEC_KA, _VEC_RK = 0, 1, 3, 5, 6, 7


def _rw_feature_kernel(f_ref, vf_ref, vec_ref, v1_ref, v2_ref, g1_ref, g2_ref, w1_ref, w2_ref,
                       a1_ref, a2_ref, v_ref, nkk_ref, g_ref, bonus_ref,
                       lw0_ref, kd0_ref, b0_ref, lw1_ref, kd1_ref, b1_ref, *, value_residual):
    w = RW_WIDTH
    f = f_ref[...]
    r, k, v, z = (f[:, j * w:(j + 1) * w] for j in range(4))
    vec = vec_ref[...]

    def row(i):
        return vec[i:i + 1, :]

    zb = z.astype(_BF)

    def lora(a, b, act=None):
        h = jnp.dot(zb, a.astype(_BF), preferred_element_type=_F32)
        h = h if act is None else act(h)
        return jnp.dot(h.astype(_BF), b.astype(_BF), preferred_element_type=_F32)

    if value_residual:
        v = v + (vf_ref[...] - v) * jax.nn.sigmoid(row(_VEC_V0) + lora(v1_ref[...], v2_ref[...]))
    kk = k * row(_VEC_KK)
    kk = kk * lax.rsqrt(_head_sums(kk * kk) + 1e-12)
    bonus = jnp.zeros_like(v)
    for d, (lw_ref, kd_ref, b_ref) in enumerate(((lw0_ref, kd0_ref, b0_ref), (lw1_ref, kd1_ref, b1_ref))):
        wl = -jax.nn.softplus(-(row(_VEC_W0 + d) + lora(w1_ref[d], w2_ref[d], jnp.tanh))) - 0.5
        a = jax.nn.sigmoid(row(_VEC_A0 + d) + lora(a1_ref[d], a2_ref[d]))
        k_d = k * (1 + (a - 1) * row(_VEC_KA))
        bonus = bonus + _head_sums(r * k_d * row(_VEC_RK)) * v
        lw_ref[...] = -jnp.exp(wl)
        kd_ref[...] = k_d
        b_ref[...] = kk * a
    v_ref[...] = v
    nkk_ref[...] = -kk
    g_ref[...] = lora(g1_ref[...], g2_ref[...], jax.nn.sigmoid)
    bonus_ref[...] = bonus


def rw_features(f_tok, vf_tok, rw, value_residual):
    t = f_tok.shape[0]
    w = RW_WIDTH
    zeros = jnp.zeros((w,), _F32)
    vec = jnp.stack([rw['v0'] if value_residual else zeros, rw['w0'][0], rw['w0'][1],
                     rw['a0'][0], rw['a0'][1], rw['k_k'], rw['k_a'], rw['r_k']])
    if value_residual:
        v1, v2 = rw['v1'], rw['v2']
    else:
        vf_tok = f_tok
        v1, v2 = jnp.zeros((w, 8), _F32), jnp.zeros((8, w), _F32)
    weights = [v1, v2, rw['g1'], rw['g2'], rw['w1'], rw['w2'], rw['a1'], rw['a2']]

    def full(a):
        return pl.BlockSpec(a.shape, lambda i, n=a.ndim: (0,) * n)

    rows = pl.BlockSpec((RW_FEATURE_ROWS, w), lambda i: (i, 0))
    return pl.pallas_call(
        functools.partial(_rw_feature_kernel, value_residual=value_residual),
        grid=(t // RW_FEATURE_ROWS,),
        in_specs=[pl.BlockSpec((RW_FEATURE_ROWS, 4 * w), lambda i: (i, 0)), rows, full(vec)]
        + [full(a) for a in weights],
        out_specs=[rows] * 10,
        out_shape=[jax.ShapeDtypeStruct((t, w), _F32)] * 10,
        compiler_params=pltpu.CompilerParams(
            dimension_semantics=("parallel",), vmem_limit_bytes=48 * 1024 * 1024),
        name="rw_features",
    )(f_tok, vf_tok, vec, *weights)


def rwkv_branch(p_rw, vf_tok, rw, n_batch, n_ctx, n_lat):
    n_c = n_batch * n_ctx
    value_residual = vf_tok is not None
    f_tok = jnp.concatenate(
        [centred_token_shift(p.reshape(n_batch, n, -1), rw['mu']).reshape(n_batch * n, -1)
         for p, n in ((p_rw[:n_c], n_ctx), (p_rw[n_c:], n_lat))], axis=0)
    v, nkk, g, bonus, lw0, kd0, b0, lw1, kd1, b1 = rw_features(f_tok, vf_tok, rw, value_residual)
    sum_fwd = rw_summaries(f_tok, lw0, kd0, v, nkk, b0, False)
    sum_bwd = rw_summaries(f_tok, lw1, kd1, v, nkk, b1, True)
    y_fwd, y_bwd = rw_state_pass(sum_fwd, sum_bwd, n_batch, n_ctx, n_lat)
    return y_fwd, y_bwd, bonus, g, v


PROJ_ROWS = 512
MERGE_ROWS = 256
MIXER_COLS = sum(IN_WIDTHS[:-1])
PROJ_GROUPS = tuple((sum(IN_WIDTHS[:j]), IN_WIDTHS[j]) for j in range(len(IN_WIDTHS) - 1))
VMEM_LIMIT_BIG = 56 * 1024 * 1024


def _rms_mod(x, gain, shift, scale):
    y = x * lax.rsqrt(jnp.mean(x * x, axis=-1, keepdims=True) + NORM_EPS) * gain
    return y * (1 + scale) + shift


def _in_proj_kernel(idx_ref, x_ref, mod_ref, gain_ref, *refs):
    del idx_ref
    d = x_ref.shape[-1]
    n_out = len(refs) // 2
    mod = mod_ref[0]
    h = _rms_mod(x_ref[...], gain_ref[...], mod[:, 0:d], mod[:, d:2 * d]).astype(_BF)
    for w_ref, o_ref in zip(refs[:n_out], refs[n_out:]):
        o_ref[...] = jnp.dot(h, w_ref[...], preferred_element_type=_F32)


def fused_in_proj(x_tok, tile_mod, mod_tab, gain, w_groups):
    t, d = x_tok.shape
    full = lambda i, idx: (0, 0)
    grid_spec = pltpu.PrefetchScalarGridSpec(
        num_scalar_prefetch=1,
        grid=(t // PROJ_ROWS,),
        in_specs=[pl.BlockSpec((PROJ_ROWS, d), lambda i, idx: (i, 0)),
                  pl.BlockSpec((1, 1, mod_tab.shape[-1]), lambda i, idx: (idx[i], 0, 0)),
                  pl.BlockSpec((1, d), full)]
        + [pl.BlockSpec(w.shape, full) for w in w_groups],
        out_specs=[pl.BlockSpec((PROJ_ROWS, w.shape[1]), lambda i, idx: (i, 0)) for w in w_groups])
    return pl.pallas_call(
        _in_proj_kernel,
        grid_spec=grid_spec,
        out_shape=[jax.ShapeDtypeStruct((t, w.shape[1]), _F32) for w in w_groups],
        compiler_params=pltpu.CompilerParams(
            dimension_semantics=("parallel",), vmem_limit_bytes=VMEM_LIMIT_BIG),
        name="in_proj",
    )(tile_mod, x_tok, mod_tab, gain[None], *w_groups)


def _merge_kernel(idx_ref, x_ref, mod_ref, gain1_ref, gain2_ref, rwf_ref, rwb_ref, rwbonus_ref,
                  rwgate_ref, rwn_ref, yb_ref, yc_ref, yd_ref, wg_ref, wb_ref, wo_ref, xo_ref, f_ref):
    rwy_ref = (rwf_ref[...], rwb_ref[...], rwbonus_ref[...], rwgate_ref[...])
    del idx_ref
    d = x_ref.shape[-1]
    x = x_ref[...]
    mod = mod_ref[0]
    sh1, sc1, g1, sh2, sc2 = (mod[:, j * d:(j + 1) * d] for j in range(5))
    h = _rms_mod(x, gain1_ref[...], sh1, sc1).astype(_BF)
    y = rwy_ref[0] + rwy_ref[1]
    dev = y - _head_sums(y) * (1.0 / HEAD_DIM)
    var = _head_sums(dev * dev) * (1.0 / HEAD_DIM)
    yn = dev * lax.rsqrt(var + RW_GN_EPS) * rwn_ref[0:1, :] + rwn_ref[1:2, :]
    ys = [(yn + rwy_ref[2]) * rwy_ref[3], yb_ref[...], yc_ref[...], yd_ref[...]]
    m = None
    for i, y_i in enumerate(ys):
        gate = jax.nn.sigmoid(jnp.dot(h, wg_ref[:, i * d:(i + 1) * d], preferred_element_type=_F32))
        term = gate * jnp.dot(y_i.astype(_BF), wb_ref[i], preferred_element_type=_F32)
        m = term if m is None else m + term
    x_new = x + g1 * jnp.dot(m.astype(_BF), wo_ref[...], preferred_element_type=_F32)
    xo_ref[...] = x_new
    f_ref[...] = _rms_mod(x_new, gain2_ref[...], sh2, sc2).astype(_BF)


def fused_merge(x_tok, tile_mod, mod_tab, gain1, gain2, rw_parts, rw_norm, ys, w_gate, w_branch, w_out):
    t, d = x_tok.shape
    full2 = lambda i, idx: (0, 0)
    rows = lambda i, idx: (i, 0)
    branch = pl.BlockSpec((MERGE_ROWS, BRANCH_WIDTH), rows)
    grid_spec = pltpu.PrefetchScalarGridSpec(
        num_scalar_prefetch=1,
        grid=(t // MERGE_ROWS,),
        in_specs=[pl.BlockSpec((MERGE_ROWS, d), rows),
                  pl.BlockSpec((1, 1, mod_tab.shape[-1]), lambda i, idx: (idx[i], 0, 0)),
                  pl.BlockSpec((1, d), full2), pl.BlockSpec((1, d), full2)]
        + [branch] * 4 + [pl.BlockSpec(rw_norm.shape, full2)] + [branch] * (N_BRANCH - 1)
        + [pl.BlockSpec(w_gate.shape, full2),
           pl.BlockSpec(w_branch.shape, lambda i, idx: (0, 0, 0)),
           pl.BlockSpec(w_out.shape, full2)],
        out_specs=[pl.BlockSpec((MERGE_ROWS, d), rows), pl.BlockSpec((MERGE_ROWS, d), rows)])
    return pl.pallas_call(
        _merge_kernel,
        grid_spec=grid_spec,
        out_shape=[jax.ShapeDtypeStruct((t, d), _F32), jax.ShapeDtypeStruct((t, d), _BF)],
        compiler_params=pltpu.CompilerParams(
            dimension_semantics=("parallel",), vmem_limit_bytes=VMEM_LIMIT_BIG),
        name="merge_proj",
    )(tile_mod, x_tok, mod_tab, gain1[None], gain2[None], *rw_parts, rw_norm, *ys, w_gate, w_branch, w_out)


FFN_ROWS = 512
FFN_COLS = 512


def _swiglu_accumulate(x_ref, w1_ref, w3_ref, w2_ref, acc_ref):
    @pl.when(pl.program_id(1) == 0)
    def _():
        acc_ref[...] = jnp.zeros_like(acc_ref)

    x = x_ref[...]
    h1 = jnp.dot(x, w1_ref[0], preferred_element_type=_F32)
    h3 = jnp.dot(x, w3_ref[0], preferred_element_type=_F32)
    h = (h1 * jax.nn.sigmoid(h1) * h3).astype(_BF)
    acc_ref[...] += jnp.dot(h, w2_ref[0], preferred_element_type=_F32)


def pack_bf16_pairs(x):
    half = x.shape[-1] // 2
    bits = lax.bitcast_convert_type(x.astype(_F32), jnp.uint32)
    return bits[:, :half] | (bits[:, half:] >> 16)


def _moe_swiglu_kernel(be_ref, tok_ref, src_ref, w1_ref, w3_ref, w2_ref, g_ref, o_ref,
                       acc_ref, rows_ref, x_ref):
    del be_ref

    @pl.when(pl.program_id(1) == 0)
    def _():
        def gather(r, carry):
            rows_ref[pl.ds(r, 1), :] = src_ref[pl.ds(tok_ref[0, 0, r], 1), :]
            return carry

        lax.fori_loop(0, rows_ref.shape[0], gather, 0, unroll=8)
        packed = rows_ref[...]
        hi = lax.bitcast_convert_type(packed & jnp.uint32(0xFFFF0000), _F32)
        lo = lax.bitcast_convert_type(packed << 16, _F32)
        x_ref[...] = jnp.concatenate([hi, lo], axis=1).astype(_BF)

    _swiglu_accumulate(x_ref, w1_ref, w3_ref, w2_ref, acc_ref)

    @pl.when(pl.program_id(1) == pl.num_programs(1) - 1)
    def _():
        o_ref[...] = acc_ref[...] * g_ref[...]


def _ffn_residual_kernel(idx_ref, f_ref, x_ref, mod_ref, w1_ref, w3_ref, w2_ref, o_ref, acc_ref):
    del idx_ref
    _swiglu_accumulate(f_ref, w1_ref, w3_ref, w2_ref, acc_ref)

    @pl.when(pl.program_id(1) == pl.num_programs(1) - 1)
    def _():
        d = x_ref.shape[-1]
        o_ref[...] = x_ref[...] + mod_ref[0][:, 5 * d:6 * d] * acc_ref[...]


def ffn_residual(f_tok, x_tok, tile_mod, mod_tab, w1, w3, w2):
    t, d = x_tok.shape
    n_f = w1.shape[-1]
    rows = lambda i, f, idx: (i, 0)
    grid_spec = pltpu.PrefetchScalarGridSpec(
        num_scalar_prefetch=1,
        grid=(t // PROJ_ROWS, n_f // FFN_COLS),
        in_specs=[pl.BlockSpec((PROJ_ROWS, d), rows), pl.BlockSpec((PROJ_ROWS, d), rows),
                  pl.BlockSpec((1, 1, mod_tab.shape[-1]), lambda i, f, idx: (idx[i], 0, 0)),
                  pl.BlockSpec((1, d, FFN_COLS), lambda i, f, idx: (0, 0, f)),
                  pl.BlockSpec((1, d, FFN_COLS), lambda i, f, idx: (0, 0, f)),
                  pl.BlockSpec((1, FFN_COLS, d), lambda i, f, idx: (0, f, 0))],
        out_specs=pl.BlockSpec((PROJ_ROWS, d), rows),
        scratch_shapes=[pltpu.VMEM((PROJ_ROWS, d), _F32)])
    return pl.pallas_call(
        _ffn_residual_kernel,
        grid_spec=grid_spec,
        out_shape=jax.ShapeDtypeStruct((t, d), _F32),
        compiler_params=pltpu.CompilerParams(
            dimension_semantics=("parallel", "arbitrary"),
            vmem_limit_bytes=48 * 1024 * 1024),
        name="ffn_residual",
    )(tile_mod, f_tok, x_tok, mod_tab, w1.astype(_BF)[None], w3.astype(_BF)[None], w2.astype(_BF)[None])


def grouped_swiglu(h, tok, block_e, gate, w1, w3, w2):
    n_rows = tok.shape[0]
    d = h.shape[1]
    n_f = w1.shape[-1]
    assert n_rows % FFN_ROWS == 0 and n_f % FFN_COLS == 0
    src = pack_bf16_pairs(h)
    grid_spec = pltpu.PrefetchScalarGridSpec(
        num_scalar_prefetch=1,
        grid=(n_rows // FFN_ROWS, n_f // FFN_COLS),
        in_specs=[pl.BlockSpec((1, 1, FFN_ROWS), lambda i, f, be: (i, 0, 0), memory_space=pltpu.SMEM),
                  pl.BlockSpec(src.shape, lambda i, f, be: (0, 0), pipeline_mode=pl.Buffered(1)),
                  pl.BlockSpec((1, d, FFN_COLS), lambda i, f, be: (be[i], 0, f)),
                  pl.BlockSpec((1, d, FFN_COLS), lambda i, f, be: (be[i], 0, f)),
                  pl.BlockSpec((1, FFN_COLS, d), lambda i, f, be: (be[i], f, 0)),
                  pl.BlockSpec((FFN_ROWS, 1), lambda i, f, be: (i, 0))],
        out_specs=pl.BlockSpec((FFN_ROWS, d), lambda i, f, be: (i, 0)),
        scratch_shapes=[pltpu.VMEM((FFN_ROWS, d), _F32),
                        pltpu.VMEM((FFN_ROWS, d // 2), jnp.uint32),
                        pltpu.VMEM((FFN_ROWS, d), _BF)])
    return pl.pallas_call(
        _moe_swiglu_kernel,
        grid_spec=grid_spec,
        out_shape=jax.ShapeDtypeStruct((n_rows, d), _F32),
        compiler_params=pltpu.CompilerParams(
            dimension_semantics=("arbitrary", "arbitrary"), vmem_limit_bytes=VMEM_LIMIT_BIG),
        name="grouped_swiglu",
    )(block_e, tok.reshape(n_rows // FFN_ROWS, 1, FFN_ROWS), src,
      w1.astype(_BF), w3.astype(_BF), w2.astype(_BF), gate)


def moe_swiglu(h, w_router, w1, w3, w2):
    n_tok, d = h.shape
    logits = jnp.dot(h, w_router.astype(_BF), preferred_element_type=jnp.float32)
    top_logit, top_idx = lax.top_k(logits, TOP_K)
    gates = jax.nn.softmax(top_logit, axis=-1)
    n_assign = n_tok * TOP_K
    flat_e = top_idx.reshape(-1)
    order = jnp.argsort(flat_e)
    sorted_e = flat_e[order]
    sorted_tok = (order // TOP_K).astype(jnp.int32)
    sorted_gate = gates.reshape(-1)[order]
    counts = jnp.bincount(flat_e, length=N_EXPERTS)
    padded = (counts + FFN_ROWS - 1) // FFN_ROWS * FFN_ROWS
    start = jnp.cumsum(counts) - counts
    pend = jnp.cumsum(padded)
    pstart = pend - padded
    dest = pstart[sorted_e] + jnp.arange(n_assign) - start[sorted_e]
    n_blocks = -(-n_assign // FFN_ROWS) + N_EXPERTS
    n_rows = n_blocks * FFN_ROWS
    tok = jnp.zeros((n_rows,), jnp.int32).at[dest].set(sorted_tok)
    gate = jnp.zeros((n_rows,), _F32).at[dest].set(sorted_gate)
    block_e = jnp.minimum(jnp.searchsorted(pend, jnp.arange(n_blocks) * FFN_ROWS, side='right'),
                          N_EXPERTS - 1).astype(jnp.int32)
    yb = grouped_swiglu(h, tok, block_e, gate[:, None], w1, w3, w2)
    return jnp.zeros((n_tok, d), _F32).at[tok].add(yb)


def kernel(x, c, ctx, c_ctx, w_mod, b_mod, norm1, norm2, w_in, w_branch, w_out,
           rw_mu, rw_w0, rw_w1, rw_w2, rw_a0, rw_a1, rw_a2, rw_v0, rw_v1, rw_v2,
           rw_g1, rw_g2, rw_k_k, rw_k_a, rw_r_k, rw_ln_w, rw_ln_b, sw_sink,
           mla_q_norm, mla_w_uq, mla_kv_norm, mla_w_ukv, ga_q_norm, ga_k_norm,
           ffn_w1, ffn_w3, ffn_w2, moe_router, moe_w1, moe_w3, moe_w2, final_norm):
    b, s_len, d = x.shape
    n_ctx_tok = ctx.shape[1]
    n_c = b * n_ctx_tok
    assert n_c % PROJ_ROWS == 0 and s_len % PROJ_ROWS == 0
    rope64 = axial_rope_tables(s_len, HEAD_DIM)
    rope32 = axial_rope_tables(s_len, MLA_ROPE)
    silu_c = jax.nn.silu(c)
    silu_cc = jax.nn.silu(c_ctx)
    x_tok = jnp.concatenate([ctx.reshape(n_c, d), x.reshape(b * s_len, d)], axis=0)
    seg_rows = [n_c] + [s_len] * b

    def tile_table(rows):
        return jnp.concatenate([jnp.full((n // rows,), i, jnp.int32) for i, n in enumerate(seg_rows)])

    tile_mod_proj, tile_mod_merge = tile_table(PROJ_ROWS), tile_table(MERGE_ROWS)

    def split_tokens(t):
        return t[:n_c].reshape(b, n_ctx_tok, -1), t[n_c:].reshape(b, s_len, -1)

    vf_tok = None
    for l in range(DEPTH):
        ctx_out = l < DEPTH - 1
        mod_l = silu_c @ w_mod[l] + b_mod[l]
        mod_c = silu_cc @ w_mod[l] + b_mod[l]
        mod_tab = jnp.concatenate([mod_c[None], mod_l], axis=0)[:, None, :]

        w_groups = []
        for off, width in PROJ_GROUPS:
            w = w_in[l][:, off:off + width].astype(_BF)
            w_groups.append(jnp.pad(w, ((0, 0), (0, _round_up(width, LANES) - width))))
        proj = fused_in_proj(x_tok, tile_mod_proj, mod_tab, norm1[l], w_groups)
        p_rw = proj[0]
        pc, pl_ = [None], [None]
        for p, (_, width) in zip(proj[1:], PROJ_GROUPS[1:]):
            p_ctx, p_lat = split_tokens(p[:, :width])
            pc.append(p_ctx)
            pl_.append(p_lat)
        rw = dict(mu=rw_mu[l], w0=rw_w0[l], w1=rw_w1[l], w2=rw_w2[l],
                  a0=rw_a0[l], a1=rw_a1[l], a2=rw_a2[l],
                  v0=rw_v0[l - 1] if l > 0 else None,
                  v1=rw_v1[l - 1] if l > 0 else None,
                  v2=rw_v2[l - 1] if l > 0 else None,
                  g1=rw_g1[l], g2=rw_g2[l], k_k=rw_k_k[l], k_a=rw_k_a[l], r_k=rw_r_k[l],
                  ln_w=rw_ln_w[l], ln_b=rw_ln_b[l])
        rw_yf, rw_yb, rw_bonus, rw_gate, rw_v = rwkv_branch(p_rw, vf_tok, rw, b, n_ctx_tok, s_len)
        if l == 0:
            vf_tok = rw_v
        yb_c, yb_l = window_mixer(pc[1:4], pl_[1:4], sw_sink[l], rope64, ctx_out)
        yc_c, yc_l = mla_mixer(pc[4:7], pl_[4:7], mla_q_norm[l], mla_w_uq[l], mla_kv_norm[l],
                               mla_w_ukv[l], rope32, ctx_out)
        yd_c, yd_l = grid_attention_mixer(pc[7:10], pl_[7:10], ga_q_norm[l], ga_k_norm[l], rope64, ctx_out)

        def tokens_of(y_ctx, y_lat):
            y_ctx = jnp.zeros((n_c, BRANCH_WIDTH), _F32) if y_ctx is None else y_ctx.reshape(n_c, BRANCH_WIDTH)
            return jnp.concatenate([y_ctx, y_lat.reshape(b * s_len, BRANCH_WIDTH)], axis=0)

        ys = [tokens_of(yc_, yl_) for yc_, yl_ in ((yb_c, yb_l), (yc_c, yc_l), (yd_c, yd_l))]
        x_tok, f_tok = fused_merge(x_tok, tile_mod_merge, mod_tab, norm1[l], norm2[l],
                                   (rw_yf, rw_yb, rw_bonus, rw_gate),
                                   jnp.stack([rw_ln_w[l], rw_ln_b[l]]), ys,
                                   w_in[l][:, MIXER_COLS:].astype(_BF), w_branch[l].astype(_BF),
                                   w_out[l].astype(_BF))
        if l % 2 == 0:
            x_tok = ffn_residual(f_tok, x_tok, tile_mod_proj, mod_tab,
                                 ffn_w1[l // 2], ffn_w3[l // 2], ffn_w2[l // 2])
        else:
            first = 0 if ctx_out else n_c
            ffn_out = moe_swiglu(f_tok[first:], moe_router[l // 2], moe_w1[l // 2], moe_w3[l // 2],
                                 moe_w2[l // 2])
            pieces, row = [], 0
            for i, n in enumerate(seg_rows):
                if row >= first:
                    g2 = mod_tab[i, :, 5 * d:]
                    pieces.append(x_tok[row:row + n] + g2 * ffn_out[row - first:row - first + n])
                else:
                    pieces.append(x_tok[row:row + n])
                row += n
            x_tok = jnp.concatenate(pieces, axis=0)
    return rms_norm(x_tok[n_c:].reshape(b, s_len, d), final_norm)
```

```python
import functools

import jax
import jax.numpy as jnp
from jax import lax
from jax.experimental import pallas as pl
from jax.experimental.pallas import tpu as pltpu

D_MODEL = 1024
DEPTH = 4
GRID_W = 64
HEAD_DIM = 64
WINDOW = 128
ROPE_BASE = 10000.0
NORM_EPS = 1e-6
NEG_INF = -1e30
N_BRANCH = 4
BRANCH_WIDTH = 256
RW_HEADS = 4
RW_WIDTH = RW_HEADS * HEAD_DIM
RW_GN_EPS = 64e-5
SW_HEADS = 4
SW_KV_HEADS = 2
MLA_HEADS = 4
MLA_NOPE = 64
MLA_ROPE = 32
MLA_V = 64
MLA_Q_RANK = 256
MLA_KV_RANK = 128
GA_HEADS = 4
GA_KV_HEADS = 2
N_EXPERTS = 8
TOP_K = 2
IN_WIDTHS = (4 * RW_WIDTH,
             SW_HEADS * HEAD_DIM, SW_KV_HEADS * HEAD_DIM, SW_KV_HEADS * HEAD_DIM,
             MLA_Q_RANK, MLA_KV_RANK, MLA_ROPE,
             GA_HEADS * HEAD_DIM, GA_KV_HEADS * HEAD_DIM, GA_KV_HEADS * HEAD_DIM,
             N_BRANCH * D_MODEL)


def _mm_kernel(a_ref, b_ref, o_ref):
    o_ref[...] = jnp.dot(a_ref[...], b_ref[...], preferred_element_type=jnp.float32)


def _round_up(x, m):
    return (x + m - 1) // m * m


def pmm(a, b, tm=512, tn=512):
    m, k = a.shape
    _, n = b.shape
    tm = min(tm, _round_up(m, 8))
    tn = min(tn, _round_up(n, 128))
    mp, np_ = _round_up(m, tm), _round_up(n, tn)
    a = a.astype(jnp.bfloat16)
    b = b.astype(jnp.bfloat16)
    if mp != m:
        a = jnp.pad(a, ((0, mp - m), (0, 0)))
    if np_ != n:
        b = jnp.pad(b, ((0, 0), (0, np_ - n)))
    out = pl.pallas_call(
        _mm_kernel,
        grid=(np_ // tn, mp // tm),
        in_specs=[pl.BlockSpec((tm, k), lambda j, i: (i, 0)),
                  pl.BlockSpec((k, tn), lambda j, i: (0, j))],
        out_specs=pl.BlockSpec((tm, tn), lambda j, i: (i, j)),
        out_shape=jax.ShapeDtypeStruct((mp, np_), jnp.float32),
        compiler_params=pltpu.CompilerParams(
            dimension_semantics=("arbitrary", "arbitrary"),
            vmem_limit_bytes=48 * 1024 * 1024),
        name="pmm",
    )(a, b)
    return out[:m, :n]


def pmm_nd(a, b, **kw):
    lead = a.shape[:-1]
    return pmm(a.reshape(-1, a.shape[-1]), b, **kw).reshape(lead + (b.shape[-1],))


RW_CHUNK = 64
RW_SUMMARY_CHUNKS = 8
RW_CHUNKS_PER_STEP = 4
RW_BLOCK = RW_CHUNK * RW_CHUNKS_PER_STEP
RW_INV_BASE = 4

_BF = jnp.bfloat16
_F32 = jnp.float32


def _bd_rows(x):
    lane_head = lax.broadcasted_iota(jnp.int32, x.shape, 1) // HEAD_DIM
    return jnp.concatenate([jnp.where(lane_head == h, x, 0.0) for h in range(RW_HEADS)], axis=0)


def _dot(a, b):
    return jnp.dot(a.astype(_BF), b.astype(_BF), preferred_element_type=_F32)


def _dot_nt(a, b):
    return lax.dot_general(a.astype(_BF), b.astype(_BF), (((1,), (1,)), ((), ())),
                           preferred_element_type=_F32)


def _dot_tn(a, b):
    return lax.dot_general(a.astype(_BF), b.astype(_BF), (((0,), (0,)), ((), ())),
                           preferred_element_type=_F32)


def _split3(x):
    h1 = x.astype(_BF)
    r1 = x - h1.astype(_F32)
    h2 = r1.astype(_BF)
    h3 = (r1 - h2.astype(_F32)).astype(_BF)
    return h1, h2, h3


def _each(fn, *lists):
    return [fn(*args) for args in zip(*lists)]


def _chunk_summaries(r, lw, k, v, a, b, reverse):
    c = RW_CHUNK
    row = lax.broadcasted_iota(jnp.int32, (c, c), 0)
    col = lax.broadcasted_iota(jnp.int32, (c, c), 1)
    tri = ((row <= col) if reverse else (row >= col)).astype(_BF)
    cum = _each(lambda x: sum(jnp.dot(tri, p, preferred_element_type=_F32) for p in _split3(x)), lw)
    lwlast = _each(lambda x: x[0:1, :] if reverse else x[c - 1:c, :], cum)
    rt = _each(lambda x, cu: x * jnp.exp(cu), r, cum)
    at = _each(lambda x, cu, l: x * jnp.exp(cu - l), a, cum, lw)
    w_inv = _each(lambda cu: jnp.exp(-cu), cum)
    w_tail = _each(lambda cu, ll: jnp.exp(ll - cu), cum, lwlast)
    wlast = _each(jnp.exp, lwlast)
    bt = _each(jnp.multiply, b, w_inv)
    kt = _each(jnp.multiply, k, w_inv)
    bh = _each(jnp.multiply, b, w_tail)
    kh = _each(jnp.multiply, k, w_tail)

    t_idx = lax.broadcasted_iota(jnp.int32, (c, RW_WIDTH), 0)
    s_idx = lax.broadcasted_iota(jnp.int32, (c, RW_WIDTH), 1) % HEAD_DIM
    strict = (s_idx > t_idx) if reverse else (s_idx < t_idx)
    incl = (s_idx >= t_idx) if reverse else (s_idx <= t_idx)

    l1 = _each(lambda x, y: jnp.concatenate([x, y], axis=0), at, rt)
    g_b = _each(lambda x, y: _dot_nt(x, _bd_rows(y)), l1, bt)
    g_k = _each(lambda x, y: _dot_nt(x, _bd_rows(y)), l1, kt)
    a_ab = _each(lambda g: jnp.where(strict, g[:c], 0.0), g_b)
    a_rb = _each(lambda g: jnp.where(incl, g[c:], 0.0), g_b)
    a_ak = _each(lambda g: jnp.where(strict, g[:c], 0.0), g_k)
    a_rk = _each(lambda g: jnp.where(incl, g[c:], 0.0), g_k)

    eye = jnp.where(s_idx == t_idx, 1.0, 0.0)
    base = RW_INV_BASE
    n_b = _each(lambda x: jnp.where(t_idx // base == s_idx // base, x, 0.0), a_ab)
    n_b2 = _each(lambda x: _dot(x, _bd_rows(x)), n_b)
    n_b3 = _each(lambda x, y: _dot(x, _bd_rows(y)), n_b, n_b2)
    tm = _each(lambda x, y, z: eye + x + y + z, n_b, n_b2, n_b3)
    blk = base
    while blk < c:
        off = (t_idx // (2 * blk) == s_idx // (2 * blk)) & (t_idx // blk != s_idx // blk)
        x = _each(lambda t, n: _dot(t, _bd_rows(jnp.where(off, n, 0.0))), tm, a_ab)
        tm = _each(lambda t, xx: t + _dot(xx, _bd_rows(t)), tm, x)
        blk *= 2

    av = _each(lambda x, y, vv: _dot(jnp.concatenate([x, y], axis=0), _bd_rows(vv)), a_ak, a_rk, v)
    ta = _each(lambda t, x: _dot(t, _bd_rows(x)), tm, at)
    u0 = _each(lambda t, x: _dot(t, _bd_rows(x[:c])), tm, av)
    r1 = _each(lambda x, ar, t: x + _dot(ar, _bd_rows(t)), rt, a_rb, ta)
    y0 = _each(lambda x, ar, u: x[c:] + _dot(ar, _bd_rows(u)), av, a_rb, u0)

    blk_r = lax.broadcasted_iota(jnp.int32, (RW_WIDTH, RW_WIDTH), 0) // HEAD_DIM
    blk_c = lax.broadcasted_iota(jnp.int32, (RW_WIDTH, RW_WIDTH), 1) // HEAD_DIM
    m_bd = _each(lambda t, x: jnp.where(blk_r == blk_c, _dot_tn(t, x), 0.0), ta, bh)
    z = _each(lambda u, vv, x, y: _dot_tn(jnp.concatenate([u, vv], axis=0),
                                          jnp.concatenate([x, y], axis=0)), u0, v, bh, kh)
    lane_head = lax.broadcasted_iota(jnp.int32, (HEAD_DIM, RW_WIDTH), 1) // HEAD_DIM

    def diag_blocks(zz):
        out = zz[:HEAD_DIM]
        for h in range(1, RW_HEADS):
            out = jnp.where(lane_head == h, zz[h * HEAD_DIM:(h + 1) * HEAD_DIM], out)
        return out

    sadd = _each(diag_blocks, z)
    return r1, y0, m_bd, sadd, wlast


def _rw_summary_kernel(r_ref, lw_ref, k_ref, v_ref, a_ref, b_ref,
                       r1_ref, y0_ref, m_ref, sadd_ref, wl_ref, *, reverse):
    c = RW_CHUNK

    def chunks(ref):
        return [ref[pl.ds(j * c, c), :] for j in range(RW_SUMMARY_CHUNKS)]

    r1, y0, m_bd, sadd, wlast = _chunk_summaries(
        chunks(r_ref), chunks(lw_ref), chunks(k_ref), chunks(v_ref), chunks(a_ref), chunks(b_ref),
        reverse)
    r1_ref[...] = jnp.concatenate(r1, axis=0)
    y0_ref[...] = jnp.concatenate(y0, axis=0)
    m_ref[...] = jnp.stack(m_bd, axis=0).astype(_BF)
    sadd_ref[...] = jnp.stack(sadd, axis=0)
    wl_ref[...] = jnp.stack(wlast, axis=0)


def rw_summaries(r_src, lw, k, v, a, b, reverse):
    t, w = lw.shape
    c, cps = RW_CHUNK, RW_SUMMARY_CHUNKS
    assert t % (c * cps) == 0 and w == RW_WIDTH
    rows = pl.BlockSpec((c * cps, w), lambda i: (i, 0))
    return pl.pallas_call(
        functools.partial(_rw_summary_kernel, reverse=reverse),
        grid=(t // (c * cps),),
        in_specs=[rows] * 6,
        out_specs=[rows, rows,
                   pl.BlockSpec((cps, w, w), lambda i: (i, 0, 0)),
                   pl.BlockSpec((cps, HEAD_DIM, w), lambda i: (i, 0, 0)),
                   pl.BlockSpec((cps, 1, w), lambda i: (i, 0, 0))],
        out_shape=[jax.ShapeDtypeStruct((t, w), _F32),
                   jax.ShapeDtypeStruct((t, w), _F32),
                   jax.ShapeDtypeStruct((t // c, w, w), _BF),
                   jax.ShapeDtypeStruct((t // c, HEAD_DIM, w), _F32),
                   jax.ShapeDtypeStruct((t // c, 1, w), _F32)],
        compiler_params=pltpu.CompilerParams(
            dimension_semantics=("parallel",), vmem_limit_bytes=48 * 1024 * 1024),
        name="rw_chunk_summary",
    )(r_src, lw, k, v, a, b)


def _rw_state_kernel(*refs):
    s_ref = refs[-1]
    y_refs = refs[10:12]
    c, cps = RW_CHUNK, RW_CHUNKS_PER_STEP

    @pl.when(pl.program_id(1) == 0)
    def _():
        s_ref[...] = jnp.zeros_like(s_ref)

    s = [s_ref[0], s_ref[1]]
    ys = [[None] * cps, [None] * cps]
    for step in range(cps):
        js = (step, cps - 1 - step)
        r1, y0 = ([refs[5 * d + n][pl.ds(js[d] * c, c), :] for d in range(2)] for n in range(2))
        m_bd, sadd, wlast = ([refs[5 * d + n][js[d]] for d in range(2)] for n in range(2, 5))
        y = _each(lambda rr, ss, yy: _dot_nt(rr, _bd_rows(ss)) + yy, r1, s, y0)
        s = _each(lambda ss, ww, mm, aa: ss * ww + _dot(ss, mm) + aa, s, wlast, m_bd, sadd)
        for d in range(2):
            ys[d][js[d]] = y[d]
    for d in range(2):
        y_refs[d][...] = jnp.concatenate(ys[d], axis=0)
        s_ref[d] = s[d]


def rw_state_pass(sum_fwd, sum_bwd, n_batch, n_ctx, n_lat):
    t, w = sum_fwd[0].shape
    cps = RW_CHUNKS_PER_STEP
    assert n_ctx % RW_BLOCK == 0 and n_lat % RW_BLOCK == 0
    n_cb, n_lb = n_ctx // RW_BLOCK, n_lat // RW_BLOCK
    lat0 = n_batch * n_cb

    def fwd_blk(q, i):
        return jnp.where(i < n_cb, q * n_cb + i, lat0 + q * n_lb + (i - n_cb))

    def bwd_blk(q, i):
        return jnp.where(i < n_cb, q * n_cb + (n_cb - 1 - i), lat0 + q * n_lb + (n_lb - 1 - (i - n_cb)))

    def specs(blk):
        return [pl.BlockSpec((RW_BLOCK, w), lambda q, i: (blk(q, i), 0)),
                pl.BlockSpec((RW_BLOCK, w), lambda q, i: (blk(q, i), 0)),
                pl.BlockSpec((cps, w, w), lambda q, i: (blk(q, i), 0, 0)),
                pl.BlockSpec((cps, HEAD_DIM, w), lambda q, i: (blk(q, i), 0, 0)),
                pl.BlockSpec((cps, 1, w), lambda q, i: (blk(q, i), 0, 0))]

    return pl.pallas_call(
        _rw_state_kernel,
        grid=(n_batch, n_cb + n_lb),
        in_specs=specs(fwd_blk) + specs(bwd_blk),
        out_specs=[pl.BlockSpec((RW_BLOCK, w), lambda q, i: (fwd_blk(q, i), 0)),
                   pl.BlockSpec((RW_BLOCK, w), lambda q, i: (bwd_blk(q, i), 0))],
        out_shape=[jax.ShapeDtypeStruct((t, w), _F32)] * 2,
        scratch_shapes=[pltpu.VMEM((2, HEAD_DIM, w), _F32)],
        compiler_params=pltpu.CompilerParams(
            dimension_semantics=("arbitrary", "arbitrary"),
            vmem_limit_bytes=48 * 1024 * 1024),
        name="rw_state_pass",
    )(*sum_fwd, *sum_bwd)


ATTN_TQ = 256
ATTN_TK = 768


LANES = 128
LOG2_E = 1.4426950408889634


def _dense_attn_kernel(sink_ref, q_ref, k_ref, v_ref, o_ref, *, tk):
    g, tq, dq = q_ref.shape[2:]
    dv = o_ref.shape[-1]
    n_keys = k_ref.shape[2]
    q = q_ref[0, 0].reshape(g * tq, dq)
    grp = lax.broadcasted_iota(jnp.int32, (g * tq, 1), 0) // tq
    sink = jnp.full((g * tq, 1), -jnp.inf, _F32)
    for j in range(g):
        sink = jnp.where(grp == j, sink_ref[pl.program_id(1) * g + j], sink)

    def scores(i):
        kc = k_ref[0, 0, pl.ds(i * tk, tk), :]
        return lax.dot_general(q, kc, (((1,), (1,)), ((), ())), preferred_element_type=_F32)

    def update(i, m, acc, s):
        vc = v_ref[0, 0, pl.ds(i * tk, tk), :]
        m_new = jnp.maximum(m, jnp.max(s, axis=-1, keepdims=True))
        p = jnp.exp2(s - m_new).astype(_BF)
        return m_new, jnp.exp2(m - m_new) * acc + jnp.dot(p, vc, preferred_element_type=_F32)

    n_chunks = n_keys // tk
    m = sink
    acc = jnp.zeros((g * tq, v_ref.shape[-1]), _F32)
    s = scores(0)
    for i in range(n_chunks):
        s_next = scores(i + 1) if i + 1 < n_chunks else None
        m, acc = update(i, m, acc, s)
        s = s_next
    total = acc[:, dv:dv + 1] + jnp.exp2(sink - m)
    o_ref[0, 0] = (acc[:, :dv] / total).reshape(g, tq, dv)


def _with_ones_column(v):
    dv = v.shape[-1]
    return jnp.concatenate([v, jnp.ones_like(v[..., :1]),
                            jnp.zeros(v.shape[:-1] + (LANES - dv - 1,), v.dtype)], axis=-1).astype(_BF)


def split_heads(t, n_heads):
    d = t.shape[-1] // n_heads
    return jnp.stack([t[..., h * d:(h + 1) * d] for h in range(n_heads)], axis=1)


def merge_heads(o):
    return jnp.concatenate([o[:, h, j] for h in range(o.shape[1]) for j in range(o.shape[2])], axis=-1)


def dense_block_attention(q, k, v, scale, sink=None):
    b, hk, g, s_len, dq = q.shape
    n_keys, dv = k.shape[2], v.shape[-1]
    tq = min(ATTN_TQ, s_len)
    tk = ATTN_TK if n_keys % ATTN_TK == 0 else n_keys
    assert s_len % tq == 0
    qt = (q * (scale * LOG2_E)).astype(_BF)
    kt = k.astype(_BF)
    vt = _with_ones_column(v)
    sink = jnp.full((hk * g,), -jnp.inf, _F32) if sink is None else (sink.astype(_F32) * LOG2_E).reshape(-1)
    grid_spec = pltpu.PrefetchScalarGridSpec(
        num_scalar_prefetch=1,
        grid=(b, hk, s_len // tq),
        in_specs=[pl.BlockSpec((1, 1, g, tq, dq), lambda bi, h, i, sk: (bi, h, 0, i, 0)),
                  pl.BlockSpec((1, 1, n_keys, dq), lambda bi, h, i, sk: (bi, h, 0, 0)),
                  pl.BlockSpec((1, 1, n_keys, LANES), lambda bi, h, i, sk: (bi, h, 0, 0))],
        out_specs=pl.BlockSpec((1, 1, g, tq, dv), lambda bi, h, i, sk: (bi, h, 0, i, 0)))
    o = pl.pallas_call(
        functools.partial(_dense_attn_kernel, tk=tk),
        grid_spec=grid_spec,
        out_shape=jax.ShapeDtypeStruct((b, hk, g, s_len, dv), _F32),
        compiler_params=pltpu.CompilerParams(
            dimension_semantics=("parallel", "parallel", "parallel"),
            vmem_limit_bytes=48 * 1024 * 1024),
        name="dense_attention",
    )(sink, qt, kt, vt)
    return merge_heads(o)


def rms_norm(x, g):
    xf = x.astype(jnp.float32)
    y = xf * lax.rsqrt(jnp.mean(xf * xf, axis=-1, keepdims=True) + NORM_EPS)
    return (y * g.astype(jnp.float32)).astype(x.dtype)


def axial_rope_tables(n_tokens, rot_dim):
    rows = n_tokens // GRID_W
    row = jnp.repeat(jnp.arange(rows, dtype=jnp.float32), GRID_W)
    col = jnp.tile(jnp.arange(GRID_W, dtype=jnp.float32), rows)
    n_freq = rot_dim // 4
    inv_freq = ROPE_BASE ** (-jnp.arange(n_freq, dtype=jnp.float32) / n_freq)
    ang = jnp.concatenate([row[:, None] * inv_freq, col[:, None] * inv_freq], axis=-1)
    return jnp.cos(ang), jnp.sin(ang)


def apply_rope(x, cos, sin):
    half = x.shape[-1] // 2
    c = cos[None, None].astype(x.dtype)
    s = sin[None, None].astype(x.dtype)
    x1, x2 = x[..., :half], x[..., half:]
    return jnp.concatenate([x1 * c - x2 * s, x1 * s + x2 * c], axis=-1)


def context_attention(q, k, v, scale, sink=None):
    return dense_block_attention(q, k, v, scale, sink)


WIN_TQ = 256
WIN_SPAN = WIN_TQ + 2 * WINDOW


def _window_attn_kernel(sink_ref, q_ref, k_ref, v_ref, kc_ref, vc_ref, o_ref):
    g, tq, _ = q_ref.shape[2:]
    dv = o_ref.shape[-1]
    n_lat = k_ref.shape[2]
    h, i = pl.program_id(1), pl.program_id(2)
    q = q_ref[0, 0].reshape(g * tq, q_ref.shape[-1])
    start = pl.multiple_of(jnp.clip(i * tq - WINDOW, 0, n_lat - WIN_SPAN), WINDOW)
    kw = k_ref[0, 0, pl.ds(start, WIN_SPAN), :]
    vw = v_ref[0, 0, pl.ds(start, WIN_SPAN), :]
    nt = (((1,), (1,)), ((), ()))
    s_win = lax.dot_general(q, kw, nt, preferred_element_type=_F32)
    s_ctx = lax.dot_general(q, kc_ref[0, 0], nt, preferred_element_type=_F32)
    row = lax.broadcasted_iota(jnp.int32, s_win.shape, 0)
    q_pos = i * tq + row % tq
    k_pos = start + lax.broadcasted_iota(jnp.int32, s_win.shape, 1)
    s_win = jnp.where(jnp.abs(q_pos - k_pos) <= WINDOW, s_win, NEG_INF)
    grp = lax.broadcasted_iota(jnp.int32, (g * tq, 1), 0) // tq
    sink = jnp.zeros((g * tq, 1), _F32)
    for j in range(g):
        sink = jnp.where(grp == j, sink_ref[h * g + j], sink)
    m = jnp.maximum(jnp.maximum(jnp.max(s_win, axis=-1, keepdims=True),
                                jnp.max(s_ctx, axis=-1, keepdims=True)), sink)
    acc = (jnp.dot(jnp.exp2(s_win - m).astype(_BF), vw, preferred_element_type=_F32)
           + jnp.dot(jnp.exp2(s_ctx - m).astype(_BF), vc_ref[0, 0], preferred_element_type=_F32))
    total = acc[:, dv:dv + 1] + jnp.exp2(sink - m)
    o_ref[0, 0] = (acc[:, :dv] / total).reshape(g, tq, dv)


def banded_window_attention(q, k, v, k_ctx, v_ctx, sink, scale):
    b, hk, g, s_len, d = q.shape
    n_ctx = k_ctx.shape[2]
    assert s_len % WIN_TQ == 0 and s_len >= WIN_SPAN
    qt = (q * (scale * LOG2_E)).astype(_BF)
    kt, kct = k.astype(_BF), k_ctx.astype(_BF)
    vt, vct = _with_ones_column(v), _with_ones_column(v_ctx)
    kv_map = lambda bi, h, i, sk: (bi, h, 0, 0)
    grid_spec = pltpu.PrefetchScalarGridSpec(
        num_scalar_prefetch=1,
        grid=(b, hk, s_len // WIN_TQ),
        in_specs=[pl.BlockSpec((1, 1, g, WIN_TQ, d), lambda bi, h, i, sk: (bi, h, 0, i, 0)),
                  pl.BlockSpec((1, 1, s_len, d), kv_map),
                  pl.BlockSpec((1, 1, s_len, LANES), kv_map),
                  pl.BlockSpec((1, 1, n_ctx, d), kv_map),
                  pl.BlockSpec((1, 1, n_ctx, LANES), kv_map)],
        out_specs=pl.BlockSpec((1, 1, g, WIN_TQ, d), lambda bi, h, i, sk: (bi, h, 0, i, 0)))
    o = pl.pallas_call(
        _window_attn_kernel,
        grid_spec=grid_spec,
        out_shape=jax.ShapeDtypeStruct((b, hk, g, s_len, d), _F32),
        compiler_params=pltpu.CompilerParams(
            dimension_semantics=("parallel", "parallel", "parallel"),
            vmem_limit_bytes=48 * 1024 * 1024),
        name="window_attention",
    )((sink.astype(_F32) * LOG2_E).reshape(-1), qt, kt, vt, kct, vct)
    return merge_heads(o)


def gqa_heads(q, k, v, n_heads, n_kv, rope=None, q_gain=None, k_gain=None):
    b, n, _ = q.shape
    q, k, v = split_heads(q, n_heads), split_heads(k, n_kv), split_heads(v, n_kv)
    if q_gain is not None:
        q = rms_norm(q, q_gain)
        k = rms_norm(k, k_gain)
    if rope is not None:
        q = apply_rope(q, rope[0], rope[1])
        k = apply_rope(k, rope[0], rope[1])
    return q.reshape(b, n_kv, n_heads // n_kv, n, HEAD_DIM), k, v


def window_mixer(pc, pl_, sink, rope, ctx_out):
    scale = HEAD_DIM ** -0.5
    qc, kc, vc = gqa_heads(pc[0], pc[1], pc[2], SW_HEADS, SW_KV_HEADS)
    ql, kl, vl = gqa_heads(pl_[0], pl_[1], pl_[2], SW_HEADS, SW_KV_HEADS, rope)
    sink = sink.reshape(SW_KV_HEADS, SW_HEADS // SW_KV_HEADS)
    y_lat = banded_window_attention(ql, kl, vl, kc, vc, sink, scale)
    y_ctx = context_attention(qc, kc, vc, scale, sink) if ctx_out else None
    return y_ctx, y_lat


def grid_attention_mixer(pc, pl_, q_gain, k_gain, rope, ctx_out):
    scale = HEAD_DIM ** -0.5
    qc, kc, vc = gqa_heads(pc[0], pc[1], pc[2], GA_HEADS, GA_KV_HEADS, None, q_gain, k_gain)
    ql, kl, vl = gqa_heads(pl_[0], pl_[1], pl_[2], GA_HEADS, GA_KV_HEADS, rope, q_gain, k_gain)
    y_lat = dense_block_attention(ql, jnp.concatenate([kc, kl], axis=2), jnp.concatenate([vc, vl], axis=2), scale)
    y_ctx = context_attention(qc, kc, vc, scale) if ctx_out else None
    return y_ctx, y_lat


def mla_mixer(pc, pl_, q_gain, w_uq, kv_gain, w_ukv, rope, ctx_out):
    scale = (MLA_NOPE + MLA_ROPE) ** -0.5

    def project(cq, ckv, k_rope, rope_tab):
        b, n, _ = cq.shape
        q = split_heads(pmm_nd(rms_norm(cq, q_gain), w_uq), MLA_HEADS)
        kv = split_heads(pmm_nd(rms_norm(ckv, kv_gain), w_ukv), MLA_HEADS)
        q_nope, q_pe = q[..., :MLA_NOPE], q[..., MLA_NOPE:]
        k_nope, v = kv[..., :MLA_NOPE], kv[..., MLA_NOPE:]
        k_pe = k_rope[:, None, :, :]
        if rope_tab is not None:
            q_pe = apply_rope(q_pe, rope_tab[0], rope_tab[1])
            k_pe = apply_rope(k_pe, rope_tab[0], rope_tab[1])
        q = jnp.concatenate([q_nope, q_pe], axis=-1)
        k = jnp.concatenate([k_nope, jnp.broadcast_to(k_pe, (b, MLA_HEADS, n, MLA_ROPE))], axis=-1)
        return q[:, :, None], k, v

    qc, kc, vc = project(pc[0], pc[1], pc[2], None)
    ql, kl, vl = project(pl_[0], pl_[1], pl_[2], rope)
    y_lat = dense_block_attention(ql, jnp.concatenate([kc, kl], axis=2), jnp.concatenate([vc, vl], axis=2), scale)
    y_ctx = context_attention(qc, kc, vc, scale) if ctx_out else None
    return y_ctx, y_lat


def centred_token_shift(f, mu):
    prev = jnp.pad(f[:, :-1], ((0, 0), (1, 0), (0, 0)))
    nxt = jnp.pad(f[:, 1:], ((0, 0), (0, 1), (0, 0)))
    return f + mu[0] * (prev - f) + mu[1] * (nxt - f)


def _head_sums(x):
    w = x.shape[-1]
    blk_r = lax.broadcasted_iota(jnp.int32, (w, w), 0) // HEAD_DIM
    blk_c = lax.broadcasted_iota(jnp.int32, (w, w), 1) // HEAD_DIM
    ones_bd = (blk_r == blk_c).astype(_BF)
    hi = x.astype(_BF)
    lo = (x - hi.astype(_F32)).astype(_BF)
    return (jnp.dot(hi, ones_bd, preferred_element_type=_F32)
            + jnp.dot(lo, ones_bd, preferred_element_type=_F32))


RW_FEATURE_ROWS = 512
_VEC_V0, _VEC_W0, _VEC_A0, _VEC_KK, _VEC_KA, _VEC_RK = 0, 1, 3, 5, 6, 7


def _rw_feature_kernel(f_ref, vf_ref, vec_ref, v1_ref, v2_ref, g1_ref, g2_ref, w1_ref, w2_ref,
                       a1_ref, a2_ref, v_ref, nkk_ref, g_ref, bonus_ref,
                       lw0_ref, kd0_ref, b0_ref, lw1_ref, kd1_ref, b1_ref, *, value_residual):
    w = RW_WIDTH
    f = f_ref[...]
    r, k, v, z = (f[:, j * w:(j + 1) * w] for j in range(4))
    vec = vec_ref[...]

    def row(i):
        return vec[i:i + 1, :]

    zb = z.astype(_BF)

    def lora(a, b, act=None):
        h = jnp.dot(zb, a.astype(_BF), preferred_element_type=_F32)
        h = h if act is None else act(h)
        return jnp.dot(h.astype(_BF), b.astype(_BF), preferred_element_type=_F32)

    if value_residual:
        v = v + (vf_ref[...] - v) * jax.nn.sigmoid(row(_VEC_V0) + lora(v1_ref[...], v2_ref[...]))
    kk = k * row(_VEC_KK)
    kk = kk * lax.rsqrt(_head_sums(kk * kk) + 1e-12)
    bonus = jnp.zeros_like(v)
    for d, (lw_ref, kd_ref, b_ref) in enumerate(((lw0_ref, kd0_ref, b0_ref), (lw1_ref, kd1_ref, b1_ref))):
        wl = -jax.nn.softplus(-(row(_VEC_W0 + d) + lora(w1_ref[d], w2_ref[d], jnp.tanh))) - 0.5
        a = jax.nn.sigmoid(row(_VEC_A0 + d) + lora(a1_ref[d], a2_ref[d]))
        k_d = k * (1 + (a - 1) * row(_VEC_KA))
        bonus = bonus + _head_sums(r * k_d * row(_VEC_RK)) * v
        lw_ref[...] = -jnp.exp(wl)
        kd_ref[...] = k_d
        b_ref[...] = kk * a
    v_ref[...] = v
    nkk_ref[...] = -kk
    g_ref[...] = lora(g1_ref[...], g2_ref[...], jax.nn.sigmoid)
    bonus_ref[...] = bonus


def rw_features(f_tok, vf_tok, rw, value_residual):
    t = f_tok.shape[0]
    w = RW_WIDTH
    zeros = jnp.zeros((w,), _F32)
    vec = jnp.stack([rw['v0'] if value_residual else zeros, rw['w0'][0], rw['w0'][1],
                     rw['a0'][0], rw['a0'][1], rw['k_k'], rw['k_a'], rw['r_k']])
    if value_residual:
        v1, v2 = rw['v1'], rw['v2']
    else:
        vf_tok = f_tok
        v1, v2 = jnp.zeros((w, 8), _F32), jnp.zeros((8, w), _F32)
    weights = [v1, v2, rw['g1'], rw['g2'], rw['w1'], rw['w2'], rw['a1'], rw['a2']]

    def full(a):
        return pl.BlockSpec(a.shape, lambda i, n=a.ndim: (0,) * n)

    rows = pl.BlockSpec((RW_FEATURE_ROWS, w), lambda i: (i, 0))
    return pl.pallas_call(
        functools.partial(_rw_feature_kernel, value_residual=value_residual),
        grid=(t // RW_FEATURE_ROWS,),
        in_specs=[pl.BlockSpec((RW_FEATURE_ROWS, 4 * w), lambda i: (i, 0)), rows, full(vec)]
        + [full(a) for a in weights],
        out_specs=[rows] * 10,
        out_shape=[jax.ShapeDtypeStruct((t, w), _F32)] * 10,
        compiler_params=pltpu.CompilerParams(
            dimension_semantics=("parallel",), vmem_limit_bytes=48 * 1024 * 1024),
        name="rw_features",
    )(f_tok, vf_tok, vec, *weights)


def rwkv_branch(p_rw, vf_tok, rw, n_batch, n_ctx, n_lat):
    n_c = n_batch * n_ctx
    value_residual = vf_tok is not None
    f_tok = jnp.concatenate(
        [centred_token_shift(p.reshape(n_batch, n, -1), rw['mu']).reshape(n_batch * n, -1)
         for p, n in ((p_rw[:n_c], n_ctx), (p_rw[n_c:], n_lat))], axis=0)
    v, nkk, g, bonus, lw0, kd0, b0, lw1, kd1, b1 = rw_features(f_tok, vf_tok, rw, value_residual)
    sum_fwd = rw_summaries(f_tok, lw0, kd0, v, nkk, b0, False)
    sum_bwd = rw_summaries(f_tok, lw1, kd1, v, nkk, b1, True)
    y_fwd, y_bwd = rw_state_pass(sum_fwd, sum_bwd, n_batch, n_ctx, n_lat)
    return y_fwd, y_bwd, bonus, g, v


PROJ_ROWS = 512
MERGE_ROWS = 256
MIXER_COLS = sum(IN_WIDTHS[:-1])
PROJ_GROUPS = tuple((sum(IN_WIDTHS[:j]), IN_WIDTHS[j]) for j in range(len(IN_WIDTHS) - 1))
VMEM_LIMIT_BIG = 56 * 1024 * 1024


def _rms_mod(x, gain, shift, scale):
    y = x * lax.rsqrt(jnp.mean(x * x, axis=-1, keepdims=True) + NORM_EPS) * gain
    return y * (1 + scale) + shift


def _in_proj_kernel(idx_ref, x_ref, mod_ref, gain_ref, *refs):
    del idx_ref
    d = x_ref.shape[-1]
    n_out = len(refs) // 2
    mod = mod_ref[0]
    h = _rms_mod(x_ref[...], gain_ref[...], mod[:, 0:d], mod[:, d:2 * d]).astype(_BF)
    for w_ref, o_ref in zip(refs[:n_out], refs[n_out:]):
        o_ref[...] = jnp.dot(h, w_ref[...], preferred_element_type=_F32)


def fused_in_proj(x_tok, tile_mod, mod_tab, gain, w_groups):
    t, d = x_tok.shape
    full = lambda i, idx: (0, 0)
    grid_spec = pltpu.PrefetchScalarGridSpec(
        num_scalar_prefetch=1,
        grid=(t // PROJ_ROWS,),
        in_specs=[pl.BlockSpec((PROJ_ROWS, d), lambda i, idx: (i, 0)),
                  pl.BlockSpec((1, 1, mod_tab.shape[-1]), lambda i, idx: (idx[i], 0, 0)),
                  pl.BlockSpec((1, d), full)]
        + [pl.BlockSpec(w.shape, full) for w in w_groups],
        out_specs=[pl.BlockSpec((PROJ_ROWS, w.shape[1]), lambda i, idx: (i, 0)) for w in w_groups])
    return pl.pallas_call(
        _in_proj_kernel,
        grid_spec=grid_spec,
        out_shape=[jax.ShapeDtypeStruct((t, w.shape[1]), _F32) for w in w_groups],
        compiler_params=pltpu.CompilerParams(
            dimension_semantics=("parallel",), vmem_limit_bytes=VMEM_LIMIT_BIG),
        name="in_proj",
    )(tile_mod, x_tok, mod_tab, gain[None], *w_groups)


def _merge_kernel(idx_ref, x_ref, mod_ref, gain1_ref, gain2_ref, rwf_ref, rwb_ref, rwbonus_ref,
                  rwgate_ref, rwn_ref, yb_ref, yc_ref, yd_ref, wg_ref, wb_ref, wo_ref, xo_ref, f_ref):
    rwy_ref = (rwf_ref[...], rwb_ref[...], rwbonus_ref[...], rwgate_ref[...])
    del idx_ref
    d = x_ref.shape[-1]
    x = x_ref[...]
    mod = mod_ref[0]
    sh1, sc1, g1, sh2, sc2 = (mod[:, j * d:(j + 1) * d] for j in range(5))
    h = _rms_mod(x, gain1_ref[...], sh1, sc1).astype(_BF)
    y = rwy_ref[0] + rwy_ref[1]
    dev = y - _head_sums(y) * (1.0 / HEAD_DIM)
    var = _head_sums(dev * dev) * (1.0 / HEAD_DIM)
    yn = dev * lax.rsqrt(var + RW_GN_EPS) * rwn_ref[0:1, :] + rwn_ref[1:2, :]
    ys = [(yn + rwy_ref[2]) * rwy_ref[3], yb_ref[...], yc_ref[...], yd_ref[...]]
    m = None
    for i, y_i in enumerate(ys):
        gate = jax.nn.sigmoid(jnp.dot(h, wg_ref[:, i * d:(i + 1) * d], preferred_element_type=_F32))
        term = gate * jnp.dot(y_i.astype(_BF), wb_ref[i], preferred_element_type=_F32)
        m = term if m is None else m + term
    x_new = x + g1 * jnp.dot(m.astype(_BF), wo_ref[...], preferred_element_type=_F32)
    xo_ref[...] = x_new
    f_ref[...] = _rms_mod(x_new, gain2_ref[...], sh2, sc2).astype(_BF)


def fused_merge(x_tok, tile_mod, mod_tab, gain1, gain2, rw_parts, rw_norm, ys, w_gate, w_branch, w_out):
    t, d = x_tok.shape
    full2 = lambda i, idx: (0, 0)
    rows = lambda i, idx: (i, 0)
    branch = pl.BlockSpec((MERGE_ROWS, BRANCH_WIDTH), rows)
    grid_spec = pltpu.PrefetchScalarGridSpec(
        num_scalar_prefetch=1,
        grid=(t // MERGE_ROWS,),
        in_specs=[pl.BlockSpec((MERGE_ROWS, d), rows),
                  pl.BlockSpec((1, 1, mod_tab.shape[-1]), lambda i, idx: (idx[i], 0, 0)),
                  pl.BlockSpec((1, d), full2), pl.BlockSpec((1, d), full2)]
        + [branch] * 4 + [pl.BlockSpec(rw_norm.shape, full2)] + [branch] * (N_BRANCH - 1)
        + [pl.BlockSpec(w_gate.shape, full2),
           pl.BlockSpec(w_branch.shape, lambda i, idx: (0, 0, 0)),
           pl.BlockSpec(w_out.shape, full2)],
        out_specs=[pl.BlockSpec((MERGE_ROWS, d), rows), pl.BlockSpec((MERGE_ROWS, d), rows)])
    return pl.pallas_call(
        _merge_kernel,
        grid_spec=grid_spec,
        out_shape=[jax.ShapeDtypeStruct((t, d), _F32), jax.ShapeDtypeStruct((t, d), _BF)],
        compiler_params=pltpu.CompilerParams(
            dimension_semantics=("parallel",), vmem_limit_bytes=VMEM_LIMIT_BIG),
        name="merge_proj",
    )(tile_mod, x_tok, mod_tab, gain1[None], gain2[None], *rw_parts, rw_norm, *ys, w_gate, w_branch, w_out)


FFN_ROWS = 512
FFN_COLS = 512


def _swiglu_accumulate(x_ref, w1_ref, w3_ref, w2_ref, acc_ref):
    @pl.when(pl.program_id(1) == 0)
    def _():
        acc_ref[...] = jnp.zeros_like(acc_ref)

    x = x_ref[...]
    h1 = jnp.dot(x, w1_ref[0], preferred_element_type=_F32)
    h3 = jnp.dot(x, w3_ref[0], preferred_element_type=_F32)
    h = (h1 * jax.nn.sigmoid(h1) * h3).astype(_BF)
    acc_ref[...] += jnp.dot(h, w2_ref[0], preferred_element_type=_F32)


def pack_bf16_pairs(x):
    half = x.shape[-1] // 2
    bits = lax.bitcast_convert_type(x.astype(_F32), jnp.uint32)
    return bits[:, :half] | (bits[:, half:] >> 16)


def _moe_swiglu_kernel(be_ref, tok_ref, src_ref, w1_ref, w3_ref, w2_ref, g_ref, o_ref,
                       acc_ref, rows_ref, x_ref):
    del be_ref

    @pl.when(pl.program_id(1) == 0)
    def _():
        def gather(r, carry):
            rows_ref[pl.ds(r, 1), :] = src_ref[pl.ds(tok_ref[0, 0, r], 1), :]
            return carry

        lax.fori_loop(0, rows_ref.shape[0], gather, 0, unroll=8)
        packed = rows_ref[...]
        hi = lax.bitcast_convert_type(packed & jnp.uint32(0xFFFF0000), _F32)
        lo = lax.bitcast_convert_type(packed << 16, _F32)
        x_ref[...] = jnp.concatenate([hi, lo], axis=1).astype(_BF)

    _swiglu_accumulate(x_ref, w1_ref, w3_ref, w2_ref, acc_ref)

    @pl.when(pl.program_id(1) == pl.num_programs(1) - 1)
    def _():
        o_ref[...] = acc_ref[...] * g_ref[...]


def _ffn_residual_kernel(idx_ref, f_ref, x_ref, mod_ref, w1_ref, w3_ref, w2_ref, o_ref, acc_ref):
    del idx_ref
    _swiglu_accumulate(f_ref, w1_ref, w3_ref, w2_ref, acc_ref)

    @pl.when(pl.program_id(1) == pl.num_programs(1) - 1)
    def _():
        d = x_ref.shape[-1]
        o_ref[...] = x_ref[...] + mod_ref[0][:, 5 * d:6 * d] * acc_ref[...]


def ffn_residual(f_tok, x_tok, tile_mod, mod_tab, w1, w3, w2):
    t, d = x_tok.shape
    n_f = w1.shape[-1]
    rows = lambda i, f, idx: (i, 0)
    grid_spec = pltpu.PrefetchScalarGridSpec(
        num_scalar_prefetch=1,
        grid=(t // PROJ_ROWS, n_f // FFN_COLS),
        in_specs=[pl.BlockSpec((PROJ_ROWS, d), rows), pl.BlockSpec((PROJ_ROWS, d), rows),
                  pl.BlockSpec((1, 1, mod_tab.shape[-1]), lambda i, f, idx: (idx[i], 0, 0)),
                  pl.BlockSpec((1, d, FFN_COLS), lambda i, f, idx: (0, 0, f)),
                  pl.BlockSpec((1, d, FFN_COLS), lambda i, f, idx: (0, 0, f)),
                  pl.BlockSpec((1, FFN_COLS, d), lambda i, f, idx: (0, f, 0))],
        out_specs=pl.BlockSpec((PROJ_ROWS, d), rows),
        scratch_shapes=[pltpu.VMEM((PROJ_ROWS, d), _F32)])
    return pl.pallas_call(
        _ffn_residual_kernel,
        grid_spec=grid_spec,
        out_shape=jax.ShapeDtypeStruct((t, d), _F32),
        compiler_params=pltpu.CompilerParams(
            dimension_semantics=("parallel", "arbitrary"),
            vmem_limit_bytes=48 * 1024 * 1024),
        name="ffn_residual",
    )(tile_mod, f_tok, x_tok, mod_tab, w1.astype(_BF)[None], w3.astype(_BF)[None], w2.astype(_BF)[None])


def grouped_swiglu(h, tok, block_e, gate, w1, w3, w2):
    n_rows = tok.shape[0]
    d = h.shape[1]
    n_f = w1.shape[-1]
    assert n_rows % FFN_ROWS == 0 and n_f % FFN_COLS == 0
    src = pack_bf16_pairs(h)
    grid_spec = pltpu.PrefetchScalarGridSpec(
        num_scalar_prefetch=1,
        grid=(n_rows // FFN_ROWS, n_f // FFN_COLS),
        in_specs=[pl.BlockSpec((1, 1, FFN_ROWS), lambda i, f, be: (i, 0, 0), memory_space=pltpu.SMEM),
                  pl.BlockSpec(src.shape, lambda i, f, be: (0, 0), pipeline_mode=pl.Buffered(1)),
                  pl.BlockSpec((1, d, FFN_COLS), lambda i, f, be: (be[i], 0, f)),
                  pl.BlockSpec((1, d, FFN_COLS), lambda i, f, be: (be[i], 0, f)),
                  pl.BlockSpec((1, FFN_COLS, d), lambda i, f, be: (be[i], f, 0)),
                  pl.BlockSpec((FFN_ROWS, 1), lambda i, f, be: (i, 0))],
        out_specs=pl.BlockSpec((FFN_ROWS, d), lambda i, f, be: (i, 0)),
        scratch_shapes=[pltpu.VMEM((FFN_ROWS, d), _F32),
                        pltpu.VMEM((FFN_ROWS, d // 2), jnp.uint32),
                        pltpu.VMEM((FFN_ROWS, d), _BF)])
    return pl.pallas_call(
        _moe_swiglu_kernel,
        grid_spec=grid_spec,
        out_shape=jax.ShapeDtypeStruct((n_rows, d), _F32),
        compiler_params=pltpu.CompilerParams(
            dimension_semantics=("arbitrary", "arbitrary"), vmem_limit_bytes=VMEM_LIMIT_BIG),
        name="grouped_swiglu",
    )(block_e, tok.reshape(n_rows // FFN_ROWS, 1, FFN_ROWS), src,
      w1.astype(_BF), w3.astype(_BF), w2.astype(_BF), gate)


def moe_swiglu(h, w_router, w1, w3, w2):
    n_tok, d = h.shape
    logits = pmm(h, w_router)
    top_logit, top_idx = lax.top_k(logits, TOP_K)
    gates = jax.nn.softmax(top_logit, axis=-1)
    n_assign = n_tok * TOP_K
    flat_e = top_idx.reshape(-1)
    order = jnp.argsort(flat_e)
    sorted_e = flat_e[order]
    sorted_tok = (order // TOP_K).astype(jnp.int32)
    sorted_gate = gates.reshape(-1)[order]
    counts = jnp.bincount(flat_e, length=N_EXPERTS)
    padded = (counts + FFN_ROWS - 1) // FFN_ROWS * FFN_ROWS
    start = jnp.cumsum(counts) - counts
    pend = jnp.cumsum(padded)
    pstart = pend - padded
    dest = pstart[sorted_e] + jnp.arange(n_assign) - start[sorted_e]
    n_blocks = -(-n_assign // FFN_ROWS) + N_EXPERTS
    n_rows = n_blocks * FFN_ROWS
    tok = jnp.zeros((n_rows,), jnp.int32).at[dest].set(sorted_tok)
    gate = jnp.zeros((n_rows,), _F32).at[dest].set(sorted_gate)
    block_e = jnp.minimum(jnp.searchsorted(pend, jnp.arange(n_blocks) * FFN_ROWS, side='right'),
                          N_EXPERTS - 1).astype(jnp.int32)
    yb = grouped_swiglu(h, tok, block_e, gate[:, None], w1, w3, w2)
    return jnp.zeros((n_tok, d), _F32).at[tok].add(yb)


def kernel(x, c, ctx, c_ctx, w_mod, b_mod, norm1, norm2, w_in, w_branch, w_out,
           rw_mu, rw_w0, rw_w1, rw_w2, rw_a0, rw_a1, rw_a2, rw_v0, rw_v1, rw_v2,
           rw_g1, rw_g2, rw_k_k, rw_k_a, rw_r_k, rw_ln_w, rw_ln_b, sw_sink,
           mla_q_norm, mla_w_uq, mla_kv_norm, mla_w_ukv, ga_q_norm, ga_k_norm,
           ffn_w1, ffn_w3, ffn_w2, moe_router, moe_w1, moe_w3, moe_w2, final_norm):
    b, s_len, d = x.shape
    n_ctx_tok = ctx.shape[1]
    n_c = b * n_ctx_tok
    assert n_c % PROJ_ROWS == 0 and s_len % PROJ_ROWS == 0
    rope64 = axial_rope_tables(s_len, HEAD_DIM)
    rope32 = axial_rope_tables(s_len, MLA_ROPE)
    silu_c = jax.nn.silu(c)
    silu_cc = jax.nn.silu(c_ctx)
    x_tok = jnp.concatenate([ctx.reshape(n_c, d), x.reshape(b * s_len, d)], axis=0)
    seg_rows = [n_c] + [s_len] * b

    def tile_table(rows):
        return jnp.concatenate([jnp.full((n // rows,), i, jnp.int32) for i, n in enumerate(seg_rows)])

    tile_mod_proj, tile_mod_merge = tile_table(PROJ_ROWS), tile_table(MERGE_ROWS)

    def split_tokens(t):
        return t[:n_c].reshape(b, n_ctx_tok, -1), t[n_c:].reshape(b, s_len, -1)

    vf_tok = None
    for l in range(DEPTH):
        ctx_out = l < DEPTH - 1
        mod_l = silu_c @ w_mod[l] + b_mod[l]
        mod_c = silu_cc @ w_mod[l] + b_mod[l]
        mod_tab = jnp.concatenate([mod_c[None], mod_l], axis=0)[:, None, :]

        w_groups = []
        for off, width in PROJ_GROUPS:
            w = w_in[l][:, off:off + width].astype(_BF)
            w_groups.append(jnp.pad(w, ((0, 0), (0, _round_up(width, LANES) - width))))
        proj = fused_in_proj(x_tok, tile_mod_proj, mod_tab, norm1[l], w_groups)
        p_rw = proj[0]
        pc, pl_ = [None], [None]
        for p, (_, width) in zip(proj[1:], PROJ_GROUPS[1:]):
            p_ctx, p_lat = split_tokens(p[:, :width])
            pc.append(p_ctx)
            pl_.append(p_lat)
        rw = dict(mu=rw_mu[l], w0=rw_w0[l], w1=rw_w1[l], w2=rw_w2[l],
                  a0=rw_a0[l], a1=rw_a1[l], a2=rw_a2[l],
                  v0=rw_v0[l - 1] if l > 0 else None,
                  v1=rw_v1[l - 1] if l > 0 else None,
                  v2=rw_v2[l - 1] if l > 0 else None,
                  g1=rw_g1[l], g2=rw_g2[l], k_k=rw_k_k[l], k_a=rw_k_a[l], r_k=rw_r_k[l],
                  ln_w=rw_ln_w[l], ln_b=rw_ln_b[l])
        rw_yf, rw_yb, rw_bonus, rw_gate, rw_v = rwkv_branch(p_rw, vf_tok, rw, b, n_ctx_tok, s_len)
        if l == 0:
            vf_tok = rw_v
        yb_c, yb_l = window_mixer(pc[1:4], pl_[1:4], sw_sink[l], rope64, ctx_out)
        yc_c, yc_l = mla_mixer(pc[4:7], pl_[4:7], mla_q_norm[l], mla_w_uq[l], mla_kv_norm[l],
                               mla_w_ukv[l], rope32, ctx_out)
        yd_c, yd_l = grid_attention_mixer(pc[7:10], pl_[7:10], ga_q_norm[l], ga_k_norm[l], rope64, ctx_out)

        def tokens_of(y_ctx, y_lat):
            y_ctx = jnp.zeros((n_c, BRANCH_WIDTH), _F32) if y_ctx is None else y_ctx.reshape(n_c, BRANCH_WIDTH)
            return jnp.concatenate([y_ctx, y_lat.reshape(b * s_len, BRANCH_WIDTH)], axis=0)

        ys = [tokens_of(yc_, yl_) for yc_, yl_ in ((yb_c, yb_l), (yc_c, yc_l), (yd_c, yd_l))]
        x_tok, f_tok = fused_merge(x_tok, tile_mod_merge, mod_tab, norm1[l], norm2[l],
                                   (rw_yf, rw_yb, rw_bonus, rw_gate),
                                   jnp.stack([rw_ln_w[l], rw_ln_b[l]]), ys,
                                   w_in[l][:, MIXER_COLS:].astype(_BF), w_branch[l].astype(_BF),
                                   w_out[l].astype(_BF))
        if l % 2 == 0:
            x_tok = ffn_residual(f_tok, x_tok, tile_mod_proj, mod_tab,
                                 ffn_w1[l // 2], ffn_w3[l // 2], ffn_w2[l // 2])
        else:
            first = 0 if ctx_out else n_c
            ffn_out = moe_swiglu(f_tok[first:], moe_router[l // 2], moe_w1[l // 2], moe_w3[l // 2],
                                 moe_w2[l // 2])
            pieces, row = [], 0
            for i, n in enumerate(seg_rows):
                if row >= first:
                    g2 = mod_tab[i, :, 5 * d:]
                    pieces.append(x_tok[row:row + n] + g2 * ffn_out[row - first:row - first + n])
                else:
                    pieces.append(x_tok[row:row + n])
                row += n
            x_tok = jnp.concatenate(pieces, axis=0)
    return rms_norm(x_tok[n_c:].reshape(b, s_len, d), final_norm)
```
